```python
import math
import jax, jax.numpy as jnp
from jax import lax
import numpy as np

D_MODEL = 1024
BATCH = 2
SEQ = 8192
DEPTH = 4

HEAD_DIM = 64
HEADS_PER_GROUP = 4
DILATION_GROUPS = ((128, 1), (512, 4), (2048, 16))
N_GROUPS = len(DILATION_GROUPS)
N_ATTN_HEADS = N_GROUPS * HEADS_PER_GROUP
ATTN_WIDTH = N_ATTN_HEADS * HEAD_DIM
ATTN_OUT_WIDTH = HEADS_PER_GROUP * HEAD_DIM
QUERY_BLOCK = 128
CONV_WIDTH = D_MODEL // 2
CONV_KERNEL = 31
N_BRANCHES = 2
SPLIT_POINTS = (ATTN_WIDTH, 2 * ATTN_WIDTH, 3 * ATTN_WIDTH, 3 * ATTN_WIDTH + 2 * CONV_WIDTH)
IN_COLS = 3 * ATTN_WIDTH + 2 * CONV_WIDTH + N_BRANCHES * D_MODEL
N_BUCKETS = 32
MAX_DISTANCE = 2048
N_EXPERTS = 16
N_EXPERT_GROUPS = 4
EXPERTS_PER_GROUP = N_EXPERTS // N_EXPERT_GROUPS
TOP_K = 2
D_FF_EXPERT = D_MODEL
MOE_BLOCK = 128
ALPHA = (2 * DEPTH) ** 0.25
BETA = (8 * DEPTH) ** -0.25
LN_EPS = 1e-5

kernel_name = "hybrid_dilated_attn_conformer_grouped_moe_deepnorm"


def layer_norm(x, g, b):
    xf = x.astype(jnp.float32)
    mu = xf.mean(-1, keepdims=True)
    var = jnp.square(xf - mu).mean(-1, keepdims=True)
    y = (xf - mu) * lax.rsqrt(var + LN_EPS) * g.astype(jnp.float32) + b.astype(jnp.float32)
    return y.astype(x.dtype)


def t5_bucket(dist):
    max_exact = N_BUCKETS // 2
    n = jnp.maximum(dist, 0)
    nf = jnp.maximum(n, 1).astype(jnp.float32)
    large = max_exact + (jnp.log(nf / max_exact) / math.log(MAX_DISTANCE / max_exact)
                         * (N_BUCKETS - max_exact)).astype(jnp.int32)
    large = jnp.minimum(large, N_BUCKETS - 1)
    return jnp.where(n < max_exact, n, large)


def dilated_group_attention(q, k, v, rel_bias, window, dilation):
    B, S, H, dh = q.shape
    r = dilation
    L = S // r
    W = window // r
    bq = math.gcd(L, QUERY_BLOCK)
    nb = L // bq
    bk = bq + W
    qb = q.reshape(B, nb, bq, r, H, dh)
    pad = ((0, 0), (W, 0), (0, 0), (0, 0), (0, 0))
    kp = jnp.pad(k.reshape(B, L, r, H, dh), pad)
    vp = jnp.pad(v.reshape(B, L, r, H, dh), pad)
    kidx = jnp.arange(nb)[:, None] * bq + jnp.arange(bk)[None, :]
    kb = kp[:, kidx]
    vb = vp[:, kidx]
    qi = jnp.arange(bq)[:, None]
    kj = jnp.arange(bk)[None, :]
    dist = qi + W - kj
    key_pos = (jnp.arange(nb) * bq)[:, None, None] + kj[None] - W
    valid = (dist >= 0) & (dist <= W) & (key_pos >= 0)
    bias = jnp.moveaxis(rel_bias[t5_bucket(dist * r)], -1, 0)
    s = jnp.einsum('bnqchd,bnkchd->bnchqk', qb, kb).astype(jnp.float32) * (dh ** -0.5)
    s = s + bias.astype(jnp.float32)
    s = jnp.where(valid[None, :, None, None], s, -jnp.inf)
    m = s.max(-1, keepdims=True)
    p = jnp.exp(s - m)
    den = p.sum(-1)
    o = jnp.einsum('bnchqk,bnkchd->bnqchd', p / den[..., None], vb.astype(jnp.float32))
    o = o.reshape(B, S, H, dh)
    lse = m[..., 0] + jnp.log(den)
    lse = jnp.transpose(lse, (0, 1, 4, 2, 3)).reshape(B, S, H)
    return o, lse


def conformer_conv(u, conv_w, conv_b, ln_g, ln_b, w_out):
    a, gate = jnp.split(u, 2, axis=-1)
    h = a * jax.nn.sigmoid(gate)
    h = lax.conv_general_dilated(h, conv_w[:, None, :], window_strides=(1,),
                                 padding=[(CONV_KERNEL - 1, 0)],
                                 dimension_numbers=('NWC', 'WIO', 'NWC'),
                                 feature_group_count=CONV_WIDTH) + conv_b
    h = jax.nn.silu(layer_norm(h, ln_g, ln_b))
    return h @ w_out


def gated_mixer(x, w_in, rel_bias, conv_w, conv_b, conv_ln_g, conv_ln_b, w_attn_out, w_conv_out, w_o):
    B, S, _ = x.shape
    proj = x @ w_in
    q, k, v, u, g = jnp.split(proj, list(SPLIT_POINTS), axis=-1)
    q = q.reshape(B, S, N_ATTN_HEADS, HEAD_DIM)
    k = k.reshape(B, S, N_ATTN_HEADS, HEAD_DIM)
    v = v.reshape(B, S, N_ATTN_HEADS, HEAD_DIM)
    outs, lses = [], []
    for gi, (window, dilation) in enumerate(DILATION_GROUPS):
        hs = slice(gi * HEADS_PER_GROUP, (gi + 1) * HEADS_PER_GROUP)
        o, lse = dilated_group_attention(q[:, :, hs], k[:, :, hs], v[:, :, hs],
                                         rel_bias[:, hs], window, dilation)
        outs.append(o)
        lses.append(lse)
    mix_w = jax.nn.softmax(jnp.stack(lses), axis=0)
    attn = jnp.einsum('gbsh,gbshd->bshd', mix_w, jnp.stack(outs))
    attn = attn.reshape(B, S, ATTN_OUT_WIDTH).astype(x.dtype)
    attn_branch = attn @ w_attn_out
    conv_branch = conformer_conv(u, conv_w, conv_b, conv_ln_g, conv_ln_b, w_conv_out)
    gates = jax.nn.sigmoid(g.astype(jnp.float32)).astype(x.dtype).reshape(B, S, N_BRANCHES, D_MODEL)
    merged = gates[:, :, 0] * attn_branch + gates[:, :, 1] * conv_branch
    return merged @ w_o


def grouped_moe(x2d, router_w, w_gu, w_dn):
    N, D = x2d.shape
    probs = jax.nn.softmax((x2d @ router_w).astype(jnp.float32), axis=-1)
    grp = probs.reshape(N, N_EXPERT_GROUPS, EXPERTS_PER_GROUP)
    group_score = lax.top_k(grp, TOP_K)[0].sum(-1)
    g_sel = jnp.argmax(group_score, axis=-1)
    in_group = jnp.take_along_axis(grp, g_sel[:, None, None], axis=1)[:, 0]
    top_vals, top_local = lax.top_k(in_group, TOP_K)
    expert_idx = g_sel[:, None] * EXPERTS_PER_GROUP + top_local
    gate_w = top_vals / top_vals.sum(-1, keepdims=True)
    n_slots = N * TOP_K
    e_flat = expert_idx.reshape(-1).astype(jnp.int32)
    tok = jnp.arange(n_slots) // TOP_K
    counts = jnp.zeros((N_EXPERTS,), jnp.int32).at[e_flat].add(1)
    padded = (counts + MOE_BLOCK - 1) // MOE_BLOCK * MOE_BLOCK
    pad_end = jnp.cumsum(padded)
    pad_start = pad_end - padded
    cnt_start = jnp.cumsum(counts) - counts
    order = jnp.argsort(e_flat, stable=True)
    sorted_e = e_flat[order]
    rank = jnp.arange(n_slots) - cnt_start[sorted_e]
    dest = jnp.zeros((n_slots,), jnp.int32).at[order].set((pad_start[sorted_e] + rank).astype(jnp.int32))
    n_blocks = -(-(n_slots + N_EXPERTS * (MOE_BLOCK - 1)) // MOE_BLOCK)
    buf = jnp.zeros((n_blocks * MOE_BLOCK, D), x2d.dtype).at[dest].set(x2d[tok])
    block_e = jnp.minimum(jnp.searchsorted(pad_end, jnp.arange(n_blocks) * MOE_BLOCK, side='right'),
                          N_EXPERTS - 1)

    def expert_block(args):
        xb, e = args
        a, b = jnp.split(xb @ w_gu[e], 2, axis=-1)
        return (jax.nn.silu(a) * b) @ w_dn[e]

    y_buf = lax.map(expert_block, (buf.reshape(n_blocks, MOE_BLOCK, D), block_e)).reshape(-1, D)
    y_slots = y_buf[dest].reshape(N, TOP_K, D)
    return jnp.einsum('nk,nkd->nd', gate_w.astype(x2d.dtype), y_slots)


def setup_inputs(seed: int = 0) -> dict:
    key = jax.random.key(seed)
    ks = jax.random.split(key, 18)
    f32 = jnp.float32

    def nrm(k, shape, fan_in, scale=1.0):
        return jax.random.normal(k, shape, f32) * (scale * fan_in ** -0.5)

    def gain(k, shape):
        return 1.0 + 0.02 * jax.random.normal(k, shape, f32)

    def small(k, shape):
        return 0.02 * jax.random.normal(k, shape, f32)

    x = jax.random.normal(ks[0], (BATCH, SEQ, D_MODEL), f32)
    w_in = nrm(ks[1], (DEPTH, D_MODEL, IN_COLS), D_MODEL)
    w_in = w_in.at[..., 2 * ATTN_WIDTH:3 * ATTN_WIDTH].multiply(BETA)
    return {
        "x": x,
        "w_in": w_in,
        "w_attn_out": nrm(ks[2], (DEPTH, ATTN_OUT_WIDTH, D_MODEL), ATTN_OUT_WIDTH),
        "w_conv_out": nrm(ks[3], (DEPTH, CONV_WIDTH, D_MODEL), CONV_WIDTH),
        "w_o": nrm(ks[4], (DEPTH, D_MODEL, D_MODEL), D_MODEL, BETA),
        "conv_w": nrm(ks[5], (DEPTH, CONV_KERNEL, CONV_WIDTH), CONV_KERNEL),
        "conv_b": small(ks[6], (DEPTH, CONV_WIDTH)),
        "conv_ln_g": gain(ks[7], (DEPTH, CONV_WIDTH)),
        "conv_ln_b": small(ks[8], (DEPTH, CONV_WIDTH)),
        "ln_mix_g": gain(ks[9], (DEPTH, D_MODEL)),
        "ln_mix_b": small(ks[10], (DEPTH, D_MODEL)),
        "expert_w_gate_up": nrm(ks[11], (DEPTH, N_EXPERTS, D_MODEL, 2 * D_FF_EXPERT), D_MODEL, BETA),
        "expert_w_down": nrm(ks[12], (DEPTH, N_EXPERTS, D_FF_EXPERT, D_MODEL), D_FF_EXPERT, BETA),
        "ln_ffn_g": gain(ks[13], (DEPTH, D_MODEL)),
        "ln_ffn_b": small(ks[14], (DEPTH, D_MODEL)),
        "router_w": nrm(ks[15], (D_MODEL, N_EXPERTS), D_MODEL),
        "rel_bias": 0.5 * jax.random.normal(ks[16], (N_BUCKETS, N_ATTN_HEADS), f32),
    }


def reference(x, w_in, w_attn_out, w_conv_out, w_o, conv_w, conv_b, conv_ln_g, conv_ln_b,
              ln_mix_g, ln_mix_b, expert_w_gate_up, expert_w_down, ln_ffn_g, ln_ffn_b,
              router_w, rel_bias):
    B, S, D = x.shape
    for l in range(DEPTH):
        h = gated_mixer(x, w_in[l], rel_bias, conv_w[l], conv_b[l], conv_ln_g[l], conv_ln_b[l],
                        w_attn_out[l], w_conv_out[l], w_o[l])
        x = layer_norm(ALPHA * x + h, ln_mix_g[l], ln_mix_b[l])
        m = grouped_moe(x.reshape(B * S, D), router_w, expert_w_gate_up[l], expert_w_down[l])
        x = layer_norm(ALPHA * x + m.reshape(B, S, D), ln_ffn_g[l], ln_ffn_b[l])
    return x
```

```python
import functools
import math

import jax
import jax.numpy as jnp
from jax import lax
from jax.experimental import pallas as pl
from jax.experimental.pallas import tpu as pltpu

F32 = jnp.float32
BF16 = jnp.bfloat16

HEAD_DIM = 64
HEADS_PER_GROUP = 4
GROUP_WIDTH = HEADS_PER_GROUP * HEAD_DIM
DILATION_GROUPS = ((128, 1), (512, 4), (2048, 16))
N_GROUPS = len(DILATION_GROUPS)
ATTN_WIDTH = N_GROUPS * GROUP_WIDTH
BAND = 128
CONV_KERNEL = 31
CONV_HALO = 32
N_BUCKETS = 32
MAX_DISTANCE = 2048
N_EXPERTS = 16
EXPERTS_PER_GROUP = 4
LN_EPS = 1e-5
MASKED = -1e30

ROW_BLOCK = 256
TOKEN_TILE = 256
VMEM_LIMIT = 56 * 1024 * 1024


def _sigmoid(v):
    return 1.0 / (1.0 + jnp.exp(-v))


def _layer_norm(z, g, b):
    mu = jnp.mean(z, axis=-1, keepdims=True)
    zc = z - mu
    var = jnp.mean(zc * zc, axis=-1, keepdims=True)
    return zc * lax.rsqrt(var + LN_EPS) * g + b


def _params(*sem):
    return pltpu.CompilerParams(dimension_semantics=sem, vmem_limit_bytes=VMEM_LIMIT)


def _mm_kernel(x_ref, w_ref, o_ref):
    o_ref[...] = jnp.dot(x_ref[...], w_ref[...], preferred_element_type=F32).astype(o_ref.dtype)


def _project(xb, w, tm, tn):
    n, d = xb.shape
    c = w.shape[1]
    return pl.pallas_call(
        _mm_kernel,
        grid=(n // tm, c // tn),
        in_specs=[pl.BlockSpec((tm, d), lambda i, j: (i, 0)),
                  pl.BlockSpec((d, tn), lambda i, j: (0, j))],
        out_specs=pl.BlockSpec((tm, tn), lambda i, j: (i, j)),
        out_shape=jax.ShapeDtypeStruct((n, c), BF16),
        compiler_params=_params("arbitrary", "arbitrary"),
        name="proj_natural",
    )(xb, w)


def _project_dilated(xb, w, batch, seq, r):
    n, d = xb.shape
    c = w.shape[1]
    sub = seq // r
    tm = min(512, sub)
    nl = sub // tm
    xv = xb.reshape(batch * sub, r * d)
    return pl.pallas_call(
        _mm_kernel,
        grid=(batch, r, nl),
        in_specs=[pl.BlockSpec((tm, d), lambda b, cc, j: (b * nl + j, cc)),
                  pl.BlockSpec((d, c), lambda b, cc, j: (0, 0))],
        out_specs=pl.BlockSpec((tm, c), lambda b, cc, j: ((b * r + cc) * nl + j, 0)),
        out_shape=jax.ShapeDtypeStruct((n, c), BF16),
        compiler_params=_params("arbitrary", "arbitrary", "arbitrary"),
        name=f"proj_dilated_{r}",
    )(xv, w)


def _attn_kernel(q_ref, kp_ref, vp_ref, kc_ref, vc_ref, bias_ref, o_ref, lse_ref, k_scr, v_scr, *, nqb):
    i = pl.program_id(1)
    k_scr[0:BAND, :] = kp_ref[0]
    k_scr[BAND:, :] = kc_ref[0]
    v_scr[0:BAND, :] = vp_ref[0]
    v_scr[BAND:, :] = vc_ref[0]
    col = lax.broadcasted_iota(jnp.int32, (BAND, 2 * BAND), 1)

    def body(j, carry):
        r0 = pl.multiple_of(j * BAND, BAND)
        q = q_ref[0, pl.ds(r0, BAND), :]
        kk = k_scr[pl.ds(r0, 2 * BAND), :]
        vv = v_scr[pl.ds(r0, 2 * BAND), :]
        no_prev = jnp.logical_and(jnp.logical_and(i == 0, j == 0), col < BAND)
        outs, lses = [], []
        for h in range(HEADS_PER_GROUP):
            sl = slice(h * HEAD_DIM, (h + 1) * HEAD_DIM)
            s = lax.dot_general(q[:, sl], kk[:, sl], (((1,), (1,)), ((), ())),
                                preferred_element_type=F32)
            s = s + bias_ref[h]
            s = jnp.where(no_prev, MASKED, s)
            m = jnp.max(s, axis=-1, keepdims=True)
            p = jnp.exp(s - m)
            den = jnp.sum(p, axis=-1, keepdims=True)
            o = jnp.dot(p.astype(BF16), vv[:, sl], preferred_element_type=F32)
            outs.append(o * (1.0 / den))
            lses.append(jnp.broadcast_to(m + jnp.log(den), (BAND, HEAD_DIM)))
        o_ref[0, pl.ds(r0, BAND), :] = jnp.concatenate(outs, axis=1).astype(o_ref.dtype)
        lse_ref[0, pl.ds(r0, BAND), :] = jnp.concatenate(lses, axis=1)
        return carry

    lax.fori_loop(0, nqb, body, 0)


def _attention(qkv, bias, batch, seq, r, col0):
    sub = seq // r
    nqb = min(8, sub // BAND)
    rows = nqb * BAND
    nt = sub // rows
    gw = GROUP_WIDTH

    def cur(c):
        return pl.BlockSpec((1, rows, gw), lambda bc, i: (bc, i, c))

    def prev(c):
        return pl.BlockSpec((1, BAND, gw), lambda bc, i: (bc, jnp.maximum(i * nqb - 1, 0), c))

    out_spec = pl.BlockSpec((1, rows, gw), lambda bc, i: (bc // r, i, bc % r))
    o, lse = pl.pallas_call(
        functools.partial(_attn_kernel, nqb=nqb),
        grid=(batch * r, nt),
        in_specs=[cur(col0), prev(col0 + 1), prev(col0 + 2), cur(col0 + 1), cur(col0 + 2),
                  pl.BlockSpec((HEADS_PER_GROUP, BAND, 2 * BAND), lambda bc, i: (0, 0, 0))],
        out_specs=[out_spec, out_spec],
        out_shape=[jax.ShapeDtypeStruct((batch, sub, r * gw), BF16),
                   jax.ShapeDtypeStruct((batch, sub, r * gw), F32)],
        scratch_shapes=[pltpu.VMEM((rows + BAND, gw), BF16), pltpu.VMEM((rows + BAND, gw), BF16)],
        compiler_params=_params("arbitrary", "arbitrary"),
        name=f"attn_dilation_{r}",
    )(qkv, qkv, qkv, qkv, qkv, bias)
    return o.reshape(batch * seq, gw), lse.reshape(batch * seq, gw)


def _t5_bucket(dist):
    max_exact = N_BUCKETS // 2
    n = jnp.maximum(dist, 0)
    nf = jnp.maximum(n, 1).astype(F32)
    large = max_exact + (jnp.log(nf / max_exact) / math.log(MAX_DISTANCE / max_exact)
                         * (N_BUCKETS - max_exact)).astype(jnp.int32)
    large = jnp.minimum(large, N_BUCKETS - 1)
    return jnp.where(n < max_exact, n, large)


def _band_bias(rel_bias, gi, r):
    qi = jnp.arange(BAND)[:, None]
    kj = jnp.arange(2 * BAND)[None, :]
    dist = qi + BAND - kj
    hs = slice(gi * HEADS_PER_GROUP, (gi + 1) * HEADS_PER_GROUP)
    b = jnp.moveaxis(rel_bias[:, hs][_t5_bucket(dist * r)], -1, 0).astype(F32)
    valid = (dist >= 0) & (dist <= BAND)
    return jnp.where(valid[None], b, MASKED)


def _post_kernel(o0_ref, o1_ref, o2_ref, l0_ref, l1_ref, l2_ref, u_ref, uh_ref, g_ref, x_ref,
                 wao_ref, cw_ref, cb_ref, clg_ref, clb_ref, wco_ref, wo_ref, lng_ref, lnb_ref, rwt_ref,
                 x1_ref, x1b_ref, idx_ref, gate_ref, rank_ref, cnt_ref,
                 hs_scr, conv_scr, carry_scr, *, tm, seq, alpha):
    i = pl.program_id(0)
    d = x_ref.shape[1]
    c = d // 2

    l0, l1, l2 = l0_ref[...], l1_ref[...], l2_ref[...]
    mx = jnp.maximum(jnp.maximum(l0, l1), l2)
    e0, e1, e2 = jnp.exp(l0 - mx), jnp.exp(l1 - mx), jnp.exp(l2 - mx)
    attn = (e0 * o0_ref[...].astype(F32) + e1 * o1_ref[...].astype(F32)
            + e2 * o2_ref[...].astype(F32)) * (1.0 / (e0 + e1 + e2))
    attn_branch = jnp.dot(attn.astype(BF16), wao_ref[...], preferred_element_type=F32)

    u = u_ref[...].astype(F32)
    uh = uh_ref[...].astype(F32)
    hh = uh[:, :c] * _sigmoid(uh[:, c:])
    seq_start = (i * tm) % seq == 0
    hs_scr[0:CONV_HALO, :] = jnp.where(seq_start, 0.0, hh)
    hs_scr[CONV_HALO:, :] = u[:, :c] * _sigmoid(u[:, c:])
    off = CONV_HALO - (CONV_KERNEL - 1)
    for cc in range(c // 128):
        ls = slice(cc * 128, (cc + 1) * 128)
        acc = jnp.zeros((tm, 128), F32)
        for k in range(CONV_KERNEL):
            acc = acc + hs_scr[off + k:off + k + tm, ls] * cw_ref[k:k + 1, ls]
        conv_scr[:, ls] = acc
    hc = _layer_norm(conv_scr[...] + cb_ref[...], clg_ref[...], clb_ref[...])
    hc = hc * _sigmoid(hc)
    conv_branch = jnp.dot(hc.astype(BF16), wco_ref[...], preferred_element_type=F32)

    gates = _sigmoid(g_ref[...].astype(F32))
    merged = gates[:, :d] * attn_branch + gates[:, d:] * conv_branch
    hmix = jnp.dot(merged.astype(BF16), wo_ref[...], preferred_element_type=F32)
    x1 = _layer_norm(alpha * x_ref[...] + hmix, lng_ref[...], lnb_ref[...])
    x1_ref[...] = x1
    x1b = x1.astype(BF16)
    x1b_ref[...] = x1b

    logits = lax.dot_general(rwt_ref[...], x1b, (((1,), (1,)), ((), ())), preferred_element_type=F32)
    ex = jnp.exp(logits - jnp.max(logits, axis=0, keepdims=True))
    rows = [ex[e:e + 1, :] for e in range(N_EXPERTS)]
    best, gsel = None, None
    for g in range(N_EXPERTS // EXPERTS_PER_GROUP):
        v = rows[g * EXPERTS_PER_GROUP:(g + 1) * EXPERTS_PER_GROUP]
        score = None
        for a in range(EXPERTS_PER_GROUP):
            for b in range(a + 1, EXPERTS_PER_GROUP):
                ps = v[a] + v[b]
                score = ps if score is None else jnp.maximum(score, ps)
        if best is None:
            best, gsel = score, jnp.zeros_like(score, dtype=jnp.int32)
        else:
            upd = score > best
            gsel = jnp.where(upd, g, gsel)
            best = jnp.where(upd, score, best)
    vals = []
    for j in range(EXPERTS_PER_GROUP):
        vj = rows[j]
        for g in range(1, N_EXPERTS // EXPERTS_PER_GROUP):
            vj = jnp.where(gsel == g, rows[g * EXPERTS_PER_GROUP + j], vj)
        vals.append(vj)
    v1, i1 = vals[0], jnp.zeros_like(gsel)
    for j in range(1, EXPERTS_PER_GROUP):
        upd = vals[j] > v1
        i1 = jnp.where(upd, j, i1)
        v1 = jnp.where(upd, vals[j], v1)
    v2, i2 = jnp.full_like(v1, -1.0), jnp.zeros_like(gsel)
    for j in range(EXPERTS_PER_GROUP):
        upd = jnp.logical_and(i1 != j, vals[j] > v2)
        i2 = jnp.where(upd, j, i2)
        v2 = jnp.where(upd, vals[j], v2)
    ea = gsel * EXPERTS_PER_GROUP + i1
    eb = gsel * EXPERTS_PER_GROUP + i2
    inv = 1.0 / (v1 + v2)
    idx_ref[...] = jnp.concatenate([ea, eb], axis=0)
    gate_ref[...] = jnp.concatenate([v1 * inv, v2 * inv], axis=0)

    @pl.when(i == 0)
    def _():
        carry_scr[...] = jnp.zeros_like(carry_scr)

    eid = lax.broadcasted_iota(jnp.int32, (N_EXPERTS, tm), 0)
    hit_a = eid == ea
    hit_b = eid == eb
    onehot = jnp.where(jnp.logical_or(hit_a, hit_b), 1.0, 0.0)
    before = (lax.broadcasted_iota(jnp.int32, (tm, tm), 0)
              < lax.broadcasted_iota(jnp.int32, (tm, tm), 1))
    upper = jnp.where(before, 1.0, 0.0).astype(BF16)
    pos = jnp.dot(onehot.astype(BF16), upper, preferred_element_type=F32) + carry_scr[:, 0:1]
    ra = jnp.sum(jnp.where(hit_a, pos, 0.0), axis=0, keepdims=True)
    rb = jnp.sum(jnp.where(hit_b, pos, 0.0), axis=0, keepdims=True)
    rank_ref[...] = jnp.concatenate([ra, rb], axis=0).astype(jnp.int32)
    carry = carry_scr[...] + jnp.sum(onehot, axis=1, keepdims=True)
    carry_scr[...] = carry
    cnt_ref[...] = carry


def _mixer_tail(o, lse, proj0, x, lw, batch, seq):
    n, d = x.shape
    tm = TOKEN_TILE
    nt = n // tm
    gw = GROUP_WIDTH
    hb = tm // CONV_HALO

    def tok(width):
        return pl.BlockSpec((tm, width), lambda i: (i, 0))

    def const(shape):
        return pl.BlockSpec(shape, lambda i: (0,) * len(shape))

    in_specs = (
        [tok(gw)] * 3 + [tok(gw)] * 3
        + [pl.BlockSpec((tm, d), lambda i: (i, 2)),
           pl.BlockSpec((CONV_HALO, d), lambda i: (jnp.maximum(i * hb - 1, 0), 2)),
           pl.BlockSpec((tm, 2 * d), lambda i: (i, 0)),
           tok(d)]
        + [const(lw[k].shape) for k in ("wao", "cw", "cb", "clg", "clb", "wco", "wo", "lng", "lnb", "rwt")]
    )
    lane_row = pl.BlockSpec((2, tm), lambda i: (0, i))
    out_specs = [tok(d), tok(d), lane_row, lane_row, lane_row, const((N_EXPERTS, 128))]
    out_shape = [jax.ShapeDtypeStruct((n, d), F32), jax.ShapeDtypeStruct((n, d), BF16),
                 jax.ShapeDtypeStruct((2, n), jnp.int32), jax.ShapeDtypeStruct((2, n), F32),
                 jax.ShapeDtypeStruct((2, n), jnp.int32), jax.ShapeDtypeStruct((N_EXPERTS, 128), F32)]
    alpha = lw["alpha"]
    return pl.pallas_call(
        functools.partial(_post_kernel, tm=tm, seq=seq, alpha=alpha),
        grid=(nt,),
        in_specs=in_specs,
        out_specs=out_specs,
        out_shape=out_shape,
        scratch_shapes=[pltpu.VMEM((tm + CONV_HALO, d // 2), F32), pltpu.VMEM((tm, d // 2), F32),
                        pltpu.VMEM((N_EXPERTS, 128), F32)],
        compiler_params=_params("arbitrary"),
        name="mixer_tail",
    )(o[0], o[1], o[2], lse[0], lse[1], lse[2], proj0, proj0, proj0, x,
      *[lw[k] for k in ("wao", "cw", "cb", "clg", "clb", "wco", "wo", "lng", "lnb", "rwt")])


def _ffn_kernel(be_ref, nv_ref, tok_ref, slot_ref, x_hbm, wgu_ref, wdn_ref, y_hbm,
                xbuf, obuf, gsem, ssem, *, nb, blk):
    b = pl.program_id(0)
    cur = b % 2
    f = wdn_ref.shape[2]

    def row_in(blk_id, buf, r):
        t = tok_ref[blk_id * blk + r]
        return pltpu.make_async_copy(x_hbm.at[pl.ds(t, 1)], xbuf.at[buf, pl.ds(r, 1)], gsem.at[buf])

    def row_out(blk_id, buf, r):
        s = slot_ref[blk_id * blk + r]
        return pltpu.make_async_copy(obuf.at[buf, pl.ds(r, 1)], y_hbm.at[pl.ds(s, 1)], ssem.at[buf])

    def gather_start(blk_id, buf):
        def body(r, carry):
            row_in(blk_id, buf, r).start()
            return carry
        lax.fori_loop(0, blk, body, 0, unroll=8)

    def gather_wait(buf):
        pltpu.make_async_copy(x_hbm.at[pl.ds(0, blk)], xbuf.at[buf], gsem.at[buf]).wait()

    def for_valid_rows(blk_id, fn):
        nv = nv_ref[blk_id]
        full = lax.shift_right_logical(nv, 3)

        def chunk(c, carry):
            for u in range(8):
                fn(c * 8 + u)
            return carry

        def tail(r, carry):
            fn(r)
            return carry

        lax.fori_loop(0, full, chunk, 0)
        lax.fori_loop(full * 8, nv, tail, 0)

    def scatter_start(blk_id, buf):
        for_valid_rows(blk_id, lambda r: row_out(blk_id, buf, r).start())

    def scatter_wait(blk_id, buf):
        for_valid_rows(blk_id, lambda r: row_out(blk_id, buf, r).wait())

    @pl.when(b == 0)
    def _():
        gather_start(b, cur)

    @pl.when(b + 1 < nb)
    def _():
        gather_start(b + 1, 1 - cur)

    gather_wait(cur)
    xb = xbuf[cur].astype(BF16)
    h = jnp.dot(xb, wgu_ref[0, 0], preferred_element_type=F32)
    a = h[:, :f]
    act = (a * _sigmoid(a) * h[:, f:]).astype(BF16)
    y = jnp.dot(act, wdn_ref[0, 0], preferred_element_type=F32)

    @pl.when(b >= 2)
    def _():
        scatter_wait(b - 2, cur)

    obuf[cur] = y
    scatter_start(b, cur)

    @pl.when(b == nb - 1)
    def _():
        if nb >= 2:
            scatter_wait(b - 1, 1 - cur)
        scatter_wait(b, cur)


def _expert_ffn(x1, wgu, wdn, layer, block_e, n_valid, tok_of_row, slot_of_row):
    n, d = x1.shape
    nb = block_e.shape[0]
    blk = ROW_BLOCK
    f2 = wgu.shape[3]
    f = wdn.shape[2]
    grid_spec = pltpu.PrefetchScalarGridSpec(
        num_scalar_prefetch=4,
        grid=(nb,),
        in_specs=[pl.BlockSpec(memory_space=pl.ANY),
                  pl.BlockSpec((1, 1, d, f2), lambda b, be, nv, tk, sl: (layer, be[b], 0, 0)),
                  pl.BlockSpec((1, 1, f, d), lambda b, be, nv, tk, sl: (layer, be[b], 0, 0))],
        out_specs=pl.BlockSpec(memory_space=pl.ANY),
        scratch_shapes=[pltpu.VMEM((2, blk, d), F32), pltpu.VMEM((2, blk, d), F32),
                        pltpu.SemaphoreType.DMA((2,)), pltpu.SemaphoreType.DMA((2,))],
    )
    return pl.pallas_call(
        functools.partial(_ffn_kernel, nb=nb, blk=blk),
        grid_spec=grid_spec,
        out_shape=jax.ShapeDtypeStruct((2 * n, d), F32),
        compiler_params=_params("arbitrary"),
        name="expert_ffn",
    )(block_e, n_valid, tok_of_row, slot_of_row, x1, wgu, wdn)


def _merge_kernel(y0_ref, y1_ref, gt_ref, x1_ref, g_ref, b_ref, x2_ref, x2b_ref, *, alpha):
    gt = gt_ref[...]
    m = gt[:, 0:1] * y0_ref[...] + gt[:, 1:2] * y1_ref[...]
    x2 = _layer_norm(alpha * x1_ref[...] + m, g_ref[...], b_ref[...])
    x2_ref[...] = x2
    x2b_ref[...] = x2.astype(BF16)


def _expert_merge(y_tok, gate_t, x1, g, b, alpha):
    n, d = x1.shape
    tm = TOKEN_TILE
    nt = n // tm
    return pl.pallas_call(
        functools.partial(_merge_kernel, alpha=alpha),
        grid=(nt,),
        in_specs=[pl.BlockSpec((tm, d), lambda i: (i, 0)),
                  pl.BlockSpec((tm, d), lambda i: (i + nt, 0)),
                  pl.BlockSpec((tm, 2), lambda i: (i, 0)),
                  pl.BlockSpec((tm, d), lambda i: (i, 0)),
                  pl.BlockSpec((1, d), lambda i: (0, 0)),
                  pl.BlockSpec((1, d), lambda i: (0, 0))],
        out_specs=[pl.BlockSpec((tm, d), lambda i: (i, 0)), pl.BlockSpec((tm, d), lambda i: (i, 0))],
        out_shape=[jax.ShapeDtypeStruct((n, d), F32), jax.ShapeDtypeStruct((n, d), BF16)],
        compiler_params=_params("arbitrary"),
        name="expert_merge",
    )(y_tok, y_tok, gate_t, x1, g, b)


def _routing_tables(idx, rank, counts, n):
    blk = ROW_BLOCK
    n_slots = 2 * n
    nb = -(-(n_slots + N_EXPERTS * (blk - 1)) // blk)
    n_rows = nb * blk
    padded = (counts + blk - 1) // blk * blk
    pad_end = jnp.cumsum(padded)
    pad_start = pad_end - padded
    dest = (pad_start[idx] + rank).reshape(-1)
    slots = jnp.arange(n_slots, dtype=jnp.int32)
    tok_of_row = jnp.zeros((n_rows,), jnp.int32).at[dest].set(slots % n)
    slot_of_row = jnp.zeros((n_rows,), jnp.int32).at[dest].set(slots)
    block_row0 = jnp.arange(nb, dtype=jnp.int32) * blk
    block_e = jnp.minimum(jnp.searchsorted(pad_end, block_row0, side="right"), N_EXPERTS - 1).astype(jnp.int32)
    n_valid = jnp.clip(pad_start[block_e] + counts[block_e] - block_row0, 0, blk).astype(jnp.int32)
    return block_e, n_valid, tok_of_row, slot_of_row


def kernel(x, w_in, w_attn_out, w_conv_out, w_o, conv_w, conv_b, conv_ln_g, conv_ln_b, ln_mix_g, ln_mix_b,
           expert_w_gate_up, expert_w_down, ln_ffn_g, ln_ffn_b, router_w, rel_bias):
    batch, seq, d = x.shape
    depth = w_in.shape[0]
    n = batch * seq
    aw, gw = ATTN_WIDTH, GROUP_WIDTH
    assert d % 256 == 0 and n % 1024 == 0
    for window, r in DILATION_GROUPS:
        assert window // r == BAND and seq % (r * BAND) == 0
    alpha = (2 * depth) ** 0.25
    scale = HEAD_DIM ** -0.5

    biases = [_band_bias(rel_bias, gi, r) for gi, (_, r) in enumerate(DILATION_GROUPS)]
    wgu = expert_w_gate_up.astype(BF16)
    wdn = expert_w_down.astype(BF16)
    rwt = router_w.T.astype(BF16)

    xf = x.reshape(n, d)
    xb = xf.astype(BF16)
    for l in range(depth):
        wl = w_in[l]

        def qkv_cols(gi):
            return jnp.concatenate([wl[:, gi * gw:(gi + 1) * gw] * scale,
                                    wl[:, aw + gi * gw:aw + (gi + 1) * gw],
                                    wl[:, 2 * aw + gi * gw:2 * aw + (gi + 1) * gw]], axis=1)

        w0 = jnp.concatenate([wl[:, 3 * aw + d:], wl[:, 3 * aw:3 * aw + d], qkv_cols(0)], axis=1).astype(BF16)
        proj0 = _project(xb, w0, 1024, 768)
        outs, lses = [], []
        for gi, (_, r) in enumerate(DILATION_GROUPS):
            if r == 1:
                qkv, col0 = proj0.reshape(batch, seq, proj0.shape[1]), (3 * d) // gw
            else:
                qkv = _project_dilated(xb, qkv_cols(gi).astype(BF16), batch, seq, r)
                qkv, col0 = qkv.reshape(batch * r, seq // r, 3 * gw), 0
            o, lse = _attention(qkv, biases[gi], batch, seq, r, col0)
            outs.append(o)
            lses.append(lse)

        lw = dict(wao=w_attn_out[l].astype(BF16), cw=conv_w[l], cb=conv_b[l][None], clg=conv_ln_g[l][None],
                  clb=conv_ln_b[l][None], wco=w_conv_out[l].astype(BF16), wo=w_o[l].astype(BF16),
                  lng=ln_mix_g[l][None], lnb=ln_mix_b[l][None], rwt=rwt, alpha=alpha)
        x1, x1b, idx, gate, rank, cnt = _mixer_tail(outs, lses, proj0, xf, lw, batch, seq)

        counts = cnt[:, 0].astype(jnp.int32)
        block_e, n_valid, tok_of_row, slot_of_row = _routing_tables(idx, rank, counts, n)
        y_tok = _expert_ffn(x1, wgu, wdn, l, block_e, n_valid, tok_of_row, slot_of_row)
        xf, xb = _expert_merge(y_tok, gate.T, x1, ln_ffn_g[l][None], ln_ffn_b[l][None], alpha)
    return xf.reshape(batch, seq, d)
```

```python
import functools
import math

import jax
import jax.numpy as jnp
from jax import lax
from jax.experimental import pallas as pl
from jax.experimental.pallas import tpu as pltpu

F32 = jnp.float32
BF16 = jnp.bfloat16

HEAD_DIM = 64
HEADS_PER_GROUP = 4
GROUP_WIDTH = HEADS_PER_GROUP * HEAD_DIM
DILATION_GROUPS = ((128, 1), (512, 4), (2048, 16))
N_GROUPS = len(DILATION_GROUPS)
ATTN_WIDTH = N_GROUPS * GROUP_WIDTH
BAND = 128
CONV_KERNEL = 31
CONV_HALO = 32
CONV_TAIL = 16
N_BUCKETS = 32
MAX_DISTANCE = 2048
N_EXPERTS = 16
EXPERTS_PER_GROUP = 4
LN_EPS = 1e-5
MASKED = -1e30

ROW_BLOCK = 256
TOKEN_TILE = 256
VMEM_LIMIT = 56 * 1024 * 1024


def _sigmoid(v):
    return 1.0 / (1.0 + jnp.exp(-v))


def _layer_norm(z, g, b):
    mu = jnp.mean(z, axis=-1, keepdims=True)
    zc = z - mu
    var = jnp.mean(zc * zc, axis=-1, keepdims=True)
    return zc * lax.rsqrt(var + LN_EPS) * g + b


def _params(*sem):
    return pltpu.CompilerParams(dimension_semantics=sem, vmem_limit_bytes=VMEM_LIMIT)


def _mm_kernel(x_ref, w_ref, o_ref):
    o_ref[...] = jnp.dot(x_ref[...], w_ref[...], preferred_element_type=F32).astype(o_ref.dtype)


def _project(xb, w, tm, tn):
    n, d = xb.shape
    c = w.shape[1]
    return pl.pallas_call(
        _mm_kernel,
        grid=(n // tm, c // tn),
        in_specs=[pl.BlockSpec((tm, d), lambda i, j: (i, 0)),
                  pl.BlockSpec((d, tn), lambda i, j: (0, j))],
        out_specs=pl.BlockSpec((tm, tn), lambda i, j: (i, j)),
        out_shape=jax.ShapeDtypeStruct((n, c), BF16),
        compiler_params=_params("arbitrary", "arbitrary"),
        name="proj_natural",
    )(xb, w)


def _project_dilated(xb, w, batch, seq, r):
    n, d = xb.shape
    c = w.shape[1]
    sub = seq // r
    tm = min(512, sub)
    nl = sub // tm
    xv = xb.reshape(batch * sub, r * d)
    return pl.pallas_call(
        _mm_kernel,
        grid=(batch, r, nl),
        in_specs=[pl.BlockSpec((tm, d), lambda b, cc, j: (b * nl + j, cc)),
                  pl.BlockSpec((d, c), lambda b, cc, j: (0, 0))],
        out_specs=pl.BlockSpec((tm, c), lambda b, cc, j: ((b * r + cc) * nl + j, 0)),
        out_shape=jax.ShapeDtypeStruct((n, c), BF16),
        compiler_params=_params("arbitrary", "arbitrary", "arbitrary"),
        name=f"proj_dilated_{r}",
    )(xv, w)


def _attn_kernel(q_ref, kp_ref, vp_ref, kc_ref, vc_ref, bias_ref, o_ref, lse_ref, k_scr, v_scr, *, nqb):
    i = pl.program_id(1)
    k_scr[0:BAND, :] = kp_ref[0]
    k_scr[BAND:, :] = kc_ref[0]
    v_scr[0:BAND, :] = vp_ref[0]
    v_scr[BAND:, :] = vc_ref[0]
    col = lax.broadcasted_iota(jnp.int32, (BAND, 2 * BAND), 1)
    first_head = lax.broadcasted_iota(jnp.int32, (BAND, 2 * HEAD_DIM), 1) < HEAD_DIM

    def body(j, carry):
        r0 = pl.multiple_of(j * BAND, BAND)
        q = q_ref[0, pl.ds(r0, BAND), :]
        kk = k_scr[pl.ds(r0, 2 * BAND), :]
        vv = v_scr[pl.ds(r0, 2 * BAND), :]
        no_prev = jnp.logical_and(jnp.logical_and(i == 0, j == 0), col < BAND)
        outs, lses = [], []
        for pair in range(HEADS_PER_GROUP // 2):
            sl = slice(pair * 2 * HEAD_DIM, (pair + 1) * 2 * HEAD_DIM)
            q2, k2, v2 = q[:, sl], kk[:, sl], vv[:, sl]
            o_pair, lse_pair = None, None
            for hh in range(2):
                mine = first_head if hh == 0 else jnp.logical_not(first_head)
                qm = jnp.where(mine, q2, jnp.zeros_like(q2))
                s = lax.dot_general(qm, k2, (((1,), (1,)), ((), ())), preferred_element_type=F32)
                s = s + bias_ref[2 * pair + hh]
                s = jnp.where(no_prev, MASKED, s)
                m = jnp.max(s, axis=-1, keepdims=True)
                p = jnp.exp(s - m)
                den = jnp.sum(p, axis=-1, keepdims=True)
                o = jnp.dot(p.astype(BF16), v2, preferred_element_type=F32) * (1.0 / den)
                lse = jnp.broadcast_to(m + jnp.log(den), (BAND, 2 * HEAD_DIM))
                o_pair = o if hh == 0 else jnp.where(first_head, o_pair, o)
                lse_pair = lse if hh == 0 else jnp.where(first_head, lse_pair, lse)
            outs.append(o_pair)
            lses.append(lse_pair)
        o_ref[0, pl.ds(r0, BAND), :] = jnp.concatenate(outs, axis=1).astype(o_ref.dtype)
        lse_ref[0, pl.ds(r0, BAND), :] = jnp.concatenate(lses, axis=1)
        return carry

    lax.fori_loop(0, nqb, body, 0, unroll=2 if nqb % 2 == 0 else 1)


def _attention(qkv, bias, batch, seq, r, col0):
    sub = seq // r
    nqb = min(8, sub // BAND)
    rows = nqb * BAND
    nt = sub // rows
    gw = GROUP_WIDTH

    def cur(c):
        return pl.BlockSpec((1, rows, gw), lambda bc, i: (bc, i, c))

    def prev(c):
        return pl.BlockSpec((1, BAND, gw), lambda bc, i: (bc, jnp.maximum(i * nqb - 1, 0), c))

    out_spec = pl.BlockSpec((1, rows, gw), lambda bc, i: (bc // r, i, bc % r))
    o, lse = pl.pallas_call(
        functools.partial(_attn_kernel, nqb=nqb),
        grid=(batch * r, nt),
        in_specs=[cur(col0), prev(col0 + 1), prev(col0 + 2), cur(col0 + 1), cur(col0 + 2),
                  pl.BlockSpec((HEADS_PER_GROUP, BAND, 2 * BAND), lambda bc, i: (0, 0, 0))],
        out_specs=[out_spec, out_spec],
        out_shape=[jax.ShapeDtypeStruct((batch, sub, r * gw), BF16),
                   jax.ShapeDtypeStruct((batch, sub, r * gw), F32)],
        scratch_shapes=[pltpu.VMEM((rows + BAND, gw), BF16), pltpu.VMEM((rows + BAND, gw), BF16)],
        compiler_params=_params("arbitrary", "arbitrary"),
        name=f"attn_dilation_{r}",
    )(qkv, qkv, qkv, qkv, qkv, bias)
    return o.reshape(batch * seq, gw), lse.reshape(batch * seq, gw)


def _t5_bucket(dist):
    max_exact = N_BUCKETS // 2
    n = jnp.maximum(dist, 0)
    nf = jnp.maximum(n, 1).astype(F32)
    large = max_exact + (jnp.log(nf / max_exact) / math.log(MAX_DISTANCE / max_exact)
                         * (N_BUCKETS - max_exact)).astype(jnp.int32)
    large = jnp.minimum(large, N_BUCKETS - 1)
    return jnp.where(n < max_exact, n, large)


def _band_bias(rel_bias, gi, r):
    qi = jnp.arange(BAND)[:, None]
    kj = jnp.arange(2 * BAND)[None, :]
    dist = qi + BAND - kj
    bucket = _t5_bucket(dist * r)
    valid = (dist >= 0) & (dist <= BAND)
    b = jnp.full((HEADS_PER_GROUP, BAND, 2 * BAND), MASKED, F32)
    for k in range(N_BUCKETS):
        row = rel_bias[k, gi * HEADS_PER_GROUP:(gi + 1) * HEADS_PER_GROUP].astype(F32)
        b = jnp.where(((bucket == k) & valid)[None], row[:, None, None], b)
    return b


def _post_kernel(o0_ref, o1_ref, o2_ref, l0_ref, l1_ref, l2_ref, u_ref, uh_ref, g_ref, x_ref,
                 wao_ref, cw_ref, cb_ref, clg_ref, clb_ref, wco_ref, wo_ref, lng_ref, lnb_ref, rwt_ref,
                 x1_ref, x1b_ref, idx_ref, gate_ref, rank_ref, cnt_ref,
                 hs_scr, conv_scr, ys_scr, carry_scr, *, tm, seq, alpha):
    i = pl.program_id(0)
    d = x_ref.shape[1]
    c = d // 2

    l0, l1, l2 = l0_ref[...], l1_ref[...], l2_ref[...]
    mx = jnp.maximum(jnp.maximum(l0, l1), l2)
    e0, e1, e2 = jnp.exp(l0 - mx), jnp.exp(l1 - mx), jnp.exp(l2 - mx)
    attn = (e0 * o0_ref[...].astype(F32) + e1 * o1_ref[...].astype(F32)
            + e2 * o2_ref[...].astype(F32)) * (1.0 / (e0 + e1 + e2))
    attn_branch = jnp.dot(attn.astype(BF16), wao_ref[...], preferred_element_type=F32)

    u = u_ref[...].astype(F32)
    uh = uh_ref[...].astype(F32)
    hh = uh[:, :c] * _sigmoid(uh[:, c:])
    seq_start = (i * tm) % seq == 0
    hs_scr[0:CONV_HALO, :] = jnp.where(seq_start, 0.0, hh)
    hs_scr[CONV_HALO:CONV_HALO + tm, :] = u[:, :c] * _sigmoid(u[:, c:])
    hs_scr[CONV_HALO + tm:, :] = jnp.zeros((CONV_TAIL, c), F32)
    off = CONV_HALO - (CONV_KERNEL - 1)
    rows_y = tm + 8
    for cc in range(c // 128):
        ls = slice(cc * 128, (cc + 1) * 128)
        acc = None
        for s in range(8):
            y = None
            for a in range((off + CONV_KERNEL + 7) // 8):
                k = 8 * a + s - off
                if 0 <= k < CONV_KERNEL:
                    term = hs_scr[8 * a:8 * a + rows_y, ls] * cw_ref[k:k + 1, ls]
                    y = term if y is None else y + term
            if s == 0:
                acc = y[0:tm]
            else:
                ys_scr[s] = y
                acc = acc + ys_scr[s, s:s + tm, :]
        conv_scr[:, ls] = acc
    hc = _layer_norm(conv_scr[...] + cb_ref[...], clg_ref[...], clb_ref[...])
    hc = hc * _sigmoid(hc)
    conv_branch = jnp.dot(hc.astype(BF16), wco_ref[...], preferred_element_type=F32)

    gates = _sigmoid(g_ref[...].astype(F32))
    merged = gates[:, :d] * attn_branch + gates[:, d:] * conv_branch
    hmix = jnp.dot(merged.astype(BF16), wo_ref[...], preferred_element_type=F32)
    x1 = _layer_norm(alpha * x_ref[...] + hmix, lng_ref[...], lnb_ref[...])
    x1_ref[...] = x1
    x1b = x1.astype(BF16)
    x1b_ref[...] = x1b

    logits = lax.dot_general(rwt_ref[...], x1b, (((1,), (1,)), ((), ())), preferred_element_type=F32)
    ex = jnp.exp(logits - jnp.max(logits, axis=0, keepdims=True))
    rows = [ex[e:e + 1, :] for e in range(N_EXPERTS)]
    best, gsel = None, None
    for g in range(N_EXPERTS // EXPERTS_PER_GROUP):
        v = rows[g * EXPERTS_PER_GROUP:(g + 1) * EXPERTS_PER_GROUP]
        score = None
        for a in range(EXPERTS_PER_GROUP):
            for b in range(a + 1, EXPERTS_PER_GROUP):
                ps = v[a] + v[b]
                score = ps if score is None else jnp.maximum(score, ps)
        if best is None:
            best, gsel = score, jnp.zeros_like(score, dtype=jnp.int32)
        else:
            upd = score > best
            gsel = jnp.where(upd, g, gsel)
            best = jnp.where(upd, score, best)
    vals = []
    for j in range(EXPERTS_PER_GROUP):
        vj = rows[j]
        for g in range(1, N_EXPERTS // EXPERTS_PER_GROUP):
            vj = jnp.where(gsel == g, rows[g * EXPERTS_PER_GROUP + j], vj)
        vals.append(vj)
    v1, i1 = vals[0], jnp.zeros_like(gsel)
    for j in range(1, EXPERTS_PER_GROUP):
        upd = vals[j] > v1
        i1 = jnp.where(upd, j, i1)
        v1 = jnp.where(upd, vals[j], v1)
    v2, i2 = jnp.full_like(v1, -1.0), jnp.zeros_like(gsel)
    for j in range(EXPERTS_PER_GROUP):
        upd = jnp.logical_and(i1 != j, vals[j] > v2)
        i2 = jnp.where(upd, j, i2)
        v2 = jnp.where(upd, vals[j], v2)
    ea = gsel * EXPERTS_PER_GROUP + i1
    eb = gsel * EXPERTS_PER_GROUP + i2
    inv = 1.0 / (v1 + v2)
    idx_ref[...] = jnp.concatenate([ea, eb], axis=0)
    gate_ref[...] = jnp.concatenate([v1 * inv, v2 * inv], axis=0)

    @pl.when(i == 0)
    def _():
        carry_scr[...] = jnp.zeros_like(carry_scr)

    eid = lax.broadcasted_iota(jnp.int32, (N_EXPERTS, tm), 0)
    hit_a = eid == ea
    hit_b = eid == eb
    onehot = jnp.where(jnp.logical_or(hit_a, hit_b), 1.0, 0.0)
    before = (lax.broadcasted_iota(jnp.int32, (tm, tm), 0)
              < lax.broadcasted_iota(jnp.int32, (tm, tm), 1))
    upper = jnp.where(before, 1.0, 0.0).astype(BF16)
    pos = jnp.dot(onehot.astype(BF16), upper, preferred_element_type=F32) + carry_scr[:, 0:1]
    ra = jnp.sum(jnp.where(hit_a, pos, 0.0), axis=0, keepdims=True)
    rb = jnp.sum(jnp.where(hit_b, pos, 0.0), axis=0, keepdims=True)
    rank_ref[...] = jnp.concatenate([ra, rb], axis=0).astype(jnp.int32)
    carry = carry_scr[...] + jnp.sum(onehot, axis=1, keepdims=True)
    carry_scr[...] = carry
    cnt_ref[...] = carry


def _mixer_tail(o, lse, proj0, x, lw, batch, seq):
    n, d = x.shape
    tm = TOKEN_TILE
    nt = n // tm
    gw = GROUP_WIDTH
    hb = tm // CONV_HALO

    def tok(width):
        return pl.BlockSpec((tm, width), lambda i: (i, 0))

    def const(shape):
        return pl.BlockSpec(shape, lambda i: (0,) * len(shape))

    in_specs = (
        [tok(gw)] * 3 + [tok(gw)] * 3
        + [pl.BlockSpec((tm, d), lambda i: (i, 2)),
           pl.BlockSpec((CONV_HALO, d), lambda i: (jnp.maximum(i * hb - 1, 0), 2)),
           pl.BlockSpec((tm, 2 * d), lambda i: (i, 0)),
           tok(d)]
        + [const(lw[k].shape) for k in ("wao", "cw", "cb", "clg", "clb", "wco", "wo", "lng", "lnb", "rwt")]
    )
    lane_row = pl.BlockSpec((2, tm), lambda i: (0, i))
    out_specs = [tok(d), tok(d), lane_row, lane_row, lane_row, const((N_EXPERTS, 128))]
    out_shape = [jax.ShapeDtypeStruct((n, d), F32), jax.ShapeDtypeStruct((n, d), BF16),
                 jax.ShapeDtypeStruct((2, n), jnp.int32), jax.ShapeDtypeStruct((2, n), F32),
                 jax.ShapeDtypeStruct((2, n), jnp.int32), jax.ShapeDtypeStruct((N_EXPERTS, 128), F32)]
    alpha = lw["alpha"]
    return pl.pallas_call(
        functools.partial(_post_kernel, tm=tm, seq=seq, alpha=alpha),
        grid=(nt,),
        in_specs=in_specs,
        out_specs=out_specs,
        out_shape=out_shape,
        scratch_shapes=[pltpu.VMEM((CONV_HALO + tm + CONV_TAIL, d // 2), F32), pltpu.VMEM((tm, d // 2), F32),
                        pltpu.VMEM((8, tm + 8, 128), F32), pltpu.VMEM((N_EXPERTS, 128), F32)],
        compiler_params=_params("arbitrary"),
        name="mixer_tail",
    )(o[0], o[1], o[2], lse[0], lse[1], lse[2], proj0, proj0, proj0, x,
      *[lw[k] for k in ("wao", "cw", "cb", "clg", "clb", "wco", "wo", "lng", "lnb", "rwt")])


def _ffn_kernel(be_ref, nv_ref, tok_ref, slot_ref, x_hbm, wgu_ref, wdn_ref, y_hbm,
                xbuf, obuf, gsem, ssem, *, nb, blk):
    b = pl.program_id(0)
    cur = b % 2
    f = wdn_ref.shape[2]

    def row_in(blk_id, buf, r):
        t = tok_ref[blk_id * blk + r]
        return pltpu.make_async_copy(x_hbm.at[pl.ds(t, 1)], xbuf.at[buf, pl.ds(r, 1)], gsem.at[buf])

    def row_out(blk_id, buf, r):
        s = slot_ref[blk_id * blk + r]
        return pltpu.make_async_copy(obuf.at[buf, pl.ds(r, 1)], y_hbm.at[pl.ds(s, 1)], ssem.at[buf])

    def gather_start(blk_id, buf):
        def body(r, carry):
            row_in(blk_id, buf, r).start()
            return carry
        lax.fori_loop(0, blk, body, 0, unroll=8)

    def gather_wait(buf):
        pltpu.make_async_copy(x_hbm.at[pl.ds(0, blk)], xbuf.at[buf], gsem.at[buf]).wait()

    def for_valid_rows(blk_id, fn):
        nv = nv_ref[blk_id]
        full = lax.shift_right_logical(nv, 3)

        def chunk(c, carry):
            for u in range(8):
                fn(c * 8 + u)
            return carry

        def tail(r, carry):
            fn(r)
            return carry

        lax.fori_loop(0, full, chunk, 0)
        lax.fori_loop(full * 8, nv, tail, 0)

    def scatter_start(blk_id, buf):
        for_valid_rows(blk_id, lambda r: row_out(blk_id, buf, r).start())

    def scatter_wait(blk_id, buf):
        for_valid_rows(blk_id, lambda r: row_out(blk_id, buf, r).wait())

    @pl.when(b == 0)
    def _():
        gather_start(b, cur)

    @pl.when(b + 1 < nb)
    def _():
        gather_start(b + 1, 1 - cur)

    gather_wait(cur)
    xb = xbuf[cur].astype(BF16)
    h = jnp.dot(xb, wgu_ref[0, 0], preferred_element_type=F32)
    a = h[:, :f]
    act = (a * _sigmoid(a) * h[:, f:]).astype(BF16)
    y = jnp.dot(act, wdn_ref[0, 0], preferred_element_type=F32)

    @pl.when(b >= 2)
    def _():
        scatter_wait(b - 2, cur)

    obuf[cur] = y
    scatter_start(b, cur)

    @pl.when(b == nb - 1)
    def _():
        if nb >= 2:
            scatter_wait(b - 1, 1 - cur)
        scatter_wait(b, cur)


def _expert_ffn(x1, wgu, wdn, layer, block_e, n_valid, tok_of_row, slot_of_row):
    n, d = x1.shape
    nb = block_e.shape[0]
    blk = ROW_BLOCK
    f2 = wgu.shape[3]
    f = wdn.shape[2]
    grid_spec = pltpu.PrefetchScalarGridSpec(
        num_scalar_prefetch=4,
        grid=(nb,),
        in_specs=[pl.BlockSpec(memory_space=pl.ANY),
                  pl.BlockSpec((1, 1, d, f2), lambda b, be, nv, tk, sl: (layer, be[b], 0, 0)),
                  pl.BlockSpec((1, 1, f, d), lambda b, be, nv, tk, sl: (layer, be[b], 0, 0))],
        out_specs=pl.BlockSpec(memory_space=pl.ANY),
        scratch_shapes=[pltpu.VMEM((2, blk, d), F32), pltpu.VMEM((2, blk, d), F32),
                        pltpu.SemaphoreType.DMA((2,)), pltpu.SemaphoreType.DMA((2,))],
    )
    return pl.pallas_call(
        functools.partial(_ffn_kernel, nb=nb, blk=blk),
        grid_spec=grid_spec,
        out_shape=jax.ShapeDtypeStruct((2 * n, d), F32),
        compiler_params=_params("arbitrary"),
        name="expert_ffn",
    )(block_e, n_valid, tok_of_row, slot_of_row, x1, wgu, wdn)


def _merge_kernel(y0_ref, y1_ref, gt_ref, x1_ref, g_ref, b_ref, x2_ref, x2b_ref, *, alpha):
    gt = gt_ref[...]
    m = gt[:, 0:1] * y0_ref[...] + gt[:, 1:2] * y1_ref[...]
    x2 = _layer_norm(alpha * x1_ref[...] + m, g_ref[...], b_ref[...])
    x2_ref[...] = x2
    x2b_ref[...] = x2.astype(BF16)


def _expert_merge(y_tok, gate_t, x1, g, b, alpha):
    n, d = x1.shape
    tm = TOKEN_TILE
    nt = n // tm
    return pl.pallas_call(
        functools.partial(_merge_kernel, alpha=alpha),
        grid=(nt,),
        in_specs=[pl.BlockSpec((tm, d), lambda i: (i, 0)),
                  pl.BlockSpec((tm, d), lambda i: (i + nt, 0)),
                  pl.BlockSpec((tm, 2), lambda i: (i, 0)),
                  pl.BlockSpec((tm, d), lambda i: (i, 0)),
                  pl.BlockSpec((1, d), lambda i: (0, 0)),
                  pl.BlockSpec((1, d), lambda i: (0, 0))],
        out_specs=[pl.BlockSpec((tm, d), lambda i: (i, 0)), pl.BlockSpec((tm, d), lambda i: (i, 0))],
        out_shape=[jax.ShapeDtypeStruct((n, d), F32), jax.ShapeDtypeStruct((n, d), BF16)],
        compiler_params=_params("arbitrary"),
        name="expert_merge",
    )(y_tok, y_tok, gate_t, x1, g, b)


def _routing_tables(idx, rank, counts, n):
    blk = ROW_BLOCK
    n_slots = 2 * n
    nb = -(-(n_slots + N_EXPERTS * (blk - 1)) // blk)
    n_rows = nb * blk
    padded = (counts + blk - 1) // blk * blk
    pad_end = jnp.cumsum(padded)
    pad_start = pad_end - padded
    experts = jnp.arange(N_EXPERTS, dtype=jnp.int32)

    def lookup(table, keys):
        hit = keys[None] == experts.reshape((N_EXPERTS,) + (1,) * keys.ndim)
        return jnp.sum(jnp.where(hit, table.reshape((N_EXPERTS,) + (1,) * keys.ndim), 0), axis=0)

    dest = (lookup(pad_start, idx) + rank).reshape(-1)
    slots = jnp.arange(n_slots, dtype=jnp.int32)
    slot_of_row = jnp.zeros((n_rows,), jnp.int32).at[dest].set(slots, unique_indices=True)
    tok_of_row = slot_of_row % n
    block_row0 = jnp.arange(nb, dtype=jnp.int32) * blk
    block_e = jnp.minimum(jnp.sum(pad_end[None, :] <= block_row0[:, None], axis=1), N_EXPERTS - 1).astype(jnp.int32)
    n_valid = jnp.clip(lookup(pad_start + counts, block_e) - block_row0, 0, blk).astype(jnp.int32)
    return block_e, n_valid, tok_of_row, slot_of_row


def kernel(x, w_in, w_attn_out, w_conv_out, w_o, conv_w, conv_b, conv_ln_g, conv_ln_b, ln_mix_g, ln_mix_b,
           expert_w_gate_up, expert_w_down, ln_ffn_g, ln_ffn_b, router_w, rel_bias):
    batch, seq, d = x.shape
    depth = w_in.shape[0]
    n = batch * seq
    aw, gw = ATTN_WIDTH, GROUP_WIDTH
    assert d % 256 == 0 and n % 1024 == 0
    for window, r in DILATION_GROUPS:
        assert window // r == BAND and seq % (r * BAND) == 0
    alpha = (2 * depth) ** 0.25
    scale = HEAD_DIM ** -0.5

    biases = [_band_bias(rel_bias, gi, r) for gi, (_, r) in enumerate(DILATION_GROUPS)]
    wgu = expert_w_gate_up.astype(BF16)
    wdn = expert_w_down.astype(BF16)
    rwt = router_w.T.astype(BF16)

    xf = x.reshape(n, d)
    xb = xf.astype(BF16)
    for l in range(depth):
        wl = w_in[l]

        def qkv_cols(gi):
            return jnp.concatenate([wl[:, gi * gw:(gi + 1) * gw] * scale,
                                    wl[:, aw + gi * gw:aw + (gi + 1) * gw],
                                    wl[:, 2 * aw + gi * gw:2 * aw + (gi + 1) * gw]], axis=1)

        w0 = jnp.concatenate([wl[:, 3 * aw + d:], wl[:, 3 * aw:3 * aw + d], qkv_cols(0)], axis=1).astype(BF16)
        proj0 = _project(xb, w0, 1024, 768)
        outs, lses = [], []
        for gi, (_, r) in enumerate(DILATION_GROUPS):
            if r == 1:
                qkv, col0 = proj0.reshape(batch, seq, proj0.shape[1]), (3 * d) // gw
            else:
                qkv = _project_dilated(xb, qkv_cols(gi).astype(BF16), batch, seq, r)
                qkv, col0 = qkv.reshape(batch * r, seq // r, 3 * gw), 0
            o, lse = _attention(qkv, biases[gi], batch, seq, r, col0)
            outs.append(o)
            lses.append(lse)

        lw = dict(wao=w_attn_out[l].astype(BF16), cw=conv_w[l], cb=conv_b[l][None], clg=conv_ln_g[l][None],
                  clb=conv_ln_b[l][None], wco=w_conv_out[l].astype(BF16), wo=w_o[l].astype(BF16),
                  lng=ln_mix_g[l][None], lnb=ln_mix_b[l][None], rwt=rwt, alpha=alpha)
        x1, x1b, idx, gate, rank, cnt = _mixer_tail(outs, lses, proj0, xf, lw, batch, seq)

        counts = cnt[:, 0].astype(jnp.int32)
        block_e, n_valid, tok_of_row, slot_of_row = _routing_tables(idx, rank, counts, n)
        y_tok = _expert_ffn(x1, wgu, wdn, l, block_e, n_valid, tok_of_row, slot_of_row)
        xf, xb = _expert_merge(y_tok, gate.T, x1, ln_ffn_g[l][None], ln_ffn_b[l][None], alpha)
    return xf.reshape(batch, seq, d)
```

```python
import functools
import math

import jax
import jax.numpy as jnp
from jax import lax
from jax.experimental import pallas as pl
from jax.experimental.pallas import tpu as pltpu

F32 = jnp.float32
BF16 = jnp.bfloat16

HEAD_DIM = 64
HEADS_PER_GROUP = 4
GROUP_WIDTH = HEADS_PER_GROUP * HEAD_DIM
DILATION_GROUPS = ((128, 1), (512, 4), (2048, 16))
N_GROUPS = len(DILATION_GROUPS)
ATTN_WIDTH = N_GROUPS * GROUP_WIDTH
BAND = 128
CONV_KERNEL = 31
CONV_HALO = 32
CONV_TAIL = 16
N_BUCKETS = 32
MAX_DISTANCE = 2048
N_EXPERTS = 16
EXPERTS_PER_GROUP = 4
LN_EPS = 1e-5
MASKED = -1e30

ROW_BLOCK = 256
TOKEN_TILE = 256
SEG_ALIGN = 8
SORTED_ROWS = -(-(2 * TOKEN_TILE + N_EXPERTS * (SEG_ALIGN - 1)) // 128) * 128
VMEM_LIMIT = 56 * 1024 * 1024


def _sigmoid(v):
    return 1.0 / (1.0 + jnp.exp(-v))


def _layer_norm(z, g, b):
    mu = jnp.mean(z, axis=-1, keepdims=True)
    zc = z - mu
    var = jnp.mean(zc * zc, axis=-1, keepdims=True)
    return zc * lax.rsqrt(var + LN_EPS) * g + b


def _params(*sem):
    return pltpu.CompilerParams(dimension_semantics=sem, vmem_limit_bytes=VMEM_LIMIT)


def _mm_kernel(x_ref, w_ref, o_ref):
    o_ref[...] = jnp.dot(x_ref[...], w_ref[...], preferred_element_type=F32).astype(o_ref.dtype)


def _project(xb, w, tm, tn):
    n, d = xb.shape
    c = w.shape[1]
    return pl.pallas_call(
        _mm_kernel,
        grid=(n // tm, c // tn),
        in_specs=[pl.BlockSpec((tm, d), lambda i, j: (i, 0)),
                  pl.BlockSpec((d, tn), lambda i, j: (0, j))],
        out_specs=pl.BlockSpec((tm, tn), lambda i, j: (i, j)),
        out_shape=jax.ShapeDtypeStruct((n, c), BF16),
        compiler_params=_params("arbitrary", "arbitrary"),
        name="proj_natural",
    )(xb, w)


def _project_dilated(xv, w, batch, seq, r):
    d, c = w.shape
    n = xv.shape[0] * r
    sub = seq // r
    tm = min(512, sub)
    nl = sub // tm
    return pl.pallas_call(
        _mm_kernel,
        grid=(batch, r, nl),
        in_specs=[pl.BlockSpec((tm, d), lambda b, cc, j: (b * nl + j, cc)),
                  pl.BlockSpec((d, c), lambda b, cc, j: (0, 0))],
        out_specs=pl.BlockSpec((tm, c), lambda b, cc, j: ((b * r + cc) * nl + j, 0)),
        out_shape=jax.ShapeDtypeStruct((n, c), BF16),
        compiler_params=_params("arbitrary", "arbitrary", "arbitrary"),
        name=f"proj_dilated_{r}",
    )(xv, w)


def _attn_kernel(q_ref, kp_ref, vp_ref, kc_ref, vc_ref, bias_ref, o_ref, lse_ref, k_scr, v_scr, *, nqb):
    i = pl.program_id(1)
    k_scr[0:BAND, :] = kp_ref[0]
    k_scr[BAND:, :] = kc_ref[0]
    v_scr[0:BAND, :] = vp_ref[0]
    v_scr[BAND:, :] = vc_ref[0]
    col = lax.broadcasted_iota(jnp.int32, (BAND, 2 * BAND), 1)
    first_head = lax.broadcasted_iota(jnp.int32, (BAND, 2 * HEAD_DIM), 1) < HEAD_DIM

    def body(j, carry):
        r0 = pl.multiple_of(j * BAND, BAND)
        q = q_ref[0, pl.ds(r0, BAND), :]
        kk = k_scr[pl.ds(r0, 2 * BAND), :]
        vv = v_scr[pl.ds(r0, 2 * BAND), :]
        no_prev = jnp.logical_and(jnp.logical_and(i == 0, j == 0), col < BAND)
        outs, lses = [], []
        for pair in range(HEADS_PER_GROUP // 2):
            sl = slice(pair * 2 * HEAD_DIM, (pair + 1) * 2 * HEAD_DIM)
            q2, k2, v2 = q[:, sl], kk[:, sl], vv[:, sl]
            o_pair, lse_pair = None, None
            for hh in range(2):
                mine = first_head if hh == 0 else jnp.logical_not(first_head)
                qm = jnp.where(mine, q2, jnp.zeros_like(q2))
                s = lax.dot_general(qm, k2, (((1,), (1,)), ((), ())), preferred_element_type=F32)
                s = s + bias_ref[2 * pair + hh]
                s = jnp.where(no_prev, MASKED, s)
                m = jnp.max(s, axis=-1, keepdims=True)
                p = jnp.exp(s - m)
                den = jnp.sum(p, axis=-1, keepdims=True)
                o = jnp.dot(p.astype(BF16), v2, preferred_element_type=F32) * (1.0 / den)
                lse = jnp.broadcast_to(m + jnp.log(den), (BAND, 2 * HEAD_DIM))
                o_pair = o if hh == 0 else jnp.where(first_head, o_pair, o)
                lse_pair = lse if hh == 0 else jnp.where(first_head, lse_pair, lse)
            outs.append(o_pair)
            lses.append(lse_pair)
        o_ref[0, pl.ds(r0, BAND), :] = jnp.concatenate(outs, axis=1).astype(o_ref.dtype)
        lse_ref[0, pl.ds(r0, BAND), :] = jnp.concatenate(lses, axis=1)
        return carry

    lax.fori_loop(0, nqb, body, 0, unroll=2 if nqb % 2 == 0 else 1)


def _attention(qkv, bias, batch, seq, r, col0):
    sub = seq // r
    nqb = min(8, sub // BAND)
    rows = nqb * BAND
    nt = sub // rows
    gw = GROUP_WIDTH

    def cur(c):
        return pl.BlockSpec((1, rows, gw), lambda bc, i: (bc, i, c))

    def prev(c):
        return pl.BlockSpec((1, BAND, gw), lambda bc, i: (bc, jnp.maximum(i * nqb - 1, 0), c))

    out_spec = pl.BlockSpec((1, rows, gw), lambda bc, i: (bc // r, i, bc % r))
    o, lse = pl.pallas_call(
        functools.partial(_attn_kernel, nqb=nqb),
        grid=(batch * r, nt),
        in_specs=[cur(col0), prev(col0 + 1), prev(col0 + 2), cur(col0 + 1), cur(col0 + 2),
                  pl.BlockSpec((HEADS_PER_GROUP, BAND, 2 * BAND), lambda bc, i: (0, 0, 0))],
        out_specs=[out_spec, out_spec],
        out_shape=[jax.ShapeDtypeStruct((batch, sub, r * gw), BF16),
                   jax.ShapeDtypeStruct((batch, sub, r * gw), F32)],
        scratch_shapes=[pltpu.VMEM((rows + BAND, gw), BF16), pltpu.VMEM((rows + BAND, gw), BF16)],
        compiler_params=_params("arbitrary", "arbitrary"),
        name=f"attn_dilation_{r}",
    )(qkv, qkv, qkv, qkv, qkv, bias)
    return o.reshape(batch * sub, r * gw), lse.reshape(batch * sub, r * gw)


def _t5_bucket(dist):
    max_exact = N_BUCKETS // 2
    n = jnp.maximum(dist, 0)
    nf = jnp.maximum(n, 1).astype(F32)
    large = max_exact + (jnp.log(nf / max_exact) / math.log(MAX_DISTANCE / max_exact)
                         * (N_BUCKETS - max_exact)).astype(jnp.int32)
    large = jnp.minimum(large, N_BUCKETS - 1)
    return jnp.where(n < max_exact, n, large)


def _band_bias(rel_bias, gi, r):
    qi = jnp.arange(BAND)[:, None]
    kj = jnp.arange(2 * BAND)[None, :]
    dist = qi + BAND - kj
    bucket = _t5_bucket(dist * r)
    valid = (dist >= 0) & (dist <= BAND)
    b = jnp.full((HEADS_PER_GROUP, BAND, 2 * BAND), MASKED, F32)
    for k in range(N_BUCKETS):
        row = rel_bias[k, gi * HEADS_PER_GROUP:(gi + 1) * HEADS_PER_GROUP].astype(F32)
        b = jnp.where(((bucket == k) & valid)[None], row[:, None, None], b)
    return b


def _post_kernel(o0_ref, o1_ref, o2_ref, l0_ref, l1_ref, l2_ref, u_ref, uh_ref, g_ref, x_ref,
                 wao_ref, cw_ref, cb_ref, clg_ref, clb_ref, wco_ref, wo_ref, lng_ref, lnb_ref, rwt_ref,
                 x1_ref, x1b_ref, gate_ref, pos_ref, cnt_ref,
                 hs_scr, conv_scr, ys_scr, nat_scr, *, tm, seq, alpha):
    i = pl.program_id(0)
    d = x_ref.shape[1]
    c = d // 2
    gw = GROUP_WIDTH

    def natural(ref, slot, r):
        if r == 1:
            return ref[...].astype(F32)
        nl = gw // 128
        for cls in range(r):
            for j in range(nl):
                lanes = slice(cls * gw + j * 128, cls * gw + (j + 1) * 128)
                nat_scr[slot * nl + j, pl.ds(cls, tm // r, stride=r), :] = ref[:, lanes].astype(F32)
        return jnp.concatenate([nat_scr[slot * nl + j] for j in range(nl)], axis=1)

    rs = [r for _, r in DILATION_GROUPS]
    o0, o1, o2 = (natural(ref, s, r) for s, (ref, r) in enumerate(zip((o0_ref, o1_ref, o2_ref), rs)))
    l0, l1, l2 = (natural(ref, 3 + s, r) for s, (ref, r) in enumerate(zip((l0_ref, l1_ref, l2_ref), rs)))
    mx = jnp.maximum(jnp.maximum(l0, l1), l2)
    e0, e1, e2 = jnp.exp(l0 - mx), jnp.exp(l1 - mx), jnp.exp(l2 - mx)
    attn = (e0 * o0 + e1 * o1 + e2 * o2) * (1.0 / (e0 + e1 + e2))
    attn_branch = jnp.dot(attn.astype(BF16), wao_ref[...], preferred_element_type=F32)

    u = u_ref[...].astype(F32)
    uh = uh_ref[...].astype(F32)
    hh = uh[:, :c] * _sigmoid(uh[:, c:])
    seq_start = (i * tm) % seq == 0
    hs_scr[0:CONV_HALO, :] = jnp.where(seq_start, 0.0, hh)
    hs_scr[CONV_HALO:CONV_HALO + tm, :] = u[:, :c] * _sigmoid(u[:, c:])
    hs_scr[CONV_HALO + tm:, :] = jnp.zeros((CONV_TAIL, c), F32)
    off = CONV_HALO - (CONV_KERNEL - 1)
    rows_y = tm + 8
    for cc in range(c // 128):
        ls = slice(cc * 128, (cc + 1) * 128)
        acc = None
        for s in range(8):
            y = None
            for a in range((off + CONV_KERNEL + 7) // 8):
                k = 8 * a + s - off
                if 0 <= k < CONV_KERNEL:
                    term = hs_scr[8 * a:8 * a + rows_y, ls] * cw_ref[k:k + 1, ls]
                    y = term if y is None else y + term
            if s == 0:
                acc = y[0:tm]
            else:
                ys_scr[s] = y
                acc = acc + ys_scr[s, s:s + tm, :]
        conv_scr[:, ls] = acc
    hc = _layer_norm(conv_scr[...] + cb_ref[...], clg_ref[...], clb_ref[...])
    hc = hc * _sigmoid(hc)
    conv_branch = jnp.dot(hc.astype(BF16), wco_ref[...], preferred_element_type=F32)

    gates = _sigmoid(g_ref[...].astype(F32))
    merged = gates[:, :d] * attn_branch + gates[:, d:] * conv_branch
    hmix = jnp.dot(merged.astype(BF16), wo_ref[...], preferred_element_type=F32)
    x1 = _layer_norm(alpha * x_ref[...] + hmix, lng_ref[...], lnb_ref[...])
    x1_ref[...] = x1
    x1b = x1.astype(BF16)
    x1b_ref[...] = x1b

    logits = lax.dot_general(rwt_ref[...], x1b, (((1,), (1,)), ((), ())), preferred_element_type=F32)
    ex = jnp.exp(logits - jnp.max(logits, axis=0, keepdims=True))
    rows = [ex[e:e + 1, :] for e in range(N_EXPERTS)]
    best, gsel = None, None
    for g in range(N_EXPERTS // EXPERTS_PER_GROUP):
        v = rows[g * EXPERTS_PER_GROUP:(g + 1) * EXPERTS_PER_GROUP]
        score = None
        for a in range(EXPERTS_PER_GROUP):
            for b in range(a + 1, EXPERTS_PER_GROUP):
                ps = v[a] + v[b]
                score = ps if score is None else jnp.maximum(score, ps)
        if best is None:
            best, gsel = score, jnp.zeros_like(score, dtype=jnp.int32)
        else:
            upd = score > best
            gsel = jnp.where(upd, g, gsel)
            best = jnp.where(upd, score, best)
    vals = []
    for j in range(EXPERTS_PER_GROUP):
        vj = rows[j]
        for g in range(1, N_EXPERTS // EXPERTS_PER_GROUP):
            vj = jnp.where(gsel == g, rows[g * EXPERTS_PER_GROUP + j], vj)
        vals.append(vj)
    v1, i1 = vals[0], jnp.zeros_like(gsel)
    for j in range(1, EXPERTS_PER_GROUP):
        upd = vals[j] > v1
        i1 = jnp.where(upd, j, i1)
        v1 = jnp.where(upd, vals[j], v1)
    v2, i2 = jnp.full_like(v1, -1.0), jnp.zeros_like(gsel)
    for j in range(EXPERTS_PER_GROUP):
        upd = jnp.logical_and(i1 != j, vals[j] > v2)
        i2 = jnp.where(upd, j, i2)
        v2 = jnp.where(upd, vals[j], v2)
    ea = gsel * EXPERTS_PER_GROUP + i1
    eb = gsel * EXPERTS_PER_GROUP + i2
    inv = 1.0 / (v1 + v2)
    gate_ref[...] = jnp.concatenate([v1 * inv, v2 * inv], axis=0)

    eid = lax.broadcasted_iota(jnp.int32, (N_EXPERTS, tm), 0)
    hit_a = eid == ea
    hit_b = eid == eb
    onehot = jnp.where(jnp.logical_or(hit_a, hit_b), 1.0, 0.0)
    before = (lax.broadcasted_iota(jnp.int32, (tm, tm), 0)
              < lax.broadcasted_iota(jnp.int32, (tm, tm), 1))
    upper = jnp.where(before, 1.0, 0.0).astype(BF16)
    rank = jnp.dot(onehot.astype(BF16), upper, preferred_element_type=F32)
    count = jnp.sum(onehot, axis=1, keepdims=True)
    seg_len = jnp.floor((count + (SEG_ALIGN - 1)) * (1.0 / SEG_ALIGN)) * SEG_ALIGN
    lower = (lax.broadcasted_iota(jnp.int32, (N_EXPERTS, N_EXPERTS), 1)
             < lax.broadcasted_iota(jnp.int32, (N_EXPERTS, N_EXPERTS), 0))
    seg_off = jnp.dot(jnp.where(lower, 1.0, 0.0).astype(BF16),
                      jnp.broadcast_to(seg_len, (N_EXPERTS, 128)).astype(BF16),
                      preferred_element_type=F32)[:, 0:1]
    row = rank + seg_off
    pa = jnp.sum(jnp.where(hit_a, row, 0.0), axis=0, keepdims=True)
    pb = jnp.sum(jnp.where(hit_b, row, 0.0), axis=0, keepdims=True)
    pos_ref[...] = jnp.concatenate([pa, pb], axis=0).astype(jnp.int32)
    cnt_ref[0] = jnp.broadcast_to(count, (N_EXPERTS, 128))


def _mixer_tail(o, lse, proj0, x, lw, batch, seq):
    n, d = x.shape
    tm = TOKEN_TILE
    nt = n // tm
    gw = GROUP_WIDTH
    hb = tm // CONV_HALO

    def tok(width):
        return pl.BlockSpec((tm, width), lambda i: (i, 0))

    def const(shape):
        return pl.BlockSpec(shape, lambda i: (0,) * len(shape))

    def grouped(r):
        return pl.BlockSpec((tm // r, r * gw), lambda i: (i, 0))

    rs = [r for _, r in DILATION_GROUPS]
    in_specs = (
        [grouped(r) for r in rs] * 2
        + [pl.BlockSpec((tm, d), lambda i: (i, 2)),
           pl.BlockSpec((CONV_HALO, d), lambda i: (jnp.maximum(i * hb - 1, 0), 2)),
           pl.BlockSpec((tm, 2 * d), lambda i: (i, 0)),
           tok(d)]
        + [const(lw[k].shape) for k in ("wao", "cw", "cb", "clg", "clb", "wco", "wo", "lng", "lnb", "rwt")]
    )
    lane_row = pl.BlockSpec((2, tm), lambda i: (0, i))
    out_specs = [tok(d), tok(d), lane_row, lane_row,
                 pl.BlockSpec((1, N_EXPERTS, 128), lambda i: (i, 0, 0))]
    out_shape = [jax.ShapeDtypeStruct((n, d), F32), jax.ShapeDtypeStruct((n, d), BF16),
                 jax.ShapeDtypeStruct((2, n), F32), jax.ShapeDtypeStruct((2, n), jnp.int32),
                 jax.ShapeDtypeStruct((nt, N_EXPERTS, 128), F32)]
    alpha = lw["alpha"]
    return pl.pallas_call(
        functools.partial(_post_kernel, tm=tm, seq=seq, alpha=alpha),
        grid=(nt,),
        in_specs=in_specs,
        out_specs=out_specs,
        out_shape=out_shape,
        scratch_shapes=[pltpu.VMEM((CONV_HALO + tm + CONV_TAIL, d // 2), F32), pltpu.VMEM((tm, d // 2), F32),
                        pltpu.VMEM((8, tm + 8, 128), F32), pltpu.VMEM((2 * N_GROUPS * (gw // 128), tm, 128), F32)],
        compiler_params=_params("arbitrary"),
        name="mixer_tail",
    )(o[0], o[1], o[2], lse[0], lse[1], lse[2], proj0, proj0, proj0, x,
      *[lw[k] for k in ("wao", "cw", "cb", "clg", "clb", "wco", "wo", "lng", "lnb", "rwt")])


def _pack_halves(v):
    h = v.shape[1] // 2
    bits = lax.bitcast_convert_type(v, jnp.uint32)
    return (bits[:, :h] >> 16) | (bits[:, h:] & jnp.uint32(0xFFFF0000))


def _unpack_halves(w):
    lo = lax.bitcast_convert_type(w << 16, F32).astype(BF16)
    hi = lax.bitcast_convert_type(w & jnp.uint32(0xFFFF0000), F32).astype(BF16)
    return lo, hi


def _dispatch_kernel(dst_ref, off_ref, len_ref, tot_ref, fill_row_ref, fill_len_ref, nact_ref,
                     x_ref, pos_ref, buf_hbm, xs, zeros, sem, zsem, *, nt, nb, blk):
    t = pl.program_id(0)
    cur = t % 2
    srows = xs.shape[1]

    def tile_wait(tile, buf):
        n = pl.multiple_of(tot_ref[tile], SEG_ALIGN)
        pltpu.make_async_copy(xs.at[buf, pl.ds(0, n)], buf_hbm.at[pl.ds(0, n)], sem.at[buf]).wait()

    @pl.when(t >= 2)
    def _():
        tile_wait(t - 2, cur)

    pos = pos_ref[...]
    row = lax.broadcasted_iota(jnp.int32, (srows, pos.shape[1]), 0)
    hit = jnp.logical_or(row == pos[0:1, :], row == pos[1:2, :])
    sel = jnp.where(hit, 1.0, 0.0).astype(BF16)
    xs[cur] = _pack_halves(jnp.dot(sel, x_ref[...], preferred_element_type=F32))

    for e in range(N_EXPERTS):
        k = t * N_EXPERTS + e
        n = pl.multiple_of(len_ref[k], SEG_ALIGN)
        src = pl.multiple_of(off_ref[k], SEG_ALIGN)
        dst = pl.multiple_of(dst_ref[k], SEG_ALIGN)

        @pl.when(n > 0)
        def _():
            pltpu.make_async_copy(xs.at[cur, pl.ds(src, n)], buf_hbm.at[pl.ds(dst, n)], sem.at[cur]).start()

    @pl.when(t == nt - 1)
    def _():
        if nt >= 2:
            tile_wait(t - 1, 1 - cur)
        tile_wait(t, cur)
        zeros[...] = jnp.zeros_like(zeros)

        def region_fill(e):
            n = pl.multiple_of(fill_len_ref[e], SEG_ALIGN)
            dst = pl.multiple_of(fill_row_ref[e], SEG_ALIGN)
            return n, pltpu.make_async_copy(zeros.at[pl.ds(0, n)], buf_hbm.at[pl.ds(dst, n)], zsem)

        def block_fill(b):
            return pltpu.make_async_copy(zeros, buf_hbm.at[pl.ds(pl.multiple_of(b * blk, blk), blk)], zsem)

        def each_block(fn):
            def body(b, carry):
                fn(block_fill(b))
                return carry
            lax.fori_loop(nact_ref[0], nb, body, 0)

        for e in range(N_EXPERTS):
            n, cp = region_fill(e)
            pl.when(n > 0)(cp.start)
        each_block(lambda cp: cp.start())
        for e in range(N_EXPERTS):
            n, cp = region_fill(e)
            pl.when(n > 0)(cp.wait)
        each_block(lambda cp: cp.wait())


def _dispatch(x1b, pos, tables, nb):
    n, d = x1b.shape
    tm = TOKEN_TILE
    nt = n // tm
    blk = ROW_BLOCK
    grid_spec = pltpu.PrefetchScalarGridSpec(
        num_scalar_prefetch=7,
        grid=(nt,),
        in_specs=[pl.BlockSpec((tm, d), lambda t, *_: (t, 0)),
                  pl.BlockSpec((2, tm), lambda t, *_: (0, t))],
        out_specs=pl.BlockSpec(memory_space=pl.ANY),
        scratch_shapes=[pltpu.VMEM((2, SORTED_ROWS, d // 2), jnp.uint32), pltpu.VMEM((blk, d // 2), jnp.uint32),
                        pltpu.SemaphoreType.DMA((2,)), pltpu.SemaphoreType.DMA],
    )
    return pl.pallas_call(
        functools.partial(_dispatch_kernel, nt=nt, nb=nb, blk=blk),
        grid_spec=grid_spec,
        out_shape=jax.ShapeDtypeStruct((nb * blk, d // 2), jnp.uint32),
        compiler_params=_params("arbitrary"),
        name="expert_dispatch",
    )(tables["dst"], tables["off"], tables["len"], tables["tot"], tables["fill_row"], tables["fill_len"],
      tables["n_act"], x1b, pos)


def _ffn_kernel(be_ref, nact_ref, xs_ref, wgu_ref, wdn_ref, y_ref, wgu_b, wdn_b):
    b = pl.program_id(0)
    f = wdn_ref.shape[2]
    half = xs_ref.shape[1]

    @pl.when(jnp.logical_or(b == 0, be_ref[b] != be_ref[jnp.maximum(b - 1, 0)]))
    def _():
        wgu_b[...] = wgu_ref[0, 0].astype(BF16)
        wdn_b[...] = wdn_ref[0, 0].astype(BF16)

    @pl.when(b < nact_ref[0])
    def _():
        x_lo, x_hi = _unpack_halves(xs_ref[...])
        h = (jnp.dot(x_lo, wgu_b[0:half, :], preferred_element_type=F32)
             + jnp.dot(x_hi, wgu_b[half:, :], preferred_element_type=F32))
        a = h[:, :f]
        act = (a * _sigmoid(a) * h[:, f:]).astype(BF16)
        y = jnp.dot(act, wdn_b[...], preferred_element_type=F32)
        y_ref[...] = _pack_halves(y.astype(BF16).astype(F32))

    @pl.when(b >= nact_ref[0])
    def _():
        y_ref[...] = jnp.zeros_like(y_ref)


def _expert_ffn(buf, wgu, wdn, layer, tables):
    n_rows, half = buf.shape
    d = 2 * half
    blk = ROW_BLOCK
    nb = n_rows // blk
    f2 = wgu.shape[3]
    f = wdn.shape[2]
    grid_spec = pltpu.PrefetchScalarGridSpec(
        num_scalar_prefetch=2,
        grid=(nb,),
        in_specs=[pl.BlockSpec((blk, half), lambda b, be, na: (jnp.minimum(b, na[0] - 1), 0)),
                  pl.BlockSpec((1, 1, d, f2), lambda b, be, na: (layer, be[b], 0, 0)),
                  pl.BlockSpec((1, 1, f, d), lambda b, be, na: (layer, be[b], 0, 0))],
        out_specs=pl.BlockSpec((blk, half), lambda b, be, na: (b, 0)),
        scratch_shapes=[pltpu.VMEM((d, f2), BF16), pltpu.VMEM((f, d), BF16)],
    )
    return pl.pallas_call(
        _ffn_kernel,
        grid_spec=grid_spec,
        out_shape=jax.ShapeDtypeStruct((n_rows, half), jnp.uint32),
        compiler_params=_params("arbitrary"),
        name="expert_ffn",
    )(tables["block_e"], tables["n_act"], buf, wgu, wdn)


def _merge_kernel(dst_ref, off_ref, len_ref, tot_ref, y_hbm, x1_ref, pos_ref, gt_ref, g_ref, b_ref,
                  x2_ref, x2b_ref, *rest, nt, alpha, views):
    view_refs, (ys, x2_scr, sem) = rest[:len(views)], rest[len(views):]
    t = pl.program_id(0)
    cur = t % 2
    tm, d = x1_ref.shape
    srows = ys.shape[1]

    def fetch(tile, buf):
        for e in range(N_EXPERTS):
            k = tile * N_EXPERTS + e
            n = pl.multiple_of(len_ref[k], SEG_ALIGN)
            dst = pl.multiple_of(off_ref[k], SEG_ALIGN)
            src = pl.multiple_of(dst_ref[k], SEG_ALIGN)

            @pl.when(n > 0)
            def _():
                pltpu.make_async_copy(y_hbm.at[pl.ds(src, n)], ys.at[buf, pl.ds(dst, n)], sem.at[buf]).start()

    @pl.when(t == 0)
    def _():
        ys[...] = jnp.zeros_like(ys)
        fetch(t, cur)

    @pl.when(t + 1 < nt)
    def _():
        fetch(t + 1, 1 - cur)

    n_all = pl.multiple_of(tot_ref[t], SEG_ALIGN)
    pltpu.make_async_copy(y_hbm.at[pl.ds(0, n_all)], ys.at[cur, pl.ds(0, n_all)], sem.at[cur]).wait()

    y_lo, y_hi = _unpack_halves(ys[cur])
    pos = pos_ref[...]
    gt = gt_ref[...]
    lane = lax.broadcasted_iota(jnp.int32, (tm, srows), 1)
    m = None
    for k in range(2):
        sel = jnp.where(lane == pos[:, k:k + 1], 1.0, 0.0).astype(BF16)
        yk = jnp.concatenate([jnp.dot(sel, y_lo, preferred_element_type=F32),
                              jnp.dot(sel, y_hi, preferred_element_type=F32)], axis=1)
        m = gt[:, k:k + 1] * yk if m is None else m + gt[:, k:k + 1] * yk
    x2 = _layer_norm(alpha * x1_ref[...] + m, g_ref[...], b_ref[...])
    x2_ref[...] = x2
    x2b_ref[...] = x2.astype(BF16)
    if views:
        for j in range(d // 128):
            x2_scr[j] = x2[:, j * 128:(j + 1) * 128]
    for ref, r in zip(view_refs, views):
        for cls in range(r):
            for j in range(d // 128):
                lanes = slice(cls * d + j * 128, cls * d + (j + 1) * 128)
                ref[:, lanes] = x2_scr[j, pl.ds(cls, tm // r, stride=r), :].astype(BF16)


def _expert_merge(y_buf, pos_t, gate_t, x1, g, b, alpha, tables, views):
    n, d = x1.shape
    tm = TOKEN_TILE
    nt = n // tm
    tok = pl.BlockSpec((tm, d), lambda t, *_: (t, 0))
    pair = pl.BlockSpec((tm, 2), lambda t, *_: (t, 0))
    vec = pl.BlockSpec((1, d), lambda t, *_: (0, 0))
    grid_spec = pltpu.PrefetchScalarGridSpec(
        num_scalar_prefetch=4,
        grid=(nt,),
        in_specs=[pl.BlockSpec(memory_space=pl.ANY), tok, pair, pair, vec, vec],
        out_specs=[tok, tok] + [pl.BlockSpec((tm // r, r * d), lambda t, *_: (t, 0)) for r in views],
        scratch_shapes=[pltpu.VMEM((2, SORTED_ROWS, d // 2), jnp.uint32), pltpu.VMEM((d // 128, tm, 128), F32),
                        pltpu.SemaphoreType.DMA((2,))],
    )
    return pl.pallas_call(
        functools.partial(_merge_kernel, nt=nt, alpha=alpha, views=views),
        grid_spec=grid_spec,
        out_shape=[jax.ShapeDtypeStruct((n, d), F32), jax.ShapeDtypeStruct((n, d), BF16)]
        + [jax.ShapeDtypeStruct((n // r, r * d), BF16) for r in views],
        compiler_params=_params("arbitrary"),
        name="expert_merge",
    )(tables["dst"], tables["off"], tables["len"], tables["tot"], y_buf, x1, pos_t, gate_t, g, b)


def _routing_tables(counts, nb):
    blk = ROW_BLOCK
    nt = counts.shape[0]
    seg = (counts + SEG_ALIGN - 1) // SEG_ALIGN * SEG_ALIGN
    off = jnp.cumsum(seg, axis=1) - seg
    used = jnp.sum(seg, axis=0)
    region = (used + blk - 1) // blk * blk
    region_end = jnp.cumsum(region)
    region_start = region_end - region
    dst = region_start[None, :] + jnp.cumsum(seg, axis=0) - seg
    n_act = region_end[-1] // blk
    block_row0 = jnp.minimum(jnp.arange(nb, dtype=jnp.int32), n_act - 1) * blk
    block_e = jnp.sum(region_end[None, :] <= block_row0[:, None], axis=1)
    i32 = lambda v: v.astype(jnp.int32)
    return dict(dst=i32(dst.reshape(-1)), off=i32(off.reshape(-1)), len=i32(seg.reshape(-1)),
                tot=i32(jnp.sum(seg, axis=1)), fill_row=i32(region_start + used), fill_len=i32(region - used),
                n_act=i32(n_act.reshape(1)), block_e=i32(block_e))


def kernel(x, w_in, w_attn_out, w_conv_out, w_o, conv_w, conv_b, conv_ln_g, conv_ln_b, ln_mix_g, ln_mix_b,
           expert_w_gate_up, expert_w_down, ln_ffn_g, ln_ffn_b, router_w, rel_bias):
    batch, seq, d = x.shape
    depth = w_in.shape[0]
    n = batch * seq
    aw, gw = ATTN_WIDTH, GROUP_WIDTH
    assert d % 256 == 0 and n % 1024 == 0
    for window, r in DILATION_GROUPS:
        assert window // r == BAND and seq % (r * BAND) == 0
    alpha = (2 * depth) ** 0.25
    scale = HEAD_DIM ** -0.5

    biases = [_band_bias(rel_bias, gi, r) for gi, (_, r) in enumerate(DILATION_GROUPS)]
    rwt = router_w.T.astype(BF16)
    nt = n // TOKEN_TILE
    nb = -(-(2 * n + N_EXPERTS * (nt * (SEG_ALIGN - 1) + ROW_BLOCK - 1)) // ROW_BLOCK)
    dilations = tuple(r for _, r in DILATION_GROUPS if r > 1)

    xf = x.reshape(n, d)
    xb = xf.astype(BF16)
    xviews = {r: xb.reshape(n // r, r * d) for r in dilations}
    for l in range(depth):
        wl = w_in[l]

        def qkv_cols(gi):
            return jnp.concatenate([wl[:, gi * gw:(gi + 1) * gw] * scale,
                                    wl[:, aw + gi * gw:aw + (gi + 1) * gw],
                                    wl[:, 2 * aw + gi * gw:2 * aw + (gi + 1) * gw]], axis=1)

        w0 = jnp.concatenate([wl[:, 3 * aw + d:], wl[:, 3 * aw:3 * aw + d], qkv_cols(0)], axis=1).astype(BF16)
        proj0 = _project(xb, w0, 1024, 768)
        outs, lses = [], []
        for gi, (_, r) in enumerate(DILATION_GROUPS):
            if r == 1:
                qkv, col0 = proj0.reshape(batch, seq, proj0.shape[1]), (3 * d) // gw
            else:
                qkv = _project_dilated(xviews[r], qkv_cols(gi).astype(BF16), batch, seq, r)
                qkv, col0 = qkv.reshape(batch * r, seq // r, 3 * gw), 0
            o, lse = _attention(qkv, biases[gi], batch, seq, r, col0)
            outs.append(o)
            lses.append(lse)

        lw = dict(wao=w_attn_out[l].astype(BF16), cw=conv_w[l], cb=conv_b[l][None], clg=conv_ln_g[l][None],
                  clb=conv_ln_b[l][None], wco=w_conv_out[l].astype(BF16), wo=w_o[l].astype(BF16),
                  lng=ln_mix_g[l][None], lnb=ln_mix_b[l][None], rwt=rwt, alpha=alpha)
        x1, x1b, gate, pos, cnt = _mixer_tail(outs, lses, proj0, xf, lw, batch, seq)

        tables = _routing_tables(cnt[:, :, 0].astype(jnp.int32), nb)
        buf = _dispatch(x1b, pos, tables, nb)
        y_buf = _expert_ffn(buf, expert_w_gate_up, expert_w_down, l, tables)
        views = dilations if l + 1 < depth else ()
        xf, xb, *xv = _expert_merge(y_buf, pos.T, gate.T, x1, ln_ffn_g[l][None], ln_ffn_b[l][None], alpha,
                                    tables, views)
        xviews = dict(zip(views, xv))
    return xf.reshape(batch, seq, d)
```

```python
import functools
import math

import jax
import jax.numpy as jnp
from jax import lax
from jax.experimental import pallas as pl
from jax.experimental.pallas import tpu as pltpu

F32 = jnp.float32
BF16 = jnp.bfloat16

HEAD_DIM = 64
HEADS_PER_GROUP = 4
GROUP_WIDTH = HEADS_PER_GROUP * HEAD_DIM
DILATION_GROUPS = ((128, 1), (512, 4), (2048, 16))
N_GROUPS = len(DILATION_GROUPS)
ATTN_WIDTH = N_GROUPS * GROUP_WIDTH
BAND = 128
CONV_KERNEL = 31
CONV_HALO = 32
CONV_TAIL = 16
N_BUCKETS = 32
MAX_DISTANCE = 2048
N_EXPERTS = 16
EXPERTS_PER_GROUP = 4
LN_EPS = 1e-5
MASKED = -1e30

ROW_BLOCK = 256
TOKEN_TILE = 256
SEG_ALIGN = 8
SORTED_ROWS = -(-(2 * TOKEN_TILE + N_EXPERTS * (SEG_ALIGN - 1)) // 128) * 128
VMEM_LIMIT = 56 * 1024 * 1024


def _sigmoid(v):
    return 1.0 / (1.0 + jnp.exp(-v))


def _layer_norm(z, g, b):
    mu = jnp.mean(z, axis=-1, keepdims=True)
    zc = z - mu
    var = jnp.mean(zc * zc, axis=-1, keepdims=True)
    return zc * lax.rsqrt(var + LN_EPS) * g + b


def _params(*sem):
    return pltpu.CompilerParams(dimension_semantics=sem, vmem_limit_bytes=VMEM_LIMIT)


def _mm_kernel(x_ref, w_ref, o_ref):
    o_ref[...] = jnp.dot(x_ref[...], w_ref[...], preferred_element_type=F32).astype(o_ref.dtype)


def _project(xb, w, tm):
    n, d = xb.shape
    c = w.shape[1]
    return pl.pallas_call(
        _mm_kernel,
        grid=(n // tm,),
        in_specs=[pl.BlockSpec((tm, d), lambda i: (i, 0)),
                  pl.BlockSpec((d, c), lambda i: (0, 0))],
        out_specs=pl.BlockSpec((tm, c), lambda i: (i, 0)),
        out_shape=jax.ShapeDtypeStruct((n, c), BF16),
        compiler_params=_params("arbitrary"),
        name="proj_natural",
    )(xb, w)


def _project_dilated(xv, w, batch, seq, r):
    d, c = w.shape
    n = xv.shape[0] * r
    sub = seq // r
    tm = min(2048, sub)
    nl = sub // tm
    return pl.pallas_call(
        _mm_kernel,
        grid=(batch, r, nl),
        in_specs=[pl.BlockSpec((tm, d), lambda b, cc, j: (b * nl + j, cc)),
                  pl.BlockSpec((d, c), lambda b, cc, j: (0, 0))],
        out_specs=pl.BlockSpec((tm, c), lambda b, cc, j: ((b * r + cc) * nl + j, 0)),
        out_shape=jax.ShapeDtypeStruct((n, c), BF16),
        compiler_params=_params("arbitrary", "arbitrary", "arbitrary"),
        name=f"proj_dilated_{r}",
    )(xv, w)


def _attn_kernel(q_ref, kp_ref, vp_ref, kc_ref, vc_ref, bias_ref, o_ref, lse_ref, k_scr, v_scr, *, nqb):
    i = pl.program_id(1)
    k_scr[0:BAND, :] = kp_ref[0]
    k_scr[BAND:, :] = kc_ref[0]
    v_scr[0:BAND, :] = vp_ref[0]
    v_scr[BAND:, :] = vc_ref[0]
    col = lax.broadcasted_iota(jnp.int32, (BAND, 2 * BAND), 1)
    first_head = lax.broadcasted_iota(jnp.int32, (BAND, 2 * HEAD_DIM), 1) < HEAD_DIM

    def body(j, carry):
        r0 = pl.multiple_of(j * BAND, BAND)
        q = q_ref[0, pl.ds(r0, BAND), :]
        kk = k_scr[pl.ds(r0, 2 * BAND), :]
        vv = v_scr[pl.ds(r0, 2 * BAND), :]
        no_prev = jnp.logical_and(jnp.logical_and(i == 0, j == 0), col < BAND)
        outs, lses = [], []
        for pair in range(HEADS_PER_GROUP // 2):
            sl = slice(pair * 2 * HEAD_DIM, (pair + 1) * 2 * HEAD_DIM)
            q2, k2, v2 = q[:, sl], kk[:, sl], vv[:, sl]
            o_pair, lse_pair = None, None
            for hh in range(2):
                mine = first_head if hh == 0 else jnp.logical_not(first_head)
                qm = jnp.where(mine, q2, jnp.zeros_like(q2))
                s = lax.dot_general(qm, k2, (((1,), (1,)), ((), ())), preferred_element_type=F32)
                s = s + bias_ref[2 * pair + hh]
                s = jnp.where(no_prev, MASKED, s)
                m = jnp.max(s, axis=-1, keepdims=True)
                p = jnp.exp(s - m)
                den = jnp.sum(p, axis=-1, keepdims=True)
                o = jnp.dot(p.astype(BF16), v2, preferred_element_type=F32) * (1.0 / den)
                lse = jnp.broadcast_to(m + jnp.log(den), (BAND, 2 * HEAD_DIM))
                o_pair = o if hh == 0 else jnp.where(first_head, o_pair, o)
                lse_pair = lse if hh == 0 else jnp.where(first_head, lse_pair, lse)
            outs.append(o_pair)
            lses.append(lse_pair)
        o_ref[0, pl.ds(r0, BAND), :] = jnp.concatenate(outs, axis=1).astype(o_ref.dtype)
        lse_ref[0, pl.ds(r0, BAND), :] = jnp.concatenate(lses, axis=1)
        return carry

    lax.fori_loop(0, nqb, body, 0, unroll=math.gcd(nqb, 4))


def _attention(qkv, bias, batch, seq, r, col0):
    sub = seq // r
    nqb = min(8, sub // BAND)
    rows = nqb * BAND
    nt = sub // rows
    gw = GROUP_WIDTH

    def cur(c):
        return pl.BlockSpec((1, rows, gw), lambda bc, i: (bc, i, c))

    def prev(c):
        return pl.BlockSpec((1, BAND, gw), lambda bc, i: (bc, jnp.maximum(i * nqb - 1, 0), c))

    out_spec = pl.BlockSpec((1, rows, gw), lambda bc, i: (bc // r, i, bc % r))
    o, lse = pl.pallas_call(
        functools.partial(_attn_kernel, nqb=nqb),
        grid=(batch * r, nt),
        in_specs=[cur(col0), prev(col0 + 1), prev(col0 + 2), cur(col0 + 1), cur(col0 + 2),
                  pl.BlockSpec((HEADS_PER_GROUP, BAND, 2 * BAND), lambda bc, i: (0, 0, 0))],
        out_specs=[out_spec, out_spec],
        out_shape=[jax.ShapeDtypeStruct((batch, sub, r * gw), BF16),
                   jax.ShapeDtypeStruct((batch, sub, r * gw), F32)],
        scratch_shapes=[pltpu.VMEM((rows + BAND, gw), BF16), pltpu.VMEM((rows + BAND, gw), BF16)],
        compiler_params=_params("arbitrary", "arbitrary"),
        name=f"attn_dilation_{r}",
    )(qkv, qkv, qkv, qkv, qkv, bias)
    return o.reshape(batch * sub, r * gw), lse.reshape(batch * sub, r * gw)


def _t5_bucket(dist):
    max_exact = N_BUCKETS // 2
    n = jnp.maximum(dist, 0)
    nf = jnp.maximum(n, 1).astype(F32)
    large = max_exact + (jnp.log(nf / max_exact) / math.log(MAX_DISTANCE / max_exact)
                         * (N_BUCKETS - max_exact)).astype(jnp.int32)
    large = jnp.minimum(large, N_BUCKETS - 1)
    return jnp.where(n < max_exact, n, large)


def _band_bias(rel_bias, gi, r):
    qi = jnp.arange(BAND)[:, None]
    kj = jnp.arange(2 * BAND)[None, :]
    dist = qi + BAND - kj
    bucket = _t5_bucket(dist * r)
    valid = (dist >= 0) & (dist <= BAND)
    b = jnp.full((HEADS_PER_GROUP, BAND, 2 * BAND), MASKED, F32)
    for k in range(N_BUCKETS):
        row = rel_bias[k, gi * HEADS_PER_GROUP:(gi + 1) * HEADS_PER_GROUP].astype(F32)
        b = jnp.where(((bucket == k) & valid)[None], row[:, None, None], b)
    return b


def _post_kernel(o0_ref, o1_ref, o2_ref, l0_ref, l1_ref, l2_ref, u_ref, uh_ref, g_ref, x_ref,
                 wao_ref, cw_ref, cb_ref, clg_ref, clb_ref, wco_ref, wo_ref, lng_ref, lnb_ref, rwt_ref,
                 x1_ref, x1b_ref, gate_ref, pos_ref, cnt_ref,
                 hs_scr, conv_scr, ys_scr, nat_scr, *, tm, seq, alpha):
    i = pl.program_id(0)
    d = x_ref.shape[1]
    c = d // 2
    gw = GROUP_WIDTH

    def natural(ref, slot, r):
        if r == 1:
            return ref[...].astype(F32)
        nl = gw // 128
        for cls in range(r):
            for j in range(nl):
                lanes = slice(cls * gw + j * 128, cls * gw + (j + 1) * 128)
                nat_scr[slot * nl + j, pl.ds(cls, tm // r, stride=r), :] = ref[:, lanes].astype(F32)
        return jnp.concatenate([nat_scr[slot * nl + j] for j in range(nl)], axis=1)

    rs = [r for _, r in DILATION_GROUPS]
    o0, o1, o2 = (natural(ref, s, r) for s, (ref, r) in enumerate(zip((o0_ref, o1_ref, o2_ref), rs)))
    l0, l1, l2 = (natural(ref, 3 + s, r) for s, (ref, r) in enumerate(zip((l0_ref, l1_ref, l2_ref), rs)))
    mx = jnp.maximum(jnp.maximum(l0, l1), l2)
    e0, e1, e2 = jnp.exp(l0 - mx), jnp.exp(l1 - mx), jnp.exp(l2 - mx)
    attn = (e0 * o0 + e1 * o1 + e2 * o2) * (1.0 / (e0 + e1 + e2))
    attn_branch = jnp.dot(attn.astype(BF16), wao_ref[...], preferred_element_type=F32)

    u = u_ref[...].astype(F32)
    uh = uh_ref[...].astype(F32)
    hh = uh[:, :c] * _sigmoid(uh[:, c:])
    seq_start = (i * tm) % seq == 0
    hs_scr[0:CONV_HALO, :] = jnp.where(seq_start, 0.0, hh)
    hs_scr[CONV_HALO:CONV_HALO + tm, :] = u[:, :c] * _sigmoid(u[:, c:])
    hs_scr[CONV_HALO + tm:, :] = jnp.zeros((CONV_TAIL, c), F32)
    off = CONV_HALO - (CONV_KERNEL - 1)
    rows_y = tm + 8
    for cc in range(c // 128):
        ls = slice(cc * 128, (cc + 1) * 128)
        acc = None
        for s in range(8):
            y = None
            for a in range((off + CONV_KERNEL + 7) // 8):
                k = 8 * a + s - off
                if 0 <= k < CONV_KERNEL:
                    term = hs_scr[8 * a:8 * a + rows_y, ls] * cw_ref[k:k + 1, ls]
                    y = term if y is None else y + term
            if s == 0:
                acc = y[0:tm]
            else:
                ys_scr[s] = y
                acc = acc + ys_scr[s, s:s + tm, :]
        conv_scr[:, ls] = acc
    hc = _layer_norm(conv_scr[...] + cb_ref[...], clg_ref[...], clb_ref[...])
    hc = hc * _sigmoid(hc)
    conv_branch = jnp.dot(hc.astype(BF16), wco_ref[...], preferred_element_type=F32)

    gates = _sigmoid(g_ref[...].astype(F32))
    merged = gates[:, :d] * attn_branch + gates[:, d:] * conv_branch
    hmix = jnp.dot(merged.astype(BF16), wo_ref[...], preferred_element_type=F32)
    x1 = _layer_norm(alpha * x_ref[...] + hmix, lng_ref[...], lnb_ref[...])
    x1_ref[...] = x1
    x1b = x1.astype(BF16)
    x1b_ref[...] = x1b

    logits = lax.dot_general(rwt_ref[...], x1b, (((1,), (1,)), ((), ())), preferred_element_type=F32)
    ex = jnp.exp(logits - jnp.max(logits, axis=0, keepdims=True))
    rows = [ex[e:e + 1, :] for e in range(N_EXPERTS)]
    best, gsel = None, None
    for g in range(N_EXPERTS // EXPERTS_PER_GROUP):
        v = rows[g * EXPERTS_PER_GROUP:(g + 1) * EXPERTS_PER_GROUP]
        score = None
        for a in range(EXPERTS_PER_GROUP):
            for b in range(a + 1, EXPERTS_PER_GROUP):
                ps = v[a] + v[b]
                score = ps if score is None else jnp.maximum(score, ps)
        if best is None:
            best, gsel = score, jnp.zeros_like(score, dtype=jnp.int32)
        else:
            upd = score > best
            gsel = jnp.where(upd, g, gsel)
            best = jnp.where(upd, score, best)
    vals = []
    for j in range(EXPERTS_PER_GROUP):
        vj = rows[j]
        for g in range(1, N_EXPERTS // EXPERTS_PER_GROUP):
            vj = jnp.where(gsel == g, rows[g * EXPERTS_PER_GROUP + j], vj)
        vals.append(vj)
    v1, i1 = vals[0], jnp.zeros_like(gsel)
    for j in range(1, EXPERTS_PER_GROUP):
        upd = vals[j] > v1
        i1 = jnp.where(upd, j, i1)
        v1 = jnp.where(upd, vals[j], v1)
    v2, i2 = jnp.full_like(v1, -1.0), jnp.zeros_like(gsel)
    for j in range(EXPERTS_PER_GROUP):
        upd = jnp.logical_and(i1 != j, vals[j] > v2)
        i2 = jnp.where(upd, j, i2)
        v2 = jnp.where(upd, vals[j], v2)
    ea = gsel * EXPERTS_PER_GROUP + i1
    eb = gsel * EXPERTS_PER_GROUP + i2
    inv = 1.0 / (v1 + v2)
    gate_ref[...] = jnp.concatenate([v1 * inv, v2 * inv], axis=0)

    eid = lax.broadcasted_iota(jnp.int32, (N_EXPERTS, tm), 0)
    hit_a = eid == ea
    hit_b = eid == eb
    onehot = jnp.where(jnp.logical_or(hit_a, hit_b), 1.0, 0.0)
    before = (lax.broadcasted_iota(jnp.int32, (tm, tm), 0)
              < lax.broadcasted_iota(jnp.int32, (tm, tm), 1))
    upper = jnp.where(before, 1.0, 0.0).astype(BF16)
    rank = jnp.dot(onehot.astype(BF16), upper, preferred_element_type=F32)
    count = jnp.sum(onehot, axis=1, keepdims=True)
    seg_len = jnp.floor((count + (SEG_ALIGN - 1)) * (1.0 / SEG_ALIGN)) * SEG_ALIGN
    lower = (lax.broadcasted_iota(jnp.int32, (N_EXPERTS, N_EXPERTS), 1)
             < lax.broadcasted_iota(jnp.int32, (N_EXPERTS, N_EXPERTS), 0))
    seg_off = jnp.dot(jnp.where(lower, 1.0, 0.0).astype(BF16),
                      jnp.broadcast_to(seg_len, (N_EXPERTS, 128)).astype(BF16),
                      preferred_element_type=F32)[:, 0:1]
    row = rank + seg_off
    pa = jnp.sum(jnp.where(hit_a, row, 0.0), axis=0, keepdims=True)
    pb = jnp.sum(jnp.where(hit_b, row, 0.0), axis=0, keepdims=True)
    pos_ref[...] = jnp.concatenate([pa, pb], axis=0).astype(jnp.int32)
    cnt_ref[0] = jnp.broadcast_to(count, (N_EXPERTS, 128))


def _mixer_tail(o, lse, proj0, x, lw, batch, seq):
    n, d = x.shape
    tm = TOKEN_TILE
    nt = n // tm
    gw = GROUP_WIDTH
    hb = tm // CONV_HALO

    def tok(width):
        return pl.BlockSpec((tm, width), lambda i: (i, 0))

    def const(shape):
        return pl.BlockSpec(shape, lambda i: (0,) * len(shape))

    def grouped(r):
        return pl.BlockSpec((tm // r, r * gw), lambda i: (i, 0))

    rs = [r for _, r in DILATION_GROUPS]
    in_specs = (
        [grouped(r) for r in rs] * 2
        + [pl.BlockSpec((tm, d), lambda i: (i, 2)),
           pl.BlockSpec((CONV_HALO, d), lambda i: (jnp.maximum(i * hb - 1, 0), 2)),
           pl.BlockSpec((tm, 2 * d), lambda i: (i, 0)),
           tok(d)]
        + [const(lw[k].shape) for k in ("wao", "cw", "cb", "clg", "clb", "wco", "wo", "lng", "lnb", "rwt")]
    )
    lane_row = pl.BlockSpec((2, tm), lambda i: (0, i))
    out_specs = [tok(d), tok(d), lane_row, lane_row,
                 pl.BlockSpec((1, N_EXPERTS, 128), lambda i: (i, 0, 0))]
    out_shape = [jax.ShapeDtypeStruct((n, d), F32), jax.ShapeDtypeStruct((n, d), BF16),
                 jax.ShapeDtypeStruct((2, n), F32), jax.ShapeDtypeStruct((2, n), jnp.int32),
                 jax.ShapeDtypeStruct((nt, N_EXPERTS, 128), F32)]
    alpha = lw["alpha"]
    return pl.pallas_call(
        functools.partial(_post_kernel, tm=tm, seq=seq, alpha=alpha),
        grid=(nt,),
        in_specs=in_specs,
        out_specs=out_specs,
        out_shape=out_shape,
        scratch_shapes=[pltpu.VMEM((CONV_HALO + tm + CONV_TAIL, d // 2), F32), pltpu.VMEM((tm, d // 2), F32),
                        pltpu.VMEM((8, tm + 8, 128), F32),
                        pltpu.VMEM((2 * N_GROUPS * (gw // 128), tm, 128), F32)],
        compiler_params=_params("arbitrary"),
        name="mixer_tail",
    )(o[0], o[1], o[2], lse[0], lse[1], lse[2], proj0, proj0, proj0, x,
      *[lw[k] for k in ("wao", "cw", "cb", "clg", "clb", "wco", "wo", "lng", "lnb", "rwt")])


def _pack_halves(v):
    h = v.shape[1] // 2
    bits = lax.bitcast_convert_type(v, jnp.uint32)
    return (bits[:, :h] >> 16) | (bits[:, h:] & jnp.uint32(0xFFFF0000))


def _unpack_halves(w):
    lo = lax.bitcast_convert_type(w << 16, F32).astype(BF16)
    hi = lax.bitcast_convert_type(w & jnp.uint32(0xFFFF0000), F32).astype(BF16)
    return lo, hi


def _dispatch_kernel(dst_ref, off_ref, len_ref, tot_ref, fill_row_ref, fill_len_ref, nact_ref,
                     x_ref, pos_ref, buf_hbm, xs, zeros, sem, zsem, *, nt, nb, blk):
    t = pl.program_id(0)
    cur = t % 2
    srows = xs.shape[1]

    def tile_wait(tile, buf):
        n = pl.multiple_of(tot_ref[tile], SEG_ALIGN)
        pltpu.make_async_copy(xs.at[buf, pl.ds(0, n)], buf_hbm.at[pl.ds(0, n)], sem.at[buf]).wait()

    @pl.when(t >= 2)
    def _():
        tile_wait(t - 2, cur)

    pos = pos_ref[...]
    row = lax.broadcasted_iota(jnp.int32, (srows, pos.shape[1]), 0)
    hit = jnp.logical_or(row == pos[0:1, :], row == pos[1:2, :])
    sel = jnp.where(hit, 1.0, 0.0).astype(BF16)
    xs[cur] = _pack_halves(jnp.dot(sel, x_ref[...], preferred_element_type=F32))

    for e in range(N_EXPERTS):
        k = t * N_EXPERTS + e
        n = pl.multiple_of(len_ref[k], SEG_ALIGN)
        src = pl.multiple_of(off_ref[k], SEG_ALIGN)
        dst = pl.multiple_of(dst_ref[k], SEG_ALIGN)

        @pl.when(n > 0)
        def _():
            pltpu.make_async_copy(xs.at[cur, pl.ds(src, n)], buf_hbm.at[pl.ds(dst, n)], sem.at[cur]).start()

    @pl.when(t == nt - 1)
    def _():
        if nt >= 2:
            tile_wait(t - 1, 1 - cur)
        tile_wait(t, cur)
        zeros[...] = jnp.zeros_like(zeros)

        def region_fill(e):
            n = pl.multiple_of(fill_len_ref[e], SEG_ALIGN)
            dst = pl.multiple_of(fill_row_ref[e], SEG_ALIGN)
            return n, pltpu.make_async_copy(zeros.at[pl.ds(0, n)], buf_hbm.at[pl.ds(dst, n)], zsem)

        def block_fill(b):
            return pltpu.make_async_copy(zeros, buf_hbm.at[pl.ds(pl.multiple_of(b * blk, blk), blk)], zsem)

        def each_block(fn):
            def body(b, carry):
                fn(block_fill(b))
                return carry
            lax.fori_loop(nact_ref[0], nb, body, 0)

        for e in range(N_EXPERTS):
            n, cp = region_fill(e)
            pl.when(n > 0)(cp.start)
        each_block(lambda cp: cp.start())
        for e in range(N_EXPERTS):
            n, cp = region_fill(e)
            pl.when(n > 0)(cp.wait)
        each_block(lambda cp: cp.wait())


def _dispatch(x1b, pos, tables, nb):
    n, d = x1b.shape
    tm = TOKEN_TILE
    nt = n // tm
    blk = ROW_BLOCK
    grid_spec = pltpu.PrefetchScalarGridSpec(
        num_scalar_prefetch=7,
        grid=(nt,),
        in_specs=[pl.BlockSpec((tm, d), lambda t, *_: (t, 0)),
                  pl.BlockSpec((2, tm), lambda t, *_: (0, t))],
        out_specs=pl.BlockSpec(memory_space=pl.ANY),
        scratch_shapes=[pltpu.VMEM((2, SORTED_ROWS, d // 2), jnp.uint32), pltpu.VMEM((blk, d // 2), jnp.uint32),
                        pltpu.SemaphoreType.DMA((2,)), pltpu.SemaphoreType.DMA],
    )
    return pl.pallas_call(
        functools.partial(_dispatch_kernel, nt=nt, nb=nb, blk=blk),
        grid_spec=grid_spec,
        out_shape=jax.ShapeDtypeStruct((nb * blk, d // 2), jnp.uint32),
        compiler_params=_params("arbitrary"),
        name="expert_dispatch",
    )(tables["dst"], tables["off"], tables["len"], tables["tot"], tables["fill_row"], tables["fill_len"],
      tables["n_act"], x1b, pos)


def _ffn_kernel(be_ref, nact_ref, xs_ref, wgu_ref, wdn_ref, y_ref, wgu_b, wdn_b):
    b = pl.program_id(0)
    f = wdn_ref.shape[2]
    half = xs_ref.shape[1]

    @pl.when(jnp.logical_or(b == 0, be_ref[b] != be_ref[jnp.maximum(b - 1, 0)]))
    def _():
        wgu_b[...] = wgu_ref[0, 0].astype(BF16)
        wdn_b[...] = wdn_ref[0, 0].astype(BF16)

    @pl.when(b < nact_ref[0])
    def _():
        x_lo, x_hi = _unpack_halves(xs_ref[...])
        h = (jnp.dot(x_lo, wgu_b[0:half, :], preferred_element_type=F32)
             + jnp.dot(x_hi, wgu_b[half:, :], preferred_element_type=F32))
        a = h[:, :f]
        act = (a * _sigmoid(a) * h[:, f:]).astype(BF16)
        y = jnp.dot(act, wdn_b[...], preferred_element_type=F32)
        y_ref[...] = _pack_halves(y.astype(BF16).astype(F32))

    @pl.when(b >= nact_ref[0])
    def _():
        y_ref[...] = jnp.zeros_like(y_ref)


def _expert_ffn(buf, wgu, wdn, layer, tables):
    n_rows, half = buf.shape
    d = 2 * half
    blk = ROW_BLOCK
    nb = n_rows // blk
    f2 = wgu.shape[3]
    f = wdn.shape[2]
    grid_spec = pltpu.PrefetchScalarGridSpec(
        num_scalar_prefetch=2,
        grid=(nb,),
        in_specs=[pl.BlockSpec((blk, half), lambda b, be, na: (jnp.minimum(b, na[0] - 1), 0)),
                  pl.BlockSpec((1, 1, d, f2), lambda b, be, na: (layer, be[b], 0, 0)),
                  pl.BlockSpec((1, 1, f, d), lambda b, be, na: (layer, be[b], 0, 0))],
        out_specs=pl.BlockSpec((blk, half), lambda b, be, na: (b, 0)),
        scratch_shapes=[pltpu.VMEM((d, f2), BF16), pltpu.VMEM((f, d), BF16)],
    )
    return pl.pallas_call(
        _ffn_kernel,
        grid_spec=grid_spec,
        out_shape=jax.ShapeDtypeStruct((n_rows, half), jnp.uint32),
        compiler_params=_params("arbitrary"),
        name="expert_ffn",
    )(tables["block_e"], tables["n_act"], buf, wgu, wdn)


def _store_class_views(xt, scr, view_refs, views):
    tm, d = xt.shape
    if views:
        for j in range(d // 128):
            scr[j] = xt[:, j * 128:(j + 1) * 128]
    for ref, r in zip(view_refs, views):
        for cls in range(r):
            for j in range(d // 128):
                lanes = slice(cls * d + j * 128, cls * d + (j + 1) * 128)
                ref[:, lanes] = scr[j, pl.ds(cls, tm // r, stride=r), :].astype(BF16)


def _views_kernel(x_ref, xb_ref, *rest, views):
    view_refs, scr = rest[:len(views)], rest[len(views)]
    xt = x_ref[...]
    xb_ref[...] = xt.astype(BF16)
    _store_class_views(xt, scr, view_refs, views)


def _input_views(xf, views):
    n, d = xf.shape
    tm = TOKEN_TILE
    tok = pl.BlockSpec((tm, d), lambda t: (t, 0))
    return pl.pallas_call(
        functools.partial(_views_kernel, views=views),
        grid=(n // tm,),
        in_specs=[tok],
        out_specs=[tok] + [pl.BlockSpec((tm // r, r * d), lambda t: (t, 0)) for r in views],
        out_shape=[jax.ShapeDtypeStruct((n, d), BF16)]
        + [jax.ShapeDtypeStruct((n // r, r * d), BF16) for r in views],
        scratch_shapes=[pltpu.VMEM((d // 128, tm, 128), F32)],
        compiler_params=_params("arbitrary"),
        name="input_views",
    )(xf)


def _merge_kernel(dst_ref, off_ref, len_ref, tot_ref, y_hbm, x1_ref, pos_ref, gt_ref, g_ref, b_ref,
                  x2_ref, x2b_ref, *rest, nt, alpha, views):
    view_refs, (ys, x2_scr, sem) = rest[:len(views)], rest[len(views):]
    t = pl.program_id(0)
    cur = t % 2
    tm, d = x1_ref.shape
    srows = ys.shape[1]

    def fetch(tile, buf):
        for e in range(N_EXPERTS):
            k = tile * N_EXPERTS + e
            n = pl.multiple_of(len_ref[k], SEG_ALIGN)
            dst = pl.multiple_of(off_ref[k], SEG_ALIGN)
            src = pl.multiple_of(dst_ref[k], SEG_ALIGN)

            @pl.when(n > 0)
            def _():
                pltpu.make_async_copy(y_hbm.at[pl.ds(src, n)], ys.at[buf, pl.ds(dst, n)], sem.at[buf]).start()

    @pl.when(t == 0)
    def _():
        ys[...] = jnp.zeros_like(ys)
        fetch(t, cur)

    @pl.when(t + 1 < nt)
    def _():
        fetch(t + 1, 1 - cur)

    n_all = pl.multiple_of(tot_ref[t], SEG_ALIGN)
    pltpu.make_async_copy(y_hbm.at[pl.ds(0, n_all)], ys.at[cur, pl.ds(0, n_all)], sem.at[cur]).wait()

    y_lo, y_hi = _unpack_halves(ys[cur])
    pos = pos_ref[...]
    gt = gt_ref[...]
    lane = lax.broadcasted_iota(jnp.int32, (tm, srows), 1)
    sel = (jnp.where(lane == pos[:, 0:1], gt[:, 0:1], 0.0)
           + jnp.where(lane == pos[:, 1:2], gt[:, 1:2], 0.0)).astype(BF16)
    m = jnp.concatenate([jnp.dot(sel, y_lo, preferred_element_type=F32),
                         jnp.dot(sel, y_hi, preferred_element_type=F32)], axis=1)
    x2 = _layer_norm(alpha * x1_ref[...] + m, g_ref[...], b_ref[...])
    x2_ref[...] = x2
    x2b_ref[...] = x2.astype(BF16)
    _store_class_views(x2, x2_scr, view_refs, views)


def _expert_merge(y_buf, pos_t, gate_t, x1, g, b, alpha, tables, views):
    n, d = x1.shape
    tm = TOKEN_TILE
    nt = n // tm
    tok = pl.BlockSpec((tm, d), lambda t, *_: (t, 0))
    pair = pl.BlockSpec((tm, 2), lambda t, *_: (t, 0))
    vec = pl.BlockSpec((1, d), lambda t, *_: (0, 0))
    grid_spec = pltpu.PrefetchScalarGridSpec(
        num_scalar_prefetch=4,
        grid=(nt,),
        in_specs=[pl.BlockSpec(memory_space=pl.ANY), tok, pair, pair, vec, vec],
        out_specs=[tok, tok] + [pl.BlockSpec((tm // r, r * d), lambda t, *_: (t, 0)) for r in views],
        scratch_shapes=[pltpu.VMEM((2, SORTED_ROWS, d // 2), jnp.uint32), pltpu.VMEM((d // 128, tm, 128), F32),
                        pltpu.SemaphoreType.DMA((2,))],
    )
    return pl.pallas_call(
        functools.partial(_merge_kernel, nt=nt, alpha=alpha, views=views),
        grid_spec=grid_spec,
        out_shape=[jax.ShapeDtypeStruct((n, d), F32), jax.ShapeDtypeStruct((n, d), BF16)]
        + [jax.ShapeDtypeStruct((n // r, r * d), BF16) for r in views],
        compiler_params=_params("arbitrary"),
        name="expert_merge",
    )(tables["dst"], tables["off"], tables["len"], tables["tot"], y_buf, x1, pos_t, gate_t, g, b)


def _routing_tables(counts, nb):
    blk = ROW_BLOCK
    nt = counts.shape[0]
    seg = (counts + SEG_ALIGN - 1) // SEG_ALIGN * SEG_ALIGN
    off = jnp.cumsum(seg, axis=1) - seg
    used = jnp.sum(seg, axis=0)
    region = (used + blk - 1) // blk * blk
    region_end = jnp.cumsum(region)
    region_start = region_end - region
    dst = region_start[None, :] + jnp.cumsum(seg, axis=0) - seg
    n_act = region_end[-1] // blk
    block_row0 = jnp.minimum(jnp.arange(nb, dtype=jnp.int32), n_act - 1) * blk
    block_e = jnp.sum(region_end[None, :] <= block_row0[:, None], axis=1)
    i32 = lambda v: v.astype(jnp.int32)
    return dict(dst=i32(dst.reshape(-1)), off=i32(off.reshape(-1)), len=i32(seg.reshape(-1)),
                tot=i32(jnp.sum(seg, axis=1)), fill_row=i32(region_start + used), fill_len=i32(region - used),
                n_act=i32(n_act.reshape(1)), block_e=i32(block_e))


def kernel(x, w_in, w_attn_out, w_conv_out, w_o, conv_w, conv_b, conv_ln_g, conv_ln_b, ln_mix_g, ln_mix_b,
           expert_w_gate_up, expert_w_down, ln_ffn_g, ln_ffn_b, router_w, rel_bias):
    batch, seq, d = x.shape
    depth = w_in.shape[0]
    n = batch * seq
    aw, gw = ATTN_WIDTH, GROUP_WIDTH
    assert d % 256 == 0 and n % 1024 == 0
    for window, r in DILATION_GROUPS:
        assert window // r == BAND and seq % (r * BAND) == 0
    alpha = (2 * depth) ** 0.25
    scale = HEAD_DIM ** -0.5

    biases = [_band_bias(rel_bias, gi, r) for gi, (_, r) in enumerate(DILATION_GROUPS)]
    rwt = router_w.T.astype(BF16)
    nt = n // TOKEN_TILE
    nb = -(-(2 * n + N_EXPERTS * (nt * (SEG_ALIGN - 1) + ROW_BLOCK - 1)) // ROW_BLOCK)
    dilations = tuple(r for _, r in DILATION_GROUPS if r > 1)

    xf = x.reshape(n, d)
    xb, *xv = _input_views(xf, dilations)
    xviews = dict(zip(dilations, xv))
    for l in range(depth):
        wl = w_in[l]

        def qkv_cols(gi):
            return jnp.concatenate([wl[:, gi * gw:(gi + 1) * gw] * scale,
                                    wl[:, aw + gi * gw:aw + (gi + 1) * gw],
                                    wl[:, 2 * aw + gi * gw:2 * aw + (gi + 1) * gw]], axis=1)

        w0 = jnp.concatenate([wl[:, 3 * aw + d:], wl[:, 3 * aw:3 * aw + d], qkv_cols(0)], axis=1).astype(BF16)
        proj0 = _project(xb, w0, 512)
        outs, lses = [], []
        for gi, (_, r) in enumerate(DILATION_GROUPS):
            if r == 1:
                qkv, col0 = proj0.reshape(batch, seq, proj0.shape[1]), (3 * d) // gw
            else:
                qkv = _project_dilated(xviews[r], qkv_cols(gi).astype(BF16), batch, seq, r)
                qkv, col0 = qkv.reshape(batch * r, seq // r, 3 * gw), 0
            o, lse = _attention(qkv, biases[gi], batch, seq, r, col0)
            outs.append(o)
            lses.append(lse)

        lw = dict(wao=w_attn_out[l].astype(BF16), cw=conv_w[l], cb=conv_b[l][None], clg=conv_ln_g[l][None],
                  clb=conv_ln_b[l][None], wco=w_conv_out[l].astype(BF16), wo=w_o[l].astype(BF16),
                  lng=ln_mix_g[l][None], lnb=ln_mix_b[l][None], rwt=rwt, alpha=alpha)
        x1, x1b, gate, pos, cnt = _mixer_tail(outs, lses, proj0, xf, lw, batch, seq)

        tables = _routing_tables(cnt[:, :, 0].astype(jnp.int32), nb)
        buf = _dispatch(x1b, pos, tables, nb)
        y_buf = _expert_ffn(buf, expert_w_gate_up, expert_w_down, l, tables)
        views = dilations if l + 1 < depth else ()
        xf, xb, *xv = _expert_merge(y_buf, pos.T, gate.T, x1, ln_ffn_g[l][None], ln_ffn_b[l][None], alpha,
                                    tables, views)
        xviews = dict(zip(views, xv))
    return xf.reshape(batch, seq, d)
```

```python
import functools
import math

import jax
import jax.numpy as jnp
from jax import lax
from jax.experimental import pallas as pl
from jax.experimental.pallas import tpu as pltpu

F32 = jnp.float32
BF16 = jnp.bfloat16

HEAD_DIM = 64
HEADS_PER_GROUP = 4
GROUP_WIDTH = HEADS_PER_GROUP * HEAD_DIM
DILATION_GROUPS = ((128, 1), (512, 4), (2048, 16))
N_GROUPS = len(DILATION_GROUPS)
ATTN_WIDTH = N_GROUPS * GROUP_WIDTH
BAND = 128
CONV_KERNEL = 31
CONV_HALO = 32
CONV_TAIL = 16
N_BUCKETS = 32
MAX_DISTANCE = 2048
N_EXPERTS = 16
EXPERTS_PER_GROUP = 4
LN_EPS = 1e-5
MASKED = -1e30

PROJ_ROWS = 2048
ROW_BLOCK = 256
TOKEN_TILE = 256
SEG_ALIGN = 8
SORTED_ROWS = -(-(2 * TOKEN_TILE + N_EXPERTS * (SEG_ALIGN - 1)) // 128) * 128
VMEM_LIMIT = 56 * 1024 * 1024


def _sigmoid(v):
    return 1.0 / (1.0 + jnp.exp(-v))


def _layer_norm(z, g, b):
    mu = jnp.mean(z, axis=-1, keepdims=True)
    zc = z - mu
    var = jnp.mean(zc * zc, axis=-1, keepdims=True)
    return zc * lax.rsqrt(var + LN_EPS) * g + b


def _params(*sem):
    return pltpu.CompilerParams(dimension_semantics=sem, vmem_limit_bytes=VMEM_LIMIT)


def _mm_kernel(x_ref, w_ref, o_ref):
    o_ref[...] = jnp.dot(x_ref[...], w_ref[...], preferred_element_type=F32).astype(o_ref.dtype)


def _project(xb, w, tm):
    n, d = xb.shape
    c = w.shape[1]
    return pl.pallas_call(
        _mm_kernel,
        grid=(n // tm,),
        in_specs=[pl.BlockSpec((tm, d), lambda i: (i, 0)),
                  pl.BlockSpec((d, c), lambda i: (0, 0))],
        out_specs=pl.BlockSpec((tm, c), lambda i: (i, 0)),
        out_shape=jax.ShapeDtypeStruct((n, c), BF16),
        compiler_params=_params("arbitrary"),
        name="proj_natural",
    )(xb, w)


def _project_dilated(xv, w, batch, seq, r):
    d, c = w.shape
    n = xv.shape[0] * r
    sub = seq // r
    tm = min(PROJ_ROWS, sub)
    nl = sub // tm
    cps = max(1, min(r, PROJ_ROWS // sub))
    assert r % cps == 0

    def mm_classes(x_ref, w_ref, o_ref):
        for k in range(cps):
            o_ref[k * tm:(k + 1) * tm, :] = jnp.dot(x_ref[:, k * d:(k + 1) * d], w_ref[...],
                                                    preferred_element_type=F32).astype(o_ref.dtype)

    return pl.pallas_call(
        mm_classes,
        grid=(batch, r // cps, nl),
        in_specs=[pl.BlockSpec((tm, cps * d), lambda b, cg, j: (b * nl + j, cg)),
                  pl.BlockSpec((d, c), lambda b, cg, j: (0, 0))],
        out_specs=pl.BlockSpec((cps * tm, c), lambda b, cg, j: ((b * (r // cps) + cg) * nl + j, 0)),
        out_shape=jax.ShapeDtypeStruct((n, c), BF16),
        compiler_params=_params("arbitrary", "arbitrary", "arbitrary"),
        name=f"proj_dilated_{r}",
    )(xv, w)


def _attn_kernel(q_ref, kp_ref, vp_ref, kc_ref, vc_ref, bias_ref, o_ref, lse_ref, k_scr, v_scr, *, nqb):
    i = pl.program_id(1)
    k_scr[0:BAND, :] = kp_ref[0]
    k_scr[BAND:, :] = kc_ref[0]
    v_scr[0:BAND, :] = vp_ref[0]
    v_scr[BAND:, :] = vc_ref[0]
    col = lax.broadcasted_iota(jnp.int32, (BAND, 2 * BAND), 1)
    first_head = lax.broadcasted_iota(jnp.int32, (BAND, 2 * HEAD_DIM), 1) < HEAD_DIM

    def body(j, carry):
        r0 = pl.multiple_of(j * BAND, BAND)
        q = q_ref[0, pl.ds(r0, BAND), :]
        kk = k_scr[pl.ds(r0, 2 * BAND), :]
        vv = v_scr[pl.ds(r0, 2 * BAND), :]
        no_prev = jnp.logical_and(jnp.logical_and(i == 0, j == 0), col < BAND)
        outs, lses = [], []
        for pair in range(HEADS_PER_GROUP // 2):
            sl = slice(pair * 2 * HEAD_DIM, (pair + 1) * 2 * HEAD_DIM)
            q2, k2, v2 = q[:, sl], kk[:, sl], vv[:, sl]
            o_pair, lse_pair = None, None
            for hh in range(2):
                mine = first_head if hh == 0 else jnp.logical_not(first_head)
                qm = jnp.where(mine, q2, jnp.zeros_like(q2))
                s = lax.dot_general(qm, k2, (((1,), (1,)), ((), ())), preferred_element_type=F32)
                s = s + bias_ref[2 * pair + hh]
                s = jnp.where(no_prev, MASKED, s)
                m = jnp.max(s, axis=-1, keepdims=True)
                p = jnp.exp(s - m)
                den = jnp.sum(p, axis=-1, keepdims=True)
                o = jnp.dot(p.astype(BF16), v2, preferred_element_type=F32) * (1.0 / den)
                lse = jnp.broadcast_to(m + jnp.log(den), (BAND, 2 * HEAD_DIM))
                o_pair = o if hh == 0 else jnp.where(first_head, o_pair, o)
                lse_pair = lse if hh == 0 else jnp.where(first_head, lse_pair, lse)
            outs.append(o_pair)
            lses.append(lse_pair)
        o_ref[0, pl.ds(r0, BAND), :] = jnp.concatenate(outs, axis=1).astype(o_ref.dtype)
        lse_ref[0, pl.ds(r0, BAND), :] = jnp.concatenate(lses, axis=1)
        return carry

    lax.fori_loop(0, nqb, body, 0, unroll=math.gcd(nqb, 4))


def _attention(qkv, bias, batch, seq, r, col0):
    sub = seq // r
    nqb = min(8, sub // BAND)
    rows = nqb * BAND
    nt = sub // rows
    gw = GROUP_WIDTH

    def cur(c):
        return pl.BlockSpec((1, rows, gw), lambda bc, i: (bc, i, c))

    def prev(c):
        return pl.BlockSpec((1, BAND, gw), lambda bc, i: (bc, jnp.maximum(i * nqb - 1, 0), c))

    out_spec = pl.BlockSpec((1, rows, gw), lambda bc, i: (bc // r, i, bc % r))
    o, lse = pl.pallas_call(
        functools.partial(_attn_kernel, nqb=nqb),
        grid=(batch * r, nt),
        in_specs=[cur(col0), prev(col0 + 1), prev(col0 + 2), cur(col0 + 1), cur(col0 + 2),
                  pl.BlockSpec((HEADS_PER_GROUP, BAND, 2 * BAND), lambda bc, i: (0, 0, 0))],
        out_specs=[out_spec, out_spec],
        out_shape=[jax.ShapeDtypeStruct((batch, sub, r * gw), BF16),
                   jax.ShapeDtypeStruct((batch, sub, r * gw), F32)],
        scratch_shapes=[pltpu.VMEM((rows + BAND, gw), BF16), pltpu.VMEM((rows + BAND, gw), BF16)],
        compiler_params=_params("arbitrary", "arbitrary"),
        name=f"attn_dilation_{r}",
    )(qkv, qkv, qkv, qkv, qkv, bias)
    return o.reshape(batch * sub, r * gw), lse.reshape(batch * sub, r * gw)


def _t5_bucket(dist):
    max_exact = N_BUCKETS // 2
    n = jnp.maximum(dist, 0)
    nf = jnp.maximum(n, 1).astype(F32)
    large = max_exact + (jnp.log(nf / max_exact) / math.log(MAX_DISTANCE / max_exact)
                         * (N_BUCKETS - max_exact)).astype(jnp.int32)
    large = jnp.minimum(large, N_BUCKETS - 1)
    return jnp.where(n < max_exact, n, large)


def _band_bias(rel_bias, gi, r):
    qi = jnp.arange(BAND)[:, None]
    kj = jnp.arange(2 * BAND)[None, :]
    dist = qi + BAND - kj
    bucket = _t5_bucket(dist * r)
    valid = (dist >= 0) & (dist <= BAND)
    b = jnp.full((HEADS_PER_GROUP, BAND, 2 * BAND), MASKED, F32)
    for k in range(N_BUCKETS):
        row = rel_bias[k, gi * HEADS_PER_GROUP:(gi + 1) * HEADS_PER_GROUP].astype(F32)
        b = jnp.where(((bucket == k) & valid)[None], row[:, None, None], b)
    return b


def _post_kernel(o0_ref, o1_ref, o2_ref, l0_ref, l1_ref, l2_ref, u_ref, uh_ref, g_ref, x_ref,
                 wao_ref, cw_ref, cb_ref, clg_ref, clb_ref, wco_ref, wo_ref, lng_ref, lnb_ref, rwt_ref,
                 x1_ref, x1b_ref, gate_ref, pos_ref, cnt_ref,
                 hs_scr, conv_scr, ys_scr, nat_scr, *, tm, seq, alpha):
    i = pl.program_id(0)
    d = x_ref.shape[1]
    c = d // 2
    gw = GROUP_WIDTH

    def natural(ref, slot, r):
        if r == 1:
            return ref[...].astype(F32)
        nl = gw // 128
        for cls in range(r):
            for j in range(nl):
                lanes = slice(cls * gw + j * 128, cls * gw + (j + 1) * 128)
                nat_scr[slot * nl + j, pl.ds(cls, tm // r, stride=r), :] = ref[:, lanes].astype(F32)
        return jnp.concatenate([nat_scr[slot * nl + j] for j in range(nl)], axis=1)

    rs = [r for _, r in DILATION_GROUPS]
    o0, o1, o2 = (natural(ref, s, r) for s, (ref, r) in enumerate(zip((o0_ref, o1_ref, o2_ref), rs)))
    l0, l1, l2 = (natural(ref, 3 + s, r) for s, (ref, r) in enumerate(zip((l0_ref, l1_ref, l2_ref), rs)))
    mx = jnp.maximum(jnp.maximum(l0, l1), l2)
    e0, e1, e2 = jnp.exp(l0 - mx), jnp.exp(l1 - mx), jnp.exp(l2 - mx)
    attn = (e0 * o0 + e1 * o1 + e2 * o2) * (1.0 / (e0 + e1 + e2))
    attn_branch = jnp.dot(attn.astype(BF16), wao_ref[...], preferred_element_type=F32)

    u = u_ref[...].astype(F32)
    uh = uh_ref[...].astype(F32)
    hh = uh[:, :c] * _sigmoid(uh[:, c:])
    seq_start = (i * tm) % seq == 0
    hs_scr[0:CONV_HALO, :] = jnp.where(seq_start, 0.0, hh)
    hs_scr[CONV_HALO:CONV_HALO + tm, :] = u[:, :c] * _sigmoid(u[:, c:])
    hs_scr[CONV_HALO + tm:, :] = jnp.zeros((CONV_TAIL, c), F32)
    off = CONV_HALO - (CONV_KERNEL - 1)
    rows_y = tm + 8
    for cc in range(c // 128):
        ls = slice(cc * 128, (cc + 1) * 128)
        acc = None
        for s in range(8):
            y = None
            for a in range((off + CONV_KERNEL + 7) // 8):
                k = 8 * a + s - off
                if 0 <= k < CONV_KERNEL:
                    term = hs_scr[8 * a:8 * a + rows_y, ls] * cw_ref[k:k + 1, ls]
                    y = term if y is None else y + term
            if s == 0:
                acc = y[0:tm]
            else:
                ys_scr[s] = y
                acc = acc + ys_scr[s, s:s + tm, :]
        conv_scr[:, ls] = acc
    hc = _layer_norm(conv_scr[...] + cb_ref[...], clg_ref[...], clb_ref[...])
    hc = hc * _sigmoid(hc)
    conv_branch = jnp.dot(hc.astype(BF16), wco_ref[...], preferred_element_type=F32)

    gates = _sigmoid(g_ref[...].astype(F32))
    merged = gates[:, :d] * attn_branch + gates[:, d:] * conv_branch
    hmix = jnp.dot(merged.astype(BF16), wo_ref[...], preferred_element_type=F32)
    x1 = _layer_norm(alpha * x_ref[...] + hmix, lng_ref[...], lnb_ref[...])
    x1_ref[...] = x1
    x1b = x1.astype(BF16)
    x1b_ref[...] = x1b

    logits = lax.dot_general(rwt_ref[...], x1b, (((1,), (1,)), ((), ())), preferred_element_type=F32)
    ex = jnp.exp(logits - jnp.max(logits, axis=0, keepdims=True))
    rows = [ex[e:e + 1, :] for e in range(N_EXPERTS)]
    best, gsel = None, None
    for g in range(N_EXPERTS // EXPERTS_PER_GROUP):
        v = rows[g * EXPERTS_PER_GROUP:(g + 1) * EXPERTS_PER_GROUP]
        score = None
        for a in range(EXPERTS_PER_GROUP):
            for b in range(a + 1, EXPERTS_PER_GROUP):
                ps = v[a] + v[b]
                score = ps if score is None else jnp.maximum(score, ps)
        if best is None:
            best, gsel = score, jnp.zeros_like(score, dtype=jnp.int32)
        else:
            upd = score > best
            gsel = jnp.where(upd, g, gsel)
            best = jnp.where(upd, score, best)
    vals = []
    for j in range(EXPERTS_PER_GROUP):
        vj = rows[j]
        for g in range(1, N_EXPERTS // EXPERTS_PER_GROUP):
            vj = jnp.where(gsel == g, rows[g * EXPERTS_PER_GROUP + j], vj)
        vals.append(vj)
    v1, i1 = vals[0], jnp.zeros_like(gsel)
    for j in range(1, EXPERTS_PER_GROUP):
        upd = vals[j] > v1
        i1 = jnp.where(upd, j, i1)
        v1 = jnp.where(upd, vals[j], v1)
    v2, i2 = jnp.full_like(v1, -1.0), jnp.zeros_like(gsel)
    for j in range(EXPERTS_PER_GROUP):
        upd = jnp.logical_and(i1 != j, vals[j] > v2)
        i2 = jnp.where(upd, j, i2)
        v2 = jnp.where(upd, vals[j], v2)
    ea = gsel * EXPERTS_PER_GROUP + i1
    eb = gsel * EXPERTS_PER_GROUP + i2
    inv = 1.0 / (v1 + v2)
    gate_ref[...] = jnp.concatenate([v1 * inv, v2 * inv], axis=0)

    eid = lax.broadcasted_iota(jnp.int32, (N_EXPERTS, tm), 0)
    hit_a = eid == ea
    hit_b = eid == eb
    onehot = jnp.where(jnp.logical_or(hit_a, hit_b), 1.0, 0.0)
    before = (lax.broadcasted_iota(jnp.int32, (tm, tm), 0)
              < lax.broadcasted_iota(jnp.int32, (tm, tm), 1))
    upper = jnp.where(before, 1.0, 0.0).astype(BF16)
    rank = jnp.dot(onehot.astype(BF16), upper, preferred_element_type=F32)
    count = jnp.sum(onehot, axis=1, keepdims=True)
    seg_len = jnp.floor((count + (SEG_ALIGN - 1)) * (1.0 / SEG_ALIGN)) * SEG_ALIGN
    lower = (lax.broadcasted_iota(jnp.int32, (N_EXPERTS, N_EXPERTS), 1)
             < lax.broadcasted_iota(jnp.int32, (N_EXPERTS, N_EXPERTS), 0))
    seg_off = jnp.dot(jnp.where(lower, 1.0, 0.0).astype(BF16),
                      jnp.broadcast_to(seg_len, (N_EXPERTS, 128)).astype(BF16),
                      preferred_element_type=F32)[:, 0:1]
    row = rank + seg_off
    pa = jnp.sum(jnp.where(hit_a, row, 0.0), axis=0, keepdims=True)
    pb = jnp.sum(jnp.where(hit_b, row, 0.0), axis=0, keepdims=True)
    pos_ref[...] = jnp.concatenate([pa, pb], axis=0).astype(jnp.int32)
    cnt_ref[0] = jnp.broadcast_to(count, (N_EXPERTS, 128))


def _mixer_tail(o, lse, proj0, x, lw, batch, seq):
    n, d = x.shape
    tm = TOKEN_TILE
    nt = n // tm
    gw = GROUP_WIDTH
    hb = tm // CONV_HALO

    def tok(width):
        return pl.BlockSpec((tm, width), lambda i: (i, 0))

    def const(shape):
        return pl.BlockSpec(shape, lambda i: (0,) * len(shape))

    def grouped(r):
        return pl.BlockSpec((tm // r, r * gw), lambda i: (i, 0))

    rs = [r for _, r in DILATION_GROUPS]
    in_specs = (
        [grouped(r) for r in rs] * 2
        + [pl.BlockSpec((tm, d), lambda i: (i, 2)),
           pl.BlockSpec((CONV_HALO, d), lambda i: (jnp.maximum(i * hb - 1, 0), 2)),
           pl.BlockSpec((tm, 2 * d), lambda i: (i, 0)),
           tok(d)]
        + [const(lw[k].shape) for k in ("wao", "cw", "cb", "clg", "clb", "wco", "wo", "lng", "lnb", "rwt")]
    )
    lane_row = pl.BlockSpec((2, tm), lambda i: (0, i))
    out_specs = [tok(d), tok(d), lane_row, lane_row,
                 pl.BlockSpec((1, N_EXPERTS, 128), lambda i: (i, 0, 0))]
    out_shape = [jax.ShapeDtypeStruct((n, d), F32), jax.ShapeDtypeStruct((n, d), BF16),
                 jax.ShapeDtypeStruct((2, n), F32), jax.ShapeDtypeStruct((2, n), jnp.int32),
                 jax.ShapeDtypeStruct((nt, N_EXPERTS, 128), F32)]
    alpha = lw["alpha"]
    return pl.pallas_call(
        functools.partial(_post_kernel, tm=tm, seq=seq, alpha=alpha),
        grid=(nt,),
        in_specs=in_specs,
        out_specs=out_specs,
        out_shape=out_shape,
        scratch_shapes=[pltpu.VMEM((CONV_HALO + tm + CONV_TAIL, d // 2), F32), pltpu.VMEM((tm, d // 2), F32),
                        pltpu.VMEM((8, tm + 8, 128), F32),
                        pltpu.VMEM((2 * N_GROUPS * (gw // 128), tm, 128), F32)],
        compiler_params=_params("arbitrary"),
        name="mixer_tail",
    )(o[0], o[1], o[2], lse[0], lse[1], lse[2], proj0, proj0, proj0, x,
      *[lw[k] for k in ("wao", "cw", "cb", "clg", "clb", "wco", "wo", "lng", "lnb", "rwt")])


def _pack_halves(v):
    h = v.shape[1] // 2
    bits = lax.bitcast_convert_type(v, jnp.uint32)
    return (bits[:, :h] >> 16) | (bits[:, h:] & jnp.uint32(0xFFFF0000))


def _unpack_halves(w):
    lo = lax.bitcast_convert_type(w << 16, F32).astype(BF16)
    hi = lax.bitcast_convert_type(w & jnp.uint32(0xFFFF0000), F32).astype(BF16)
    return lo, hi


def _dispatch_kernel(dst_ref, off_ref, len_ref, tot_ref, fill_row_ref, fill_len_ref, nact_ref,
                     x_ref, pos_ref, buf_hbm, xs, zeros, sem, zsem, *, nt, nb, blk):
    t = pl.program_id(0)
    cur = t % 2
    srows = xs.shape[1]

    def tile_wait(tile, buf):
        n = pl.multiple_of(tot_ref[tile], SEG_ALIGN)
        pltpu.make_async_copy(xs.at[buf, pl.ds(0, n)], buf_hbm.at[pl.ds(0, n)], sem.at[buf]).wait()

    @pl.when(t >= 2)
    def _():
        tile_wait(t - 2, cur)

    pos = pos_ref[...]
    row = lax.broadcasted_iota(jnp.int32, (srows, pos.shape[1]), 0)
    hit = jnp.logical_or(row == pos[0:1, :], row == pos[1:2, :])
    sel = jnp.where(hit, 1.0, 0.0).astype(BF16)
    xs[cur] = _pack_halves(jnp.dot(sel, x_ref[...], preferred_element_type=F32))

    for e in range(N_EXPERTS):
        k = t * N_EXPERTS + e
        n = pl.multiple_of(len_ref[k], SEG_ALIGN)
        src = pl.multiple_of(off_ref[k], SEG_ALIGN)
        dst = pl.multiple_of(dst_ref[k], SEG_ALIGN)

        @pl.when(n > 0)
        def _():
            pltpu.make_async_copy(xs.at[cur, pl.ds(src, n)], buf_hbm.at[pl.ds(dst, n)], sem.at[cur]).start()

    @pl.when(t == nt - 1)
    def _():
        if nt >= 2:
            tile_wait(t - 1, 1 - cur)
        tile_wait(t, cur)
        zeros[...] = jnp.zeros_like(zeros)

        def region_fill(e):
            n = pl.multiple_of(fill_len_ref[e], SEG_ALIGN)
            dst = pl.multiple_of(fill_row_ref[e], SEG_ALIGN)
            return n, pltpu.make_async_copy(zeros.at[pl.ds(0, n)], buf_hbm.at[pl.ds(dst, n)], zsem)

        def block_fill(b):
            return pltpu.make_async_copy(zeros, buf_hbm.at[pl.ds(pl.multiple_of(b * blk, blk), blk)], zsem)

        def each_block(fn):
            def body(b, carry):
                fn(block_fill(b))
                return carry
            lax.fori_loop(nact_ref[0], nb, body, 0)

        for e in range(N_EXPERTS):
            n, cp = region_fill(e)
            pl.when(n > 0)(cp.start)
        each_block(lambda cp: cp.start())
        for e in range(N_EXPERTS):
            n, cp = region_fill(e)
            pl.when(n > 0)(cp.wait)
        each_block(lambda cp: cp.wait())


def _dispatch(x1b, pos, tables, nb):
    n, d = x1b.shape
    tm = TOKEN_TILE
    nt = n // tm
    blk = ROW_BLOCK
    grid_spec = pltpu.PrefetchScalarGridSpec(
        num_scalar_prefetch=7,
        grid=(nt,),
        in_specs=[pl.BlockSpec((tm, d), lambda t, *_: (t, 0)),
                  pl.BlockSpec((2, tm), lambda t, *_: (0, t))],
        out_specs=pl.BlockSpec(memory_space=pl.ANY),
        scratch_shapes=[pltpu.VMEM((2, SORTED_ROWS, d // 2), jnp.uint32), pltpu.VMEM((blk, d // 2), jnp.uint32),
                        pltpu.SemaphoreType.DMA((2,)), pltpu.SemaphoreType.DMA],
    )
    return pl.pallas_call(
        functools.partial(_dispatch_kernel, nt=nt, nb=nb, blk=blk),
        grid_spec=grid_spec,
        out_shape=jax.ShapeDtypeStruct((nb * blk, d // 2), jnp.uint32),
        compiler_params=_params("arbitrary"),
        name="expert_dispatch",
    )(tables["dst"], tables["off"], tables["len"], tables["tot"], tables["fill_row"], tables["fill_len"],
      tables["n_act"], x1b, pos)


def _ffn_kernel(first_ref, count_ref, nact_ref, xs_hbm, wgu_ref, wdn_ref, y_hbm,
                wgu_b, wdn_b, xbuf, obuf, isem, osem, zsem, *, nb, blk):
    e = pl.program_id(0)
    f = wdn_ref.shape[2]
    half = xbuf.shape[2]
    b0 = first_ref[e]
    n_blocks = count_ref[e]

    def rows(j):
        return pl.ds(pl.multiple_of((b0 + j) * blk, blk), blk)

    def load(j, buf):
        return pltpu.make_async_copy(xs_hbm.at[rows(j)], xbuf.at[buf], isem.at[buf])

    def store(j, buf):
        return pltpu.make_async_copy(obuf.at[buf], y_hbm.at[rows(j)], osem.at[buf])

    @pl.when(n_blocks > 0)
    def _():
        load(0, 0).start()
        wgu_b[...] = wgu_ref[0, 0].astype(BF16)
        wdn_b[...] = wdn_ref[0, 0].astype(BF16)

    def block(j, carry):
        buf = j % 2

        @pl.when(j + 1 < n_blocks)
        def _():
            load(j + 1, 1 - buf).start()

        load(j, buf).wait()
        x_lo, x_hi = _unpack_halves(xbuf[buf])
        h = (jnp.dot(x_lo, wgu_b[0:half, :], preferred_element_type=F32)
             + jnp.dot(x_hi, wgu_b[half:, :], preferred_element_type=F32))
        a = h[:, :f]
        act = (a * _sigmoid(a) * h[:, f:]).astype(BF16)
        y = jnp.dot(act, wdn_b[...], preferred_element_type=F32)

        @pl.when(j >= 2)
        def _():
            store(j - 2, buf).wait()

        obuf[buf] = _pack_halves(y.astype(BF16).astype(F32))
        store(j, buf).start()
        return carry

    lax.fori_loop(0, n_blocks, block, 0)

    @pl.when(n_blocks >= 2)
    def _():
        store(n_blocks - 2, n_blocks % 2).wait()

    @pl.when(n_blocks >= 1)
    def _():
        store(n_blocks - 1, (n_blocks - 1) % 2).wait()

    @pl.when(e == pl.num_programs(0) - 1)
    def _():
        xbuf[0] = jnp.zeros((blk, half), jnp.uint32)

        def fill(b):
            return pltpu.make_async_copy(xbuf.at[0], y_hbm.at[pl.ds(pl.multiple_of(b * blk, blk), blk)], zsem)

        def each(fn):
            def body(b, carry):
                fn(fill(b))
                return carry
            lax.fori_loop(nact_ref[0], nb, body, 0)

        each(lambda cp: cp.start())
        each(lambda cp: cp.wait())


def _expert_ffn(buf, wgu, wdn, layer, tables):
    n_rows, half = buf.shape
    d = 2 * half
    blk = ROW_BLOCK
    nb = n_rows // blk
    f2 = wgu.shape[3]
    f = wdn.shape[2]
    grid_spec = pltpu.PrefetchScalarGridSpec(
        num_scalar_prefetch=3,
        grid=(N_EXPERTS,),
        in_specs=[pl.BlockSpec(memory_space=pl.ANY),
                  pl.BlockSpec((1, 1, d, f2), lambda e, *_: (layer, e, 0, 0)),
                  pl.BlockSpec((1, 1, f, d), lambda e, *_: (layer, e, 0, 0))],
        out_specs=pl.BlockSpec(memory_space=pl.ANY),
        scratch_shapes=[pltpu.VMEM((d, f2), BF16), pltpu.VMEM((f, d), BF16),
                        pltpu.VMEM((2, blk, half), jnp.uint32), pltpu.VMEM((2, blk, half), jnp.uint32),
                        pltpu.SemaphoreType.DMA((2,)), pltpu.SemaphoreType.DMA((2,)), pltpu.SemaphoreType.DMA],
    )
    return pl.pallas_call(
        functools.partial(_ffn_kernel, nb=nb, blk=blk),
        grid_spec=grid_spec,
        out_shape=jax.ShapeDtypeStruct((n_rows, half), jnp.uint32),
        compiler_params=_params("arbitrary"),
        name="expert_ffn",
    )(tables["first_block"], tables["n_blocks"], tables["n_act"], buf, wgu, wdn)


def _store_class_views(xt, scr, view_refs, views):
    tm, d = xt.shape
    if views:
        for j in range(d // 128):
            scr[j] = xt[:, j * 128:(j + 1) * 128]
    for ref, r in zip(view_refs, views):
        for cls in range(r):
            for j in range(d // 128):
                lanes = slice(cls * d + j * 128, cls * d + (j + 1) * 128)
                ref[:, lanes] = scr[j, pl.ds(cls, tm // r, stride=r), :].astype(BF16)


def _views_kernel(x_ref, xb_ref, *rest, views):
    view_refs, scr = rest[:len(views)], rest[len(views)]
    xt = x_ref[...]
    xb_ref[...] = xt.astype(BF16)
    _store_class_views(xt, scr, view_refs, views)


def _input_views(xf, views):
    n, d = xf.shape
    tm = TOKEN_TILE
    tok = pl.BlockSpec((tm, d), lambda t: (t, 0))
    return pl.pallas_call(
        functools.partial(_views_kernel, views=views),
        grid=(n // tm,),
        in_specs=[tok],
        out_specs=[tok] + [pl.BlockSpec((tm // r, r * d), lambda t: (t, 0)) for r in views],
        out_shape=[jax.ShapeDtypeStruct((n, d), BF16)]
        + [jax.ShapeDtypeStruct((n // r, r * d), BF16) for r in views],
        scratch_shapes=[pltpu.VMEM((d // 128, tm, 128), F32)],
        compiler_params=_params("arbitrary"),
        name="input_views",
    )(xf)


def _merge_kernel(dst_ref, off_ref, len_ref, tot_ref, y_hbm, x1_ref, pos_ref, gt_ref, g_ref, b_ref,
                  x2_ref, x2b_ref, *rest, nt, alpha, views):
    view_refs, (ys, x2_scr, sem) = rest[:len(views)], rest[len(views):]
    t = pl.program_id(0)
    cur = t % 2
    tm, d = x1_ref.shape
    srows = ys.shape[1]

    def fetch(tile, buf):
        for e in range(N_EXPERTS):
            k = tile * N_EXPERTS + e
            n = pl.multiple_of(len_ref[k], SEG_ALIGN)
            dst = pl.multiple_of(off_ref[k], SEG_ALIGN)
            src = pl.multiple_of(dst_ref[k], SEG_ALIGN)

            @pl.when(n > 0)
            def _():
                pltpu.make_async_copy(y_hbm.at[pl.ds(src, n)], ys.at[buf, pl.ds(dst, n)], sem.at[buf]).start()

    @pl.when(t == 0)
    def _():
        ys[...] = jnp.zeros_like(ys)
        fetch(t, cur)

    @pl.when(t + 1 < nt)
    def _():
        fetch(t + 1, 1 - cur)

    n_all = pl.multiple_of(tot_ref[t], SEG_ALIGN)
    pltpu.make_async_copy(y_hbm.at[pl.ds(0, n_all)], ys.at[cur, pl.ds(0, n_all)], sem.at[cur]).wait()

    y_lo, y_hi = _unpack_halves(ys[cur])
    pos = pos_ref[...]
    gt = gt_ref[...]
    lane = lax.broadcasted_iota(jnp.int32, (tm, srows), 1)
    sel = (jnp.where(lane == pos[:, 0:1], gt[:, 0:1], 0.0)
           + jnp.where(lane == pos[:, 1:2], gt[:, 1:2], 0.0)).astype(BF16)
    m = jnp.concatenate([jnp.dot(sel, y_lo, preferred_element_type=F32),
                         jnp.dot(sel, y_hi, preferred_element_type=F32)], axis=1)
    x2 = _layer_norm(alpha * x1_ref[...] + m, g_ref[...], b_ref[...])
    x2_ref[...] = x2
    x2b_ref[...] = x2.astype(BF16)
    _store_class_views(x2, x2_scr, view_refs, views)


def _expert_merge(y_buf, pos_t, gate_t, x1, g, b, alpha, tables, views):
    n, d = x1.shape
    tm = TOKEN_TILE
    nt = n // tm
    tok = pl.BlockSpec((tm, d), lambda t, *_: (t, 0))
    pair = pl.BlockSpec((tm, 2), lambda t, *_: (t, 0))
    vec = pl.BlockSpec((1, d), lambda t, *_: (0, 0))
    grid_spec = pltpu.PrefetchScalarGridSpec(
        num_scalar_prefetch=4,
        grid=(nt,),
        in_specs=[pl.BlockSpec(memory_space=pl.ANY), tok, pair, pair, vec, vec],
        out_specs=[tok, tok] + [pl.BlockSpec((tm // r, r * d), lambda t, *_: (t, 0)) for r in views],
        scratch_shapes=[pltpu.VMEM((2, SORTED_ROWS, d // 2), jnp.uint32), pltpu.VMEM((d // 128, tm, 128), F32),
                        pltpu.SemaphoreType.DMA((2,))],
    )
    return pl.pallas_call(
        functools.partial(_merge_kernel, nt=nt, alpha=alpha, views=views),
        grid_spec=grid_spec,
        out_shape=[jax.ShapeDtypeStruct((n, d), F32), jax.ShapeDtypeStruct((n, d), BF16)]
        + [jax.ShapeDtypeStruct((n // r, r * d), BF16) for r in views],
        compiler_params=_params("arbitrary"),
        name="expert_merge",
    )(tables["dst"], tables["off"], tables["len"], tables["tot"], y_buf, x1, pos_t, gate_t, g, b)


def _routing_tables(counts):
    blk = ROW_BLOCK
    nt = counts.shape[0]
    seg = (counts + SEG_ALIGN - 1) // SEG_ALIGN * SEG_ALIGN
    off = jnp.cumsum(seg, axis=1) - seg
    used = jnp.sum(seg, axis=0)
    region = (used + blk - 1) // blk * blk
    region_end = jnp.cumsum(region)
    region_start = region_end - region
    dst = region_start[None, :] + jnp.cumsum(seg, axis=0) - seg
    n_act = region_end[-1] // blk
    i32 = lambda v: v.astype(jnp.int32)
    return dict(dst=i32(dst.reshape(-1)), off=i32(off.reshape(-1)), len=i32(seg.reshape(-1)),
                tot=i32(jnp.sum(seg, axis=1)), fill_row=i32(region_start + used), fill_len=i32(region - used),
                n_act=i32(n_act.reshape(1)), first_block=i32(region_start // blk), n_blocks=i32(region // blk))


def kernel(x, w_in, w_attn_out, w_conv_out, w_o, conv_w, conv_b, conv_ln_g, conv_ln_b, ln_mix_g, ln_mix_b,
           expert_w_gate_up, expert_w_down, ln_ffn_g, ln_ffn_b, router_w, rel_bias):
    batch, seq, d = x.shape
    depth = w_in.shape[0]
    n = batch * seq
    aw, gw = ATTN_WIDTH, GROUP_WIDTH
    assert d % 256 == 0 and n % 1024 == 0
    for window, r in DILATION_GROUPS:
        assert window // r == BAND and seq % (r * BAND) == 0
    alpha = (2 * depth) ** 0.25
    scale = HEAD_DIM ** -0.5

    biases = [_band_bias(rel_bias, gi, r) for gi, (_, r) in enumerate(DILATION_GROUPS)]
    rwt = router_w.T.astype(BF16)
    nt = n // TOKEN_TILE
    nb = -(-(2 * n + N_EXPERTS * (nt * (SEG_ALIGN - 1) + ROW_BLOCK - 1)) // ROW_BLOCK)
    dilations = tuple(r for _, r in DILATION_GROUPS if r > 1)

    xf = x.reshape(n, d)
    xb, *xv = _input_views(xf, dilations)
    xviews = dict(zip(dilations, xv))
    for l in range(depth):
        wl = w_in[l]

        def qkv_cols(gi):
            return jnp.concatenate([wl[:, gi * gw:(gi + 1) * gw] * scale,
                                    wl[:, aw + gi * gw:aw + (gi + 1) * gw],
                                    wl[:, 2 * aw + gi * gw:2 * aw + (gi + 1) * gw]], axis=1)

        w0 = jnp.concatenate([wl[:, 3 * aw + d:], wl[:, 3 * aw:3 * aw + d], qkv_cols(0)], axis=1).astype(BF16)
        proj0 = _project(xb, w0, 512)
        outs, lses = [], []
        for gi, (_, r) in enumerate(DILATION_GROUPS):
            if r == 1:
                qkv, col0 = proj0.reshape(batch, seq, proj0.shape[1]), (3 * d) // gw
            else:
                qkv = _project_dilated(xviews[r], qkv_cols(gi).astype(BF16), batch, seq, r)
                qkv, col0 = qkv.reshape(batch * r, seq // r, 3 * gw), 0
            o, lse = _attention(qkv, biases[gi], batch, seq, r, col0)
            outs.append(o)
            lses.append(lse)

        lw = dict(wao=w_attn_out[l].astype(BF16), cw=conv_w[l], cb=conv_b[l][None], clg=conv_ln_g[l][None],
                  clb=conv_ln_b[l][None], wco=w_conv_out[l].astype(BF16), wo=w_o[l].astype(BF16),
                  lng=ln_mix_g[l][None], lnb=ln_mix_b[l][None], rwt=rwt, alpha=alpha)
        x1, x1b, gate, pos, cnt = _mixer_tail(outs, lses, proj0, xf, lw, batch, seq)

        tables = _routing_tables(cnt[:, :, 0].astype(jnp.int32))
        buf = _dispatch(x1b, pos, tables, nb)
        y_buf = _expert_ffn(buf, expert_w_gate_up, expert_w_down, l, tables)
        views = dilations if l + 1 < depth else ()
        xf, xb, *xv = _expert_merge(y_buf, pos.T, gate.T, x1, ln_ffn_g[l][None], ln_ffn_b[l][None], alpha,
                                    tables, views)
        xviews = dict(zip(views, xv))
    return xf.reshape(batch, seq, d)
```

```python
import functools
import math

import jax
import jax.numpy as jnp
from jax import lax
from jax.experimental import pallas as pl
from jax.experimental.pallas import tpu as pltpu

F32 = jnp.float32
BF16 = jnp.bfloat16

HEAD_DIM = 64
HEADS_PER_GROUP = 4
GROUP_WIDTH = HEADS_PER_GROUP * HEAD_DIM
DILATION_GROUPS = ((128, 1), (512, 4), (2048, 16))
N_GROUPS = len(DILATION_GROUPS)
ATTN_WIDTH = N_GROUPS * GROUP_WIDTH
BAND = 128
CONV_KERNEL = 31
CONV_HALO = 32
CONV_TAIL = 16
N_BUCKETS = 32
MAX_DISTANCE = 2048
N_EXPERTS = 16
EXPERTS_PER_GROUP = 4
LN_EPS = 1e-5
MASKED = -1e30

ROW_DMA_PRIORITY = 1
PROJ_ROWS = 2048
ROW_BLOCK = 256
TOKEN_TILE = 256
SEG_ALIGN = 8
SORTED_ROWS = -(-(2 * TOKEN_TILE + N_EXPERTS * (SEG_ALIGN - 1)) // 128) * 128
VMEM_LIMIT = 56 * 1024 * 1024


def _sigmoid(v):
    return 1.0 / (1.0 + jnp.exp(-v))


def _layer_norm(z, g, b):
    mu = jnp.mean(z, axis=-1, keepdims=True)
    zc = z - mu
    var = jnp.mean(zc * zc, axis=-1, keepdims=True)
    return zc * lax.rsqrt(var + LN_EPS) * g + b


def _params(*sem):
    return pltpu.CompilerParams(dimension_semantics=sem, vmem_limit_bytes=VMEM_LIMIT)


def _mm_kernel(x_ref, w_ref, o_ref):
    o_ref[...] = jnp.dot(x_ref[...], w_ref[...], preferred_element_type=F32).astype(o_ref.dtype)


def _project(xb, w, tm):
    n, d = xb.shape
    c = w.shape[1]
    return pl.pallas_call(
        _mm_kernel,
        grid=(n // tm,),
        in_specs=[pl.BlockSpec((tm, d), lambda i: (i, 0)),
                  pl.BlockSpec((d, c), lambda i: (0, 0))],
        out_specs=pl.BlockSpec((tm, c), lambda i: (i, 0)),
        out_shape=jax.ShapeDtypeStruct((n, c), BF16),
        compiler_params=_params("arbitrary"),
        name="proj_natural",
    )(xb, w)


def _project_dilated(xv, w, batch, seq, r):
    d, c = w.shape
    n = xv.shape[0] * r
    sub = seq // r
    tm = min(PROJ_ROWS, sub)
    nl = sub // tm
    cps = max(1, min(r, PROJ_ROWS // sub))
    assert r % cps == 0

    def mm_classes(x_ref, w_ref, o_ref):
        for k in range(cps):
            o_ref[k * tm:(k + 1) * tm, :] = jnp.dot(x_ref[:, k * d:(k + 1) * d], w_ref[...],
                                                    preferred_element_type=F32).astype(o_ref.dtype)

    return pl.pallas_call(
        mm_classes,
        grid=(batch, r // cps, nl),
        in_specs=[pl.BlockSpec((tm, cps * d), lambda b, cg, j: (b * nl + j, cg)),
                  pl.BlockSpec((d, c), lambda b, cg, j: (0, 0))],
        out_specs=pl.BlockSpec((cps * tm, c), lambda b, cg, j: ((b * (r // cps) + cg) * nl + j, 0)),
        out_shape=jax.ShapeDtypeStruct((n, c), BF16),
        compiler_params=_params("arbitrary", "arbitrary", "arbitrary"),
        name=f"proj_dilated_{r}",
    )(xv, w)


def _attn_kernel(q_ref, kp_ref, vp_ref, kc_ref, vc_ref, bias_ref, o_ref, lse_ref, k_scr, v_scr, *, nqb):
    i = pl.program_id(1)
    k_scr[0:BAND, :] = kp_ref[0]
    k_scr[BAND:, :] = kc_ref[0]
    v_scr[0:BAND, :] = vp_ref[0]
    v_scr[BAND:, :] = vc_ref[0]
    col = lax.broadcasted_iota(jnp.int32, (BAND, 2 * BAND), 1)
    first_head = lax.broadcasted_iota(jnp.int32, (BAND, 2 * HEAD_DIM), 1) < HEAD_DIM

    def body(j, carry):
        r0 = pl.multiple_of(j * BAND, BAND)
        q = q_ref[0, pl.ds(r0, BAND), :]
        kk = k_scr[pl.ds(r0, 2 * BAND), :]
        vv = v_scr[pl.ds(r0, 2 * BAND), :]
        no_prev = jnp.logical_and(jnp.logical_and(i == 0, j == 0), col < BAND)
        outs, lses = [], []
        for pair in range(HEADS_PER_GROUP // 2):
            sl = slice(pair * 2 * HEAD_DIM, (pair + 1) * 2 * HEAD_DIM)
            q2, k2, v2 = q[:, sl], kk[:, sl], vv[:, sl]
            o_pair, lse_pair = None, None
            for hh in range(2):
                mine = first_head if hh == 0 else jnp.logical_not(first_head)
                qm = jnp.where(mine, q2, jnp.zeros_like(q2))
                s = lax.dot_general(qm, k2, (((1,), (1,)), ((), ())), preferred_element_type=F32)
                s = s + bias_ref[2 * pair + hh]
                s = jnp.where(no_prev, MASKED, s)
                m = jnp.max(s, axis=-1, keepdims=True)
                p = jnp.exp(s - m)
                den = jnp.sum(p, axis=-1, keepdims=True)
                o = jnp.dot(p.astype(BF16), v2, preferred_element_type=F32) * (1.0 / den)
                lse = jnp.broadcast_to(m + jnp.log(den), (BAND, 2 * HEAD_DIM))
                o_pair = o if hh == 0 else jnp.where(first_head, o_pair, o)
                lse_pair = lse if hh == 0 else jnp.where(first_head, lse_pair, lse)
            outs.append(o_pair)
            lses.append(lse_pair)
        o_ref[0, pl.ds(r0, BAND), :] = jnp.concatenate(outs, axis=1).astype(o_ref.dtype)
        lse_ref[0, pl.ds(r0, BAND), :] = jnp.concatenate(lses, axis=1)
        return carry

    lax.fori_loop(0, nqb, body, 0, unroll=math.gcd(nqb, 8))


def _attention(qkv, bias, batch, seq, r, col0):
    sub = seq // r
    nqb = min(8, sub // BAND)
    rows = nqb * BAND
    nt = sub // rows
    gw = GROUP_WIDTH

    def cur(c):
        return pl.BlockSpec((1, rows, gw), lambda bc, i: (bc, i, c))

    def prev(c):
        return pl.BlockSpec((1, BAND, gw), lambda bc, i: (bc, jnp.maximum(i * nqb - 1, 0), c))

    out_spec = pl.BlockSpec((1, rows, gw), lambda bc, i: (bc // r, i, bc % r))
    o, lse = pl.pallas_call(
        functools.partial(_attn_kernel, nqb=nqb),
        grid=(batch * r, nt),
        in_specs=[cur(col0), prev(col0 + 1), prev(col0 + 2), cur(col0 + 1), cur(col0 + 2),
                  pl.BlockSpec((HEADS_PER_GROUP, BAND, 2 * BAND), lambda bc, i: (0, 0, 0))],
        out_specs=[out_spec, out_spec],
        out_shape=[jax.ShapeDtypeStruct((batch, sub, r * gw), BF16),
                   jax.ShapeDtypeStruct((batch, sub, r * gw), F32)],
        scratch_shapes=[pltpu.VMEM((rows + BAND, gw), BF16), pltpu.VMEM((rows + BAND, gw), BF16)],
        compiler_params=_params("arbitrary", "arbitrary"),
        name=f"attn_dilation_{r}",
    )(qkv, qkv, qkv, qkv, qkv, bias)
    return o.reshape(batch * sub, r * gw), lse.reshape(batch * sub, r * gw)


def _t5_bucket(dist):
    max_exact = N_BUCKETS // 2
    n = jnp.maximum(dist, 0)
    nf = jnp.maximum(n, 1).astype(F32)
    large = max_exact + (jnp.log(nf / max_exact) / math.log(MAX_DISTANCE / max_exact)
                         * (N_BUCKETS - max_exact)).astype(jnp.int32)
    large = jnp.minimum(large, N_BUCKETS - 1)
    return jnp.where(n < max_exact, n, large)


def _band_bias(rel_bias, gi, r):
    qi = jnp.arange(BAND)[:, None]
    kj = jnp.arange(2 * BAND)[None, :]
    dist = qi + BAND - kj
    bucket = _t5_bucket(dist * r)
    valid = (dist >= 0) & (dist <= BAND)
    b = jnp.full((HEADS_PER_GROUP, BAND, 2 * BAND), MASKED, F32)
    for k in range(N_BUCKETS):
        row = rel_bias[k, gi * HEADS_PER_GROUP:(gi + 1) * HEADS_PER_GROUP].astype(F32)
        b = jnp.where(((bucket == k) & valid)[None], row[:, None, None], b)
    return b


def _post_kernel(o0_ref, o1_ref, o2_ref, l0_ref, l1_ref, l2_ref, u_ref, uh_ref, g_ref, x_ref,
                 wao_ref, cw_ref, cb_ref, clg_ref, clb_ref, wco_ref, wo_ref, lng_ref, lnb_ref, rwt_ref,
                 x1_ref, x1b_ref, gate_ref, pos_ref, cnt_ref,
                 hs_scr, conv_scr, ys_scr, nat_scr, *, tm, seq, alpha):
    i = pl.program_id(0)
    d = x_ref.shape[1]
    c = d // 2
    gw = GROUP_WIDTH

    def natural(ref, slot, r):
        if r == 1:
            return ref[...].astype(F32)
        nl = gw // 128
        for cls in range(r):
            for j in range(nl):
                lanes = slice(cls * gw + j * 128, cls * gw + (j + 1) * 128)
                nat_scr[slot * nl + j, pl.ds(cls, tm // r, stride=r), :] = ref[:, lanes].astype(F32)
        return jnp.concatenate([nat_scr[slot * nl + j] for j in range(nl)], axis=1)

    rs = [r for _, r in DILATION_GROUPS]
    o0, o1, o2 = (natural(ref, s, r) for s, (ref, r) in enumerate(zip((o0_ref, o1_ref, o2_ref), rs)))
    l0, l1, l2 = (natural(ref, 3 + s, r) for s, (ref, r) in enumerate(zip((l0_ref, l1_ref, l2_ref), rs)))
    mx = jnp.maximum(jnp.maximum(l0, l1), l2)
    e0, e1, e2 = jnp.exp(l0 - mx), jnp.exp(l1 - mx), jnp.exp(l2 - mx)
    attn = (e0 * o0 + e1 * o1 + e2 * o2) * (1.0 / (e0 + e1 + e2))
    attn_branch = jnp.dot(attn.astype(BF16), wao_ref[...], preferred_element_type=F32)

    u = u_ref[...].astype(F32)
    uh = uh_ref[...].astype(F32)
    hh = uh[:, :c] * _sigmoid(uh[:, c:])
    seq_start = (i * tm) % seq == 0
    hs_scr[0:CONV_HALO, :] = jnp.where(seq_start, 0.0, hh)
    hs_scr[CONV_HALO:CONV_HALO + tm, :] = u[:, :c] * _sigmoid(u[:, c:])
    hs_scr[CONV_HALO + tm:, :] = jnp.zeros((CONV_TAIL, c), F32)
    off = CONV_HALO - (CONV_KERNEL - 1)
    rows_y = tm + 8
    for cc in range(c // 128):
        ls = slice(cc * 128, (cc + 1) * 128)
        acc = None
        for s in range(8):
            y = None
            for a in range((off + CONV_KERNEL + 7) // 8):
                k = 8 * a + s - off
                if 0 <= k < CONV_KERNEL:
                    term = hs_scr[8 * a:8 * a + rows_y, ls] * cw_ref[k:k + 1, ls]
                    y = term if y is None else y + term
            if s == 0:
                acc = y[0:tm]
            else:
                ys_scr[s] = y
                acc = acc + ys_scr[s, s:s + tm, :]
        conv_scr[:, ls] = acc
    hc = _layer_norm(conv_scr[...] + cb_ref[...], clg_ref[...], clb_ref[...])
    hc = hc * _sigmoid(hc)
    conv_branch = jnp.dot(hc.astype(BF16), wco_ref[...], preferred_element_type=F32)

    gates = _sigmoid(g_ref[...].astype(F32))
    merged = gates[:, :d] * attn_branch + gates[:, d:] * conv_branch
    hmix = jnp.dot(merged.astype(BF16), wo_ref[...], preferred_element_type=F32)
    x1 = _layer_norm(alpha * x_ref[...] + hmix, lng_ref[...], lnb_ref[...])
    x1_ref[...] = x1
    x1b = x1.astype(BF16)
    x1b_ref[...] = x1b

    logits = lax.dot_general(rwt_ref[...], x1b, (((1,), (1,)), ((), ())), preferred_element_type=F32)
    ex = jnp.exp(logits - jnp.max(logits, axis=0, keepdims=True))
    rows = [ex[e:e + 1, :] for e in range(N_EXPERTS)]
    best, gsel = None, None
    for g in range(N_EXPERTS // EXPERTS_PER_GROUP):
        v = rows[g * EXPERTS_PER_GROUP:(g + 1) * EXPERTS_PER_GROUP]
        score = None
        for a in range(EXPERTS_PER_GROUP):
            for b in range(a + 1, EXPERTS_PER_GROUP):
                ps = v[a] + v[b]
                score = ps if score is None else jnp.maximum(score, ps)
        if best is None:
            best, gsel = score, jnp.zeros_like(score, dtype=jnp.int32)
        else:
            upd = score > best
            gsel = jnp.where(upd, g, gsel)
            best = jnp.where(upd, score, best)
    vals = []
    for j in range(EXPERTS_PER_GROUP):
        vj = rows[j]
        for g in range(1, N_EXPERTS // EXPERTS_PER_GROUP):
            vj = jnp.where(gsel == g, rows[g * EXPERTS_PER_GROUP + j], vj)
        vals.append(vj)
    v1, i1 = vals[0], jnp.zeros_like(gsel)
    for j in range(1, EXPERTS_PER_GROUP):
        upd = vals[j] > v1
        i1 = jnp.where(upd, j, i1)
        v1 = jnp.where(upd, vals[j], v1)
    v2, i2 = jnp.full_like(v1, -1.0), jnp.zeros_like(gsel)
    for j in range(EXPERTS_PER_GROUP):
        upd = jnp.logical_and(i1 != j, vals[j] > v2)
        i2 = jnp.where(upd, j, i2)
        v2 = jnp.where(upd, vals[j], v2)
    ea = gsel * EXPERTS_PER_GROUP + i1
    eb = gsel * EXPERTS_PER_GROUP + i2
    inv = 1.0 / (v1 + v2)
    gate_ref[...] = jnp.concatenate([v1 * inv, v2 * inv], axis=0)

    eid = lax.broadcasted_iota(jnp.int32, (N_EXPERTS, tm), 0)
    hit_a = eid == ea
    hit_b = eid == eb
    onehot = jnp.where(jnp.logical_or(hit_a, hit_b), 1.0, 0.0)
    before = (lax.broadcasted_iota(jnp.int32, (tm, tm), 0)
              < lax.broadcasted_iota(jnp.int32, (tm, tm), 1))
    upper = jnp.where(before, 1.0, 0.0).astype(BF16)
    rank = jnp.dot(onehot.astype(BF16), upper, preferred_element_type=F32)
    count = jnp.sum(onehot, axis=1, keepdims=True)
    seg_len = jnp.floor((count + (SEG_ALIGN - 1)) * (1.0 / SEG_ALIGN)) * SEG_ALIGN
    lower = (lax.broadcasted_iota(jnp.int32, (N_EXPERTS, N_EXPERTS), 1)
             < lax.broadcasted_iota(jnp.int32, (N_EXPERTS, N_EXPERTS), 0))
    seg_off = jnp.dot(jnp.where(lower, 1.0, 0.0).astype(BF16),
                      jnp.broadcast_to(seg_len, (N_EXPERTS, 128)).astype(BF16),
                      preferred_element_type=F32)[:, 0:1]
    row = rank + seg_off
    pa = jnp.sum(jnp.where(hit_a, row, 0.0), axis=0, keepdims=True)
    pb = jnp.sum(jnp.where(hit_b, row, 0.0), axis=0, keepdims=True)
    pos_ref[...] = jnp.concatenate([pa, pb], axis=0).astype(jnp.int32)
    cnt_ref[0] = jnp.broadcast_to(count, (N_EXPERTS, 128))


def _mixer_tail(o, lse, proj0, x, lw, batch, seq):
    n, d = x.shape
    tm = TOKEN_TILE
    nt = n // tm
    gw = GROUP_WIDTH
    hb = tm // CONV_HALO

    def tok(width):
        return pl.BlockSpec((tm, width), lambda i: (i, 0))

    def const(shape):
        return pl.BlockSpec(shape, lambda i: (0,) * len(shape))

    def grouped(r):
        return pl.BlockSpec((tm // r, r * gw), lambda i: (i, 0))

    rs = [r for _, r in DILATION_GROUPS]
    in_specs = (
        [grouped(r) for r in rs] * 2
        + [pl.BlockSpec((tm, d), lambda i: (i, 2)),
           pl.BlockSpec((CONV_HALO, d), lambda i: (jnp.maximum(i * hb - 1, 0), 2)),
           pl.BlockSpec((tm, 2 * d), lambda i: (i, 0)),
           tok(d)]
        + [const(lw[k].shape) for k in ("wao", "cw", "cb", "clg", "clb", "wco", "wo", "lng", "lnb", "rwt")]
    )
    lane_row = pl.BlockSpec((2, tm), lambda i: (0, i))
    out_specs = [tok(d), tok(d), lane_row, lane_row,
                 pl.BlockSpec((1, N_EXPERTS, 128), lambda i: (i, 0, 0))]
    out_shape = [jax.ShapeDtypeStruct((n, d), F32), jax.ShapeDtypeStruct((n, d), BF16),
                 jax.ShapeDtypeStruct((2, n), F32), jax.ShapeDtypeStruct((2, n), jnp.int32),
                 jax.ShapeDtypeStruct((nt, N_EXPERTS, 128), F32)]
    alpha = lw["alpha"]
    return pl.pallas_call(
        functools.partial(_post_kernel, tm=tm, seq=seq, alpha=alpha),
        grid=(nt,),
        in_specs=in_specs,
        out_specs=out_specs,
        out_shape=out_shape,
        scratch_shapes=[pltpu.VMEM((CONV_HALO + tm + CONV_TAIL, d // 2), F32), pltpu.VMEM((tm, d // 2), F32),
                        pltpu.VMEM((8, tm + 8, 128), F32),
                        pltpu.VMEM((2 * N_GROUPS * (gw // 128), tm, 128), F32)],
        compiler_params=_params("arbitrary"),
        name="mixer_tail",
    )(o[0], o[1], o[2], lse[0], lse[1], lse[2], proj0, proj0, proj0, x,
      *[lw[k] for k in ("wao", "cw", "cb", "clg", "clb", "wco", "wo", "lng", "lnb", "rwt")])


def _pack_halves(v):
    h = v.shape[1] // 2
    bits = lax.bitcast_convert_type(v, jnp.uint32)
    return (bits[:, :h] >> 16) | (bits[:, h:] & jnp.uint32(0xFFFF0000))


def _unpack_halves(w):
    lo = lax.bitcast_convert_type(w << 16, F32).astype(BF16)
    hi = lax.bitcast_convert_type(w & jnp.uint32(0xFFFF0000), F32).astype(BF16)
    return lo, hi


def _dispatch_kernel(dst_ref, off_ref, len_ref, tot_ref, fill_row_ref, fill_len_ref, nact_ref,
                     x_ref, pos_ref, buf_hbm, xs, zeros, sem, zsem, *, nt, nb, blk):
    t = pl.program_id(0)
    cur = t % 2
    srows = xs.shape[1]

    def tile_wait(tile, buf):
        n = pl.multiple_of(tot_ref[tile], SEG_ALIGN)
        pltpu.make_async_copy(xs.at[buf, pl.ds(0, n)], buf_hbm.at[pl.ds(0, n)], sem.at[buf]).wait()

    @pl.when(t >= 2)
    def _():
        tile_wait(t - 2, cur)

    pos = pos_ref[...]
    row = lax.broadcasted_iota(jnp.int32, (srows, pos.shape[1]), 0)
    hit = jnp.logical_or(row == pos[0:1, :], row == pos[1:2, :])
    sel = jnp.where(hit, 1.0, 0.0).astype(BF16)
    xs[cur] = _pack_halves(jnp.dot(sel, x_ref[...], preferred_element_type=F32))

    for e in range(N_EXPERTS):
        k = t * N_EXPERTS + e
        n = pl.multiple_of(len_ref[k], SEG_ALIGN)
        src = pl.multiple_of(off_ref[k], SEG_ALIGN)
        dst = pl.multiple_of(dst_ref[k], SEG_ALIGN)

        @pl.when(n > 0)
        def _():
            pltpu.make_async_copy(xs.at[cur, pl.ds(src, n)], buf_hbm.at[pl.ds(dst, n)], sem.at[cur]).start()

    @pl.when(t == nt - 1)
    def _():
        if nt >= 2:
            tile_wait(t - 1, 1 - cur)
        tile_wait(t, cur)
        zeros[...] = jnp.zeros_like(zeros)

        def region_fill(e):
            n = pl.multiple_of(fill_len_ref[e], SEG_ALIGN)
            dst = pl.multiple_of(fill_row_ref[e], SEG_ALIGN)
            return n, pltpu.make_async_copy(zeros.at[pl.ds(0, n)], buf_hbm.at[pl.ds(dst, n)], zsem)

        def block_fill(b):
            return pltpu.make_async_copy(zeros, buf_hbm.at[pl.ds(pl.multiple_of(b * blk, blk), blk)], zsem)

        def each_block(fn):
            def body(b, carry):
                fn(block_fill(b))
                return carry
            lax.fori_loop(nact_ref[0], nb, body, 0)

        for e in range(N_EXPERTS):
            n, cp = region_fill(e)
            pl.when(n > 0)(cp.start)
        each_block(lambda cp: cp.start())
        for e in range(N_EXPERTS):
            n, cp = region_fill(e)
            pl.when(n > 0)(cp.wait)
        each_block(lambda cp: cp.wait())


def _dispatch(x1b, pos, tables, nb):
    n, d = x1b.shape
    tm = TOKEN_TILE
    nt = n // tm
    blk = ROW_BLOCK
    grid_spec = pltpu.PrefetchScalarGridSpec(
        num_scalar_prefetch=7,
        grid=(nt,),
        in_specs=[pl.BlockSpec((tm, d), lambda t, *_: (t, 0)),
                  pl.BlockSpec((2, tm), lambda t, *_: (0, t))],
        out_specs=pl.BlockSpec(memory_space=pl.ANY),
        scratch_shapes=[pltpu.VMEM((2, SORTED_ROWS, d // 2), jnp.uint32), pltpu.VMEM((blk, d // 2), jnp.uint32),
                        pltpu.SemaphoreType.DMA((2,)), pltpu.SemaphoreType.DMA],
    )
    return pl.pallas_call(
        functools.partial(_dispatch_kernel, nt=nt, nb=nb, blk=blk),
        grid_spec=grid_spec,
        out_shape=jax.ShapeDtypeStruct((nb * blk, d // 2), jnp.uint32),
        compiler_params=_params("arbitrary"),
        name="expert_dispatch",
    )(tables["dst"], tables["off"], tables["len"], tables["tot"], tables["fill_row"], tables["fill_len"],
      tables["n_act"], x1b, pos)


def _ffn_kernel(first_ref, count_ref, nact_ref, xs_hbm, wgu_ref, wdn_ref, y_hbm,
                wgu_b, wdn_b, xbuf, obuf, isem, osem, zsem, *, nb, blk):
    e = pl.program_id(0)
    f = wdn_ref.shape[2]
    half = xbuf.shape[2]
    b0 = first_ref[e]
    n_blocks = count_ref[e]

    def rows(j):
        return pl.ds(pl.multiple_of((b0 + j) * blk, blk), blk)

    def load(j, buf):
        return pltpu.make_async_copy(xs_hbm.at[rows(j)], xbuf.at[buf], isem.at[buf])

    def store(j, buf):
        return pltpu.make_async_copy(obuf.at[buf], y_hbm.at[rows(j)], osem.at[buf])

    @pl.when(n_blocks > 0)
    def _():
        load(0, 0).start(priority=ROW_DMA_PRIORITY)
        wgu_b[...] = wgu_ref[0, 0].astype(BF16)
        wdn_b[...] = wdn_ref[0, 0].astype(BF16)

    def block(j, carry):
        buf = j % 2

        @pl.when(j + 1 < n_blocks)
        def _():
            load(j + 1, 1 - buf).start(priority=ROW_DMA_PRIORITY)

        load(j, buf).wait()
        x_lo, x_hi = _unpack_halves(xbuf[buf])
        h = (jnp.dot(x_lo, wgu_b[0:half, :], preferred_element_type=F32)
             + jnp.dot(x_hi, wgu_b[half:, :], preferred_element_type=F32))
        a = h[:, :f]
        act = (a * _sigmoid(a) * h[:, f:]).astype(BF16)
        y = jnp.dot(act, wdn_b[...], preferred_element_type=F32)

        @pl.when(j >= 2)
        def _():
            store(j - 2, buf).wait()

        obuf[buf] = _pack_halves(y.astype(BF16).astype(F32))
        store(j, buf).start(priority=ROW_DMA_PRIORITY)
        return carry

    lax.fori_loop(0, n_blocks, block, 0)

    @pl.when(n_blocks >= 2)
    def _():
        store(n_blocks - 2, n_blocks % 2).wait()

    @pl.when(n_blocks >= 1)
    def _():
        store(n_blocks - 1, (n_blocks - 1) % 2).wait()

    @pl.when(e == pl.num_programs(0) - 1)
    def _():
        xbuf[0] = jnp.zeros((blk, half), jnp.uint32)

        def fill(b):
            return pltpu.make_async_copy(xbuf.at[0], y_hbm.at[pl.ds(pl.multiple_of(b * blk, blk), blk)], zsem)

        def each(fn):
            def body(b, carry):
                fn(fill(b))
                return carry
            lax.fori_loop(nact_ref[0], nb, body, 0)

        each(lambda cp: cp.start())
        each(lambda cp: cp.wait())


def _expert_ffn(buf, wgu, wdn, layer, tables):
    n_rows, half = buf.shape
    d = 2 * half
    blk = ROW_BLOCK
    nb = n_rows // blk
    f2 = wgu.shape[3]
    f = wdn.shape[2]
    grid_spec = pltpu.PrefetchScalarGridSpec(
        num_scalar_prefetch=3,
        grid=(N_EXPERTS,),
        in_specs=[pl.BlockSpec(memory_space=pl.ANY),
                  pl.BlockSpec((1, 1, d, f2), lambda e, *_: (layer, e, 0, 0)),
                  pl.BlockSpec((1, 1, f, d), lambda e, *_: (layer, e, 0, 0))],
        out_specs=pl.BlockSpec(memory_space=pl.ANY),
        scratch_shapes=[pltpu.VMEM((d, f2), BF16), pltpu.VMEM((f, d), BF16),
                        pltpu.VMEM((2, blk, half), jnp.uint32), pltpu.VMEM((2, blk, half), jnp.uint32),
                        pltpu.SemaphoreType.DMA((2,)), pltpu.SemaphoreType.DMA((2,)), pltpu.SemaphoreType.DMA],
    )
    return pl.pallas_call(
        functools.partial(_ffn_kernel, nb=nb, blk=blk),
        grid_spec=grid_spec,
        out_shape=jax.ShapeDtypeStruct((n_rows, half), jnp.uint32),
        compiler_params=_params("arbitrary"),
        name="expert_ffn",
    )(tables["first_block"], tables["n_blocks"], tables["n_act"], buf, wgu, wdn)


def _store_class_views(xt, scr, view_refs, views):
    tm, d = xt.shape
    if views:
        for j in range(d // 128):
            scr[j] = xt[:, j * 128:(j + 1) * 128]
    for ref, r in zip(view_refs, views):
        for cls in range(r):
            for j in range(d // 128):
                lanes = slice(cls * d + j * 128, cls * d + (j + 1) * 128)
                ref[:, lanes] = scr[j, pl.ds(cls, tm // r, stride=r), :].astype(BF16)


def _views_kernel(x_ref, xb_ref, *rest, views):
    view_refs, scr = rest[:len(views)], rest[len(views)]
    xt = x_ref[...]
    xb_ref[...] = xt.astype(BF16)
    _store_class_views(xt, scr, view_refs, views)


def _input_views(xf, views):
    n, d = xf.shape
    tm = TOKEN_TILE
    tok = pl.BlockSpec((tm, d), lambda t: (t, 0))
    return pl.pallas_call(
        functools.partial(_views_kernel, views=views),
        grid=(n // tm,),
        in_specs=[tok],
        out_specs=[tok] + [pl.BlockSpec((tm // r, r * d), lambda t: (t, 0)) for r in views],
        out_shape=[jax.ShapeDtypeStruct((n, d), BF16)]
        + [jax.ShapeDtypeStruct((n // r, r * d), BF16) for r in views],
        scratch_shapes=[pltpu.VMEM((d // 128, tm, 128), F32)],
        compiler_params=_params("arbitrary"),
        name="input_views",
    )(xf)


def _merge_kernel(dst_ref, off_ref, len_ref, tot_ref, y_hbm, x1_ref, pos_ref, gt_ref, g_ref, b_ref,
                  x2_ref, x2b_ref, *rest, nt, alpha, views):
    view_refs, (ys, x2_scr, sem) = rest[:len(views)], rest[len(views):]
    t = pl.program_id(0)
    cur = t % 2
    tm, d = x1_ref.shape
    srows = ys.shape[1]

    def fetch(tile, buf):
        for e in range(N_EXPERTS):
            k = tile * N_EXPERTS + e
            n = pl.multiple_of(len_ref[k], SEG_ALIGN)
            dst = pl.multiple_of(off_ref[k], SEG_ALIGN)
            src = pl.multiple_of(dst_ref[k], SEG_ALIGN)

            @pl.when(n > 0)
            def _():
                pltpu.make_async_copy(y_hbm.at[pl.ds(src, n)], ys.at[buf, pl.ds(dst, n)], sem.at[buf]).start()

    @pl.when(t == 0)
    def _():
        ys[...] = jnp.zeros_like(ys)
        fetch(t, cur)

    @pl.when(t + 1 < nt)
    def _():
        fetch(t + 1, 1 - cur)

    n_all = pl.multiple_of(tot_ref[t], SEG_ALIGN)
    pltpu.make_async_copy(y_hbm.at[pl.ds(0, n_all)], ys.at[cur, pl.ds(0, n_all)], sem.at[cur]).wait()

    y_lo, y_hi = _unpack_halves(ys[cur])
    pos = pos_ref[...]
    gt = gt_ref[...]
    lane = lax.broadcasted_iota(jnp.int32, (tm, srows), 1)
    sel = (jnp.where(lane == pos[:, 0:1], gt[:, 0:1], 0.0)
           + jnp.where(lane == pos[:, 1:2], gt[:, 1:2], 0.0)).astype(BF16)
    m = jnp.concatenate([jnp.dot(sel, y_lo, preferred_element_type=F32),
                         jnp.dot(sel, y_hi, preferred_element_type=F32)], axis=1)
    x2 = _layer_norm(alpha * x1_ref[...] + m, g_ref[...], b_ref[...])
    x2_ref[...] = x2
    x2b_ref[...] = x2.astype(BF16)
    _store_class_views(x2, x2_scr, view_refs, views)


def _expert_merge(y_buf, pos_t, gate_t, x1, g, b, alpha, tables, views):
    n, d = x1.shape
    tm = TOKEN_TILE
    nt = n // tm
    tok = pl.BlockSpec((tm, d), lambda t, *_: (t, 0))
    pair = pl.BlockSpec((tm, 2), lambda t, *_: (t, 0))
    vec = pl.BlockSpec((1, d), lambda t, *_: (0, 0))
    grid_spec = pltpu.PrefetchScalarGridSpec(
        num_scalar_prefetch=4,
        grid=(nt,),
        in_specs=[pl.BlockSpec(memory_space=pl.ANY), tok, pair, pair, vec, vec],
        out_specs=[tok, tok] + [pl.BlockSpec((tm // r, r * d), lambda t, *_: (t, 0)) for r in views],
        scratch_shapes=[pltpu.VMEM((2, SORTED_ROWS, d // 2), jnp.uint32), pltpu.VMEM((d // 128, tm, 128), F32),
                        pltpu.SemaphoreType.DMA((2,))],
    )
    return pl.pallas_call(
        functools.partial(_merge_kernel, nt=nt, alpha=alpha, views=views),
        grid_spec=grid_spec,
        out_shape=[jax.ShapeDtypeStruct((n, d), F32), jax.ShapeDtypeStruct((n, d), BF16)]
        + [jax.ShapeDtypeStruct((n // r, r * d), BF16) for r in views],
        compiler_params=_params("arbitrary"),
        name="expert_merge",
    )(tables["dst"], tables["off"], tables["len"], tables["tot"], y_buf, x1, pos_t, gate_t, g, b)


def _routing_tables(counts):
    blk = ROW_BLOCK
    nt = counts.shape[0]
    seg = (counts + SEG_ALIGN - 1) // SEG_ALIGN * SEG_ALIGN
    off = jnp.cumsum(seg, axis=1) - seg
    used = jnp.sum(seg, axis=0)
    region = (used + blk - 1) // blk * blk
    region_end = jnp.cumsum(region)
    region_start = region_end - region
    dst = region_start[None, :] + jnp.cumsum(seg, axis=0) - seg
    n_act = region_end[-1] // blk
    i32 = lambda v: v.astype(jnp.int32)
    return dict(dst=i32(dst.reshape(-1)), off=i32(off.reshape(-1)), len=i32(seg.reshape(-1)),
                tot=i32(jnp.sum(seg, axis=1)), fill_row=i32(region_start + used), fill_len=i32(region - used),
                n_act=i32(n_act.reshape(1)), first_block=i32(region_start // blk), n_blocks=i32(region // blk))


def kernel(x, w_in, w_attn_out, w_conv_out, w_o, conv_w, conv_b, conv_ln_g, conv_ln_b, ln_mix_g, ln_mix_b,
           expert_w_gate_up, expert_w_down, ln_ffn_g, ln_ffn_b, router_w, rel_bias):
    batch, seq, d = x.shape
    depth = w_in.shape[0]
    n = batch * seq
    aw, gw = ATTN_WIDTH, GROUP_WIDTH
    assert d % 256 == 0 and n % 1024 == 0
    for window, r in DILATION_GROUPS:
        assert window // r == BAND and seq % (r * BAND) == 0
    alpha = (2 * depth) ** 0.25
    scale = HEAD_DIM ** -0.5

    biases = [_band_bias(rel_bias, gi, r) for gi, (_, r) in enumerate(DILATION_GROUPS)]
    rwt = router_w.T.astype(BF16)
    nt = n // TOKEN_TILE
    nb = -(-(2 * n + N_EXPERTS * (nt * (SEG_ALIGN - 1) + ROW_BLOCK - 1)) // ROW_BLOCK)
    dilations = tuple(r for _, r in DILATION_GROUPS if r > 1)

    xf = x.reshape(n, d)
    xb, *xv = _input_views(xf, dilations)
    xviews = dict(zip(dilations, xv))
    for l in range(depth):
        wl = w_in[l]

        def qkv_cols(gi):
            return jnp.concatenate([wl[:, gi * gw:(gi + 1) * gw] * scale,
                                    wl[:, aw + gi * gw:aw + (gi + 1) * gw],
                                    wl[:, 2 * aw + gi * gw:2 * aw + (gi + 1) * gw]], axis=1)

        w0 = jnp.concatenate([wl[:, 3 * aw + d:], wl[:, 3 * aw:3 * aw + d], qkv_cols(0)], axis=1).astype(BF16)
        proj0 = _project(xb, w0, 512)
        outs, lses = [], []
        for gi, (_, r) in enumerate(DILATION_GROUPS):
            if r == 1:
                qkv, col0 = proj0.reshape(batch, seq, proj0.shape[1]), (3 * d) // gw
            else:
                qkv = _project_dilated(xviews[r], qkv_cols(gi).astype(BF16), batch, seq, r)
                qkv, col0 = qkv.reshape(batch * r, seq // r, 3 * gw), 0
            o, lse = _attention(qkv, biases[gi], batch, seq, r, col0)
            outs.append(o)
            lses.append(lse)

        lw = dict(wao=w_attn_out[l].astype(BF16), cw=conv_w[l], cb=conv_b[l][None], clg=conv_ln_g[l][None],
                  clb=conv_ln_b[l][None], wco=w_conv_out[l].astype(BF16), wo=w_o[l].astype(BF16),
                  lng=ln_mix_g[l][None], lnb=ln_mix_b[l][None], rwt=rwt, alpha=alpha)
        x1, x1b, gate, pos, cnt = _mixer_tail(outs, lses, proj0, xf, lw, batch, seq)

        tables = _routing_tables(cnt[:, :, 0].astype(jnp.int32))
        buf = _dispatch(x1b, pos, tables, nb)
        y_buf = _expert_ffn(buf, expert_w_gate_up, expert_w_down, l, tables)
        views = dilations if l + 1 < depth else ()
        xf, xb, *xv = _expert_merge(y_buf, pos.T, gate.T, x1, ln_ffn_g[l][None], ln_ffn_b[l][None], alpha,
                                    tables, views)
        xviews = dict(zip(views, xv))
    return xf.reshape(batch, seq, d)
```

```python
import functools
import math

import jax
import jax.numpy as jnp
from jax import lax
from jax.experimental import pallas as pl
from jax.experimental.pallas import tpu as pltpu

F32 = jnp.float32
BF16 = jnp.bfloat16

HEAD_DIM = 64
HEADS_PER_GROUP = 4
GROUP_WIDTH = HEADS_PER_GROUP * HEAD_DIM
DILATION_GROUPS = ((128, 1), (512, 4), (2048, 16))
N_GROUPS = len(DILATION_GROUPS)
ATTN_WIDTH = N_GROUPS * GROUP_WIDTH
BAND = 128
CONV_KERNEL = 31
CONV_HALO = 32
CONV_TAIL = 16
N_BUCKETS = 32
MAX_DISTANCE = 2048
N_EXPERTS = 16
EXPERTS_PER_GROUP = 4
LN_EPS = 1e-5
MASKED = -1e30

ROW_DMA_PRIORITY = 1
PROJ_ROWS = 2048
ROW_BLOCK = 512
TOKEN_TILE = 256
SEG_ALIGN = 8
SORTED_ROWS = -(-(2 * TOKEN_TILE + N_EXPERTS * (SEG_ALIGN - 1)) // 128) * 128
VMEM_LIMIT = 56 * 1024 * 1024


def _sigmoid(v):
    return 1.0 / (1.0 + jnp.exp(-v))


def _layer_norm(z, g, b):
    mu = jnp.mean(z, axis=-1, keepdims=True)
    zc = z - mu
    var = jnp.mean(zc * zc, axis=-1, keepdims=True)
    return zc * lax.rsqrt(var + LN_EPS) * g + b


def _params(*sem):
    return pltpu.CompilerParams(dimension_semantics=sem, vmem_limit_bytes=VMEM_LIMIT)


def _mm_kernel(x_ref, w_ref, o_ref):
    o_ref[...] = jnp.dot(x_ref[...], w_ref[...], preferred_element_type=F32).astype(o_ref.dtype)


def _project(xb, w, tm):
    n, d = xb.shape
    c = w.shape[1]
    return pl.pallas_call(
        _mm_kernel,
        grid=(n // tm,),
        in_specs=[pl.BlockSpec((tm, d), lambda i: (i, 0)),
                  pl.BlockSpec((d, c), lambda i: (0, 0))],
        out_specs=pl.BlockSpec((tm, c), lambda i: (i, 0)),
        out_shape=jax.ShapeDtypeStruct((n, c), BF16),
        compiler_params=_params("arbitrary"),
        name="proj_natural",
    )(xb, w)


def _project_dilated(xv, w, batch, seq, r):
    d, c = w.shape
    n = xv.shape[0] * r
    sub = seq // r
    tm = min(PROJ_ROWS, sub)
    nl = sub // tm
    cps = max(1, min(r, PROJ_ROWS // sub))
    assert r % cps == 0

    def mm_classes(x_ref, w_ref, o_ref):
        for k in range(cps):
            o_ref[k * tm:(k + 1) * tm, :] = jnp.dot(x_ref[:, k * d:(k + 1) * d], w_ref[...],
                                                    preferred_element_type=F32).astype(o_ref.dtype)

    return pl.pallas_call(
        mm_classes,
        grid=(batch, r // cps, nl),
        in_specs=[pl.BlockSpec((tm, cps * d), lambda b, cg, j: (b * nl + j, cg)),
                  pl.BlockSpec((d, c), lambda b, cg, j: (0, 0))],
        out_specs=pl.BlockSpec((cps * tm, c), lambda b, cg, j: ((b * (r // cps) + cg) * nl + j, 0)),
        out_shape=jax.ShapeDtypeStruct((n, c), BF16),
        compiler_params=_params("arbitrary", "arbitrary", "arbitrary"),
        name=f"proj_dilated_{r}",
    )(xv, w)


def _attn_kernel(q_ref, kp_ref, vp_ref, kc_ref, vc_ref, bias_ref, o_ref, lse_ref, k_scr, v_scr, *, nqb):
    i = pl.program_id(1)
    k_scr[0:BAND, :] = kp_ref[0]
    k_scr[BAND:, :] = kc_ref[0]
    v_scr[0:BAND, :] = vp_ref[0]
    v_scr[BAND:, :] = vc_ref[0]
    col = lax.broadcasted_iota(jnp.int32, (BAND, 2 * BAND), 1)
    first_head = lax.broadcasted_iota(jnp.int32, (BAND, 2 * HEAD_DIM), 1) < HEAD_DIM

    def body(j, carry):
        r0 = pl.multiple_of(j * BAND, BAND)
        q = q_ref[0, pl.ds(r0, BAND), :]
        kk = k_scr[pl.ds(r0, 2 * BAND), :]
        vv = v_scr[pl.ds(r0, 2 * BAND), :]
        no_prev = jnp.logical_and(jnp.logical_and(i == 0, j == 0), col < BAND)
        outs, lses = [], []
        for pair in range(HEADS_PER_GROUP // 2):
            sl = slice(pair * 2 * HEAD_DIM, (pair + 1) * 2 * HEAD_DIM)
            q2, k2, v2 = q[:, sl], kk[:, sl], vv[:, sl]
            o_pair, lse_pair = None, None
            for hh in range(2):
                mine = first_head if hh == 0 else jnp.logical_not(first_head)
                qm = jnp.where(mine, q2, jnp.zeros_like(q2))
                s = lax.dot_general(qm, k2, (((1,), (1,)), ((), ())), preferred_element_type=F32)
                s = s + bias_ref[2 * pair + hh]
                s = jnp.where(no_prev, MASKED, s)
                m = jnp.max(s, axis=-1, keepdims=True)
                p = jnp.exp(s - m)
                den = jnp.sum(p, axis=-1, keepdims=True)
                o = jnp.dot(p.astype(BF16), v2, preferred_element_type=F32) * (1.0 / den)
                lse = jnp.broadcast_to(m + jnp.log(den), (BAND, 2 * HEAD_DIM))
                o_pair = o if hh == 0 else jnp.where(first_head, o_pair, o)
                lse_pair = lse if hh == 0 else jnp.where(first_head, lse_pair, lse)
            outs.append(o_pair)
            lses.append(lse_pair)
        o_ref[0, pl.ds(r0, BAND), :] = jnp.concatenate(outs, axis=1).astype(o_ref.dtype)
        lse_ref[0, pl.ds(r0, BAND), :] = jnp.concatenate(lses, axis=1)
        return carry

    lax.fori_loop(0, nqb, body, 0, unroll=math.gcd(nqb, 8))


def _attention(qkv, bias, batch, seq, r, col0):
    sub = seq // r
    nqb = min(8, sub // BAND)
    rows = nqb * BAND
    nt = sub // rows
    gw = GROUP_WIDTH

    def cur(c):
        return pl.BlockSpec((1, rows, gw), lambda bc, i: (bc, i, c))

    def prev(c):
        return pl.BlockSpec((1, BAND, gw), lambda bc, i: (bc, jnp.maximum(i * nqb - 1, 0), c))

    out_spec = pl.BlockSpec((1, rows, gw), lambda bc, i: (bc // r, i, bc % r))
    o, lse = pl.pallas_call(
        functools.partial(_attn_kernel, nqb=nqb),
        grid=(batch * r, nt),
        in_specs=[cur(col0), prev(col0 + 1), prev(col0 + 2), cur(col0 + 1), cur(col0 + 2),
                  pl.BlockSpec((HEADS_PER_GROUP, BAND, 2 * BAND), lambda bc, i: (0, 0, 0))],
        out_specs=[out_spec, out_spec],
        out_shape=[jax.ShapeDtypeStruct((batch, sub, r * gw), BF16),
                   jax.ShapeDtypeStruct((batch, sub, r * gw), F32)],
        scratch_shapes=[pltpu.VMEM((rows + BAND, gw), BF16), pltpu.VMEM((rows + BAND, gw), BF16)],
        compiler_params=_params("arbitrary", "arbitrary"),
        name=f"attn_dilation_{r}",
    )(qkv, qkv, qkv, qkv, qkv, bias)
    return o.reshape(batch * sub, r * gw), lse.reshape(batch * sub, r * gw)


def _t5_bucket(dist):
    max_exact = N_BUCKETS // 2
    n = jnp.maximum(dist, 0)
    nf = jnp.maximum(n, 1).astype(F32)
    large = max_exact + (jnp.log(nf / max_exact) / math.log(MAX_DISTANCE / max_exact)
                         * (N_BUCKETS - max_exact)).astype(jnp.int32)
    large = jnp.minimum(large, N_BUCKETS - 1)
    return jnp.where(n < max_exact, n, large)


def _band_bias(rel_bias, gi, r):
    qi = jnp.arange(BAND)[:, None]
    kj = jnp.arange(2 * BAND)[None, :]
    dist = qi + BAND - kj
    bucket = _t5_bucket(dist * r)
    valid = (dist >= 0) & (dist <= BAND)
    b = jnp.full((HEADS_PER_GROUP, BAND, 2 * BAND), MASKED, F32)
    for k in range(N_BUCKETS):
        row = rel_bias[k, gi * HEADS_PER_GROUP:(gi + 1) * HEADS_PER_GROUP].astype(F32)
        b = jnp.where(((bucket == k) & valid)[None], row[:, None, None], b)
    return b


def _post_kernel(o0_ref, o1_ref, o2_ref, l0_ref, l1_ref, l2_ref, u_ref, uh_ref, g_ref, x_ref,
                 wao_ref, cw_ref, cb_ref, clg_ref, clb_ref, wco_ref, wo_ref, lng_ref, lnb_ref, rwt_ref,
                 x1_ref, x1b_ref, gate_ref, pos_ref, cnt_ref,
                 hs_scr, conv_scr, ys_scr, nat_scr, *, tm, seq, alpha):
    i = pl.program_id(0)
    d = x_ref.shape[1]
    c = d // 2
    gw = GROUP_WIDTH

    def natural(ref, slot, r):
        if r == 1:
            return ref[...].astype(F32)
        nl = gw // 128
        for cls in range(r):
            for j in range(nl):
                lanes = slice(cls * gw + j * 128, cls * gw + (j + 1) * 128)
                nat_scr[slot * nl + j, pl.ds(cls, tm // r, stride=r), :] = ref[:, lanes].astype(F32)
        return jnp.concatenate([nat_scr[slot * nl + j] for j in range(nl)], axis=1)

    rs = [r for _, r in DILATION_GROUPS]
    o0, o1, o2 = (natural(ref, s, r) for s, (ref, r) in enumerate(zip((o0_ref, o1_ref, o2_ref), rs)))
    l0, l1, l2 = (natural(ref, 3 + s, r) for s, (ref, r) in enumerate(zip((l0_ref, l1_ref, l2_ref), rs)))
    mx = jnp.maximum(jnp.maximum(l0, l1), l2)
    e0, e1, e2 = jnp.exp(l0 - mx), jnp.exp(l1 - mx), jnp.exp(l2 - mx)
    attn = (e0 * o0 + e1 * o1 + e2 * o2) * (1.0 / (e0 + e1 + e2))
    attn_branch = jnp.dot(attn.astype(BF16), wao_ref[...], preferred_element_type=F32)

    u = u_ref[...].astype(F32)
    uh = uh_ref[...].astype(F32)
    hh = uh[:, :c] * _sigmoid(uh[:, c:])
    seq_start = (i * tm) % seq == 0
    hs_scr[0:CONV_HALO, :] = jnp.where(seq_start, 0.0, hh)
    hs_scr[CONV_HALO:CONV_HALO + tm, :] = u[:, :c] * _sigmoid(u[:, c:])
    hs_scr[CONV_HALO + tm:, :] = jnp.zeros((CONV_TAIL, c), F32)
    off = CONV_HALO - (CONV_KERNEL - 1)
    rows_y = tm + 8
    for cc in range(c // 128):
        ls = slice(cc * 128, (cc + 1) * 128)
        acc = None
        for s in range(8):
            y = None
            for a in range((off + CONV_KERNEL + 7) // 8):
                k = 8 * a + s - off
                if 0 <= k < CONV_KERNEL:
                    term = hs_scr[8 * a:8 * a + rows_y, ls] * cw_ref[k:k + 1, ls]
                    y = term if y is None else y + term
            if s == 0:
                acc = y[0:tm]
            else:
                ys_scr[s] = y
                acc = acc + ys_scr[s, s:s + tm, :]
        conv_scr[:, ls] = acc
    hc = _layer_norm(conv_scr[...] + cb_ref[...], clg_ref[...], clb_ref[...])
    hc = hc * _sigmoid(hc)
    conv_branch = jnp.dot(hc.astype(BF16), wco_ref[...], preferred_element_type=F32)

    gates = _sigmoid(g_ref[...].astype(F32))
    merged = gates[:, :d] * attn_branch + gates[:, d:] * conv_branch
    hmix = jnp.dot(merged.astype(BF16), wo_ref[...], preferred_element_type=F32)
    x1 = _layer_norm(alpha * x_ref[...] + hmix, lng_ref[...], lnb_ref[...])
    x1_ref[...] = x1
    x1b = x1.astype(BF16)
    x1b_ref[...] = x1b

    logits = lax.dot_general(rwt_ref[...], x1b, (((1,), (1,)), ((), ())), preferred_element_type=F32)
    ex = jnp.exp(logits - jnp.max(logits, axis=0, keepdims=True))
    rows = [ex[e:e + 1, :] for e in range(N_EXPERTS)]
    best, gsel = None, None
    for g in range(N_EXPERTS // EXPERTS_PER_GROUP):
        v = rows[g * EXPERTS_PER_GROUP:(g + 1) * EXPERTS_PER_GROUP]
        score = None
        for a in range(EXPERTS_PER_GROUP):
            for b in range(a + 1, EXPERTS_PER_GROUP):
                ps = v[a] + v[b]
                score = ps if score is None else jnp.maximum(score, ps)
        if best is None:
            best, gsel = score, jnp.zeros_like(score, dtype=jnp.int32)
        else:
            upd = score > best
            gsel = jnp.where(upd, g, gsel)
            best = jnp.where(upd, score, best)
    vals = []
    for j in range(EXPERTS_PER_GROUP):
        vj = rows[j]
        for g in range(1, N_EXPERTS // EXPERTS_PER_GROUP):
            vj = jnp.where(gsel == g, rows[g * EXPERTS_PER_GROUP + j], vj)
        vals.append(vj)
    v1, i1 = vals[0], jnp.zeros_like(gsel)
    for j in range(1, EXPERTS_PER_GROUP):
        upd = vals[j] > v1
        i1 = jnp.where(upd, j, i1)
        v1 = jnp.where(upd, vals[j], v1)
    v2, i2 = jnp.full_like(v1, -1.0), jnp.zeros_like(gsel)
    for j in range(EXPERTS_PER_GROUP):
        upd = jnp.logical_and(i1 != j, vals[j] > v2)
        i2 = jnp.where(upd, j, i2)
        v2 = jnp.where(upd, vals[j], v2)
    ea = gsel * EXPERTS_PER_GROUP + i1
    eb = gsel * EXPERTS_PER_GROUP + i2
    inv = 1.0 / (v1 + v2)
    gate_ref[...] = jnp.concatenate([v1 * inv, v2 * inv], axis=0)

    eid = lax.broadcasted_iota(jnp.int32, (N_EXPERTS, tm), 0)
    hit_a = eid == ea
    hit_b = eid == eb
    onehot = jnp.where(jnp.logical_or(hit_a, hit_b), 1.0, 0.0)
    before = (lax.broadcasted_iota(jnp.int32, (tm, tm), 0)
              < lax.broadcasted_iota(jnp.int32, (tm, tm), 1))
    upper = jnp.where(before, 1.0, 0.0).astype(BF16)
    rank = jnp.dot(onehot.astype(BF16), upper, preferred_element_type=F32)
    count = jnp.sum(onehot, axis=1, keepdims=True)
    seg_len = jnp.floor((count + (SEG_ALIGN - 1)) * (1.0 / SEG_ALIGN)) * SEG_ALIGN
    lower = (lax.broadcasted_iota(jnp.int32, (N_EXPERTS, N_EXPERTS), 1)
             < lax.broadcasted_iota(jnp.int32, (N_EXPERTS, N_EXPERTS), 0))
    seg_off = jnp.dot(jnp.where(lower, 1.0, 0.0).astype(BF16),
                      jnp.broadcast_to(seg_len, (N_EXPERTS, 128)).astype(BF16),
                      preferred_element_type=F32)[:, 0:1]
    row = rank + seg_off
    pa = jnp.sum(jnp.where(hit_a, row, 0.0), axis=0, keepdims=True)
    pb = jnp.sum(jnp.where(hit_b, row, 0.0), axis=0, keepdims=True)
    pos_ref[...] = jnp.concatenate([pa, pb], axis=0).astype(jnp.int32)
    cnt_ref[0] = jnp.broadcast_to(count, (N_EXPERTS, 128))


def _mixer_tail(o, lse, proj0, x, lw, batch, seq):
    n, d = x.shape
    tm = TOKEN_TILE
    nt = n // tm
    gw = GROUP_WIDTH
    hb = tm // CONV_HALO

    def tok(width):
        return pl.BlockSpec((tm, width), lambda i: (i, 0))

    def const(shape):
        return pl.BlockSpec(shape, lambda i: (0,) * len(shape))

    def grouped(r):
        return pl.BlockSpec((tm // r, r * gw), lambda i: (i, 0))

    rs = [r for _, r in DILATION_GROUPS]
    in_specs = (
        [grouped(r) for r in rs] * 2
        + [pl.BlockSpec((tm, d), lambda i: (i, 2)),
           pl.BlockSpec((CONV_HALO, d), lambda i: (jnp.maximum(i * hb - 1, 0), 2)),
           pl.BlockSpec((tm, 2 * d), lambda i: (i, 0)),
           tok(d)]
        + [const(lw[k].shape) for k in ("wao", "cw", "cb", "clg", "clb", "wco", "wo", "lng", "lnb", "rwt")]
    )
    lane_row = pl.BlockSpec((2, tm), lambda i: (0, i))
    out_specs = [tok(d), tok(d), lane_row, lane_row,
                 pl.BlockSpec((1, N_EXPERTS, 128), lambda i: (i, 0, 0))]
    out_shape = [jax.ShapeDtypeStruct((n, d), F32), jax.ShapeDtypeStruct((n, d), BF16),
                 jax.ShapeDtypeStruct((2, n), F32), jax.ShapeDtypeStruct((2, n), jnp.int32),
                 jax.ShapeDtypeStruct((nt, N_EXPERTS, 128), F32)]
    alpha = lw["alpha"]
    return pl.pallas_call(
        functools.partial(_post_kernel, tm=tm, seq=seq, alpha=alpha),
        grid=(nt,),
        in_specs=in_specs,
        out_specs=out_specs,
        out_shape=out_shape,
        scratch_shapes=[pltpu.VMEM((CONV_HALO + tm + CONV_TAIL, d // 2), F32), pltpu.VMEM((tm, d // 2), F32),
                        pltpu.VMEM((8, tm + 8, 128), F32),
                        pltpu.VMEM((2 * N_GROUPS * (gw // 128), tm, 128), F32)],
        compiler_params=_params("arbitrary"),
        name="mixer_tail",
    )(o[0], o[1], o[2], lse[0], lse[1], lse[2], proj0, proj0, proj0, x,
      *[lw[k] for k in ("wao", "cw", "cb", "clg", "clb", "wco", "wo", "lng", "lnb", "rwt")])


def _pack_halves(v):
    h = v.shape[1] // 2
    bits = lax.bitcast_convert_type(v, jnp.uint32)
    return (bits[:, :h] >> 16) | (bits[:, h:] & jnp.uint32(0xFFFF0000))


def _unpack_halves(w):
    lo = lax.bitcast_convert_type(w << 16, F32).astype(BF16)
    hi = lax.bitcast_convert_type(w & jnp.uint32(0xFFFF0000), F32).astype(BF16)
    return lo, hi


def _dispatch_kernel(dst_ref, off_ref, len_ref, tot_ref, fill_row_ref, fill_len_ref, nact_ref,
                     x_ref, pos_ref, buf_hbm, xs, zeros, sem, zsem, *, nt, nb, blk):
    t = pl.program_id(0)
    cur = t % 2
    srows = xs.shape[1]

    def tile_wait(tile, buf):
        n = pl.multiple_of(tot_ref[tile], SEG_ALIGN)
        pltpu.make_async_copy(xs.at[buf, pl.ds(0, n)], buf_hbm.at[pl.ds(0, n)], sem.at[buf]).wait()

    @pl.when(t >= 2)
    def _():
        tile_wait(t - 2, cur)

    pos = pos_ref[...]
    row = lax.broadcasted_iota(jnp.int32, (srows, pos.shape[1]), 0)
    hit = jnp.logical_or(row == pos[0:1, :], row == pos[1:2, :])
    sel = jnp.where(hit, 1.0, 0.0).astype(BF16)
    xs[cur] = _pack_halves(jnp.dot(sel, x_ref[...], preferred_element_type=F32))

    for e in range(N_EXPERTS):
        k = t * N_EXPERTS + e
        n = pl.multiple_of(len_ref[k], SEG_ALIGN)
        src = pl.multiple_of(off_ref[k], SEG_ALIGN)
        dst = pl.multiple_of(dst_ref[k], SEG_ALIGN)

        @pl.when(n > 0)
        def _():
            pltpu.make_async_copy(xs.at[cur, pl.ds(src, n)], buf_hbm.at[pl.ds(dst, n)], sem.at[cur]).start()

    @pl.when(t == nt - 1)
    def _():
        if nt >= 2:
            tile_wait(t - 1, 1 - cur)
        tile_wait(t, cur)
        zeros[...] = jnp.zeros_like(zeros)

        def region_fill(e):
            n = pl.multiple_of(fill_len_ref[e], SEG_ALIGN)
            dst = pl.multiple_of(fill_row_ref[e], SEG_ALIGN)
            return n, pltpu.make_async_copy(zeros.at[pl.ds(0, n)], buf_hbm.at[pl.ds(dst, n)], zsem)

        def block_fill(b):
            return pltpu.make_async_copy(zeros, buf_hbm.at[pl.ds(pl.multiple_of(b * blk, blk), blk)], zsem)

        def each_block(fn):
            def body(b, carry):
                fn(block_fill(b))
                return carry
            lax.fori_loop(nact_ref[0], nb, body, 0)

        for e in range(N_EXPERTS):
            n, cp = region_fill(e)
            pl.when(n > 0)(cp.start)
        each_block(lambda cp: cp.start())
        for e in range(N_EXPERTS):
            n, cp = region_fill(e)
            pl.when(n > 0)(cp.wait)
        each_block(lambda cp: cp.wait())


def _dispatch(x1b, pos, tables, nb):
    n, d = x1b.shape
    tm = TOKEN_TILE
    nt = n // tm
    blk = ROW_BLOCK
    grid_spec = pltpu.PrefetchScalarGridSpec(
        num_scalar_prefetch=7,
        grid=(nt,),
        in_specs=[pl.BlockSpec((tm, d), lambda t, *_: (t, 0)),
                  pl.BlockSpec((2, tm), lambda t, *_: (0, t))],
        out_specs=pl.BlockSpec(memory_space=pl.ANY),
        scratch_shapes=[pltpu.VMEM((2, SORTED_ROWS, d // 2), jnp.uint32), pltpu.VMEM((blk, d // 2), jnp.uint32),
                        pltpu.SemaphoreType.DMA((2,)), pltpu.SemaphoreType.DMA],
    )
    return pl.pallas_call(
        functools.partial(_dispatch_kernel, nt=nt, nb=nb, blk=blk),
        grid_spec=grid_spec,
        out_shape=jax.ShapeDtypeStruct((nb * blk, d // 2), jnp.uint32),
        compiler_params=_params("arbitrary"),
        name="expert_dispatch",
    )(tables["dst"], tables["off"], tables["len"], tables["tot"], tables["fill_row"], tables["fill_len"],
      tables["n_act"], x1b, pos)


def _ffn_kernel(first_ref, count_ref, nact_ref, xs_hbm, wgu_ref, wdn_ref, y_hbm,
                wgu_b, wdn_b, xbuf, obuf, isem, osem, zsem, *, nb, blk):
    e = pl.program_id(0)
    f = wdn_ref.shape[2]
    half = xbuf.shape[2]
    b0 = first_ref[e]
    n_blocks = count_ref[e]

    def rows(j):
        return pl.ds(pl.multiple_of((b0 + j) * blk, blk), blk)

    def load(j, buf):
        return pltpu.make_async_copy(xs_hbm.at[rows(j)], xbuf.at[buf], isem.at[buf])

    def store(j, buf):
        return pltpu.make_async_copy(obuf.at[buf], y_hbm.at[rows(j)], osem.at[buf])

    @pl.when(n_blocks > 0)
    def _():
        load(0, 0).start(priority=ROW_DMA_PRIORITY)
        wgu_b[...] = wgu_ref[0, 0].astype(BF16)
        wdn_b[...] = wdn_ref[0, 0].astype(BF16)

    def block(j, carry):
        buf = j % 2

        @pl.when(j + 1 < n_blocks)
        def _():
            load(j + 1, 1 - buf).start(priority=ROW_DMA_PRIORITY)

        load(j, buf).wait()
        x_lo, x_hi = _unpack_halves(xbuf[buf])
        h = (jnp.dot(x_lo, wgu_b[0:half, :], preferred_element_type=F32)
             + jnp.dot(x_hi, wgu_b[half:, :], preferred_element_type=F32))
        a = h[:, :f]
        act = (a * _sigmoid(a) * h[:, f:]).astype(BF16)
        y = jnp.dot(act, wdn_b[...], preferred_element_type=F32)

        @pl.when(j >= 2)
        def _():
            store(j - 2, buf).wait()

        obuf[buf] = _pack_halves(y.astype(BF16).astype(F32))
        store(j, buf).start(priority=ROW_DMA_PRIORITY)
        return carry

    lax.fori_loop(0, n_blocks, block, 0)

    @pl.when(n_blocks >= 2)
    def _():
        store(n_blocks - 2, n_blocks % 2).wait()

    @pl.when(n_blocks >= 1)
    def _():
        store(n_blocks - 1, (n_blocks - 1) % 2).wait()

    @pl.when(e == pl.num_programs(0) - 1)
    def _():
        xbuf[0] = jnp.zeros((blk, half), jnp.uint32)

        def fill(b):
            return pltpu.make_async_copy(xbuf.at[0], y_hbm.at[pl.ds(pl.multiple_of(b * blk, blk), blk)], zsem)

        def each(fn):
            def body(b, carry):
                fn(fill(b))
                return carry
            lax.fori_loop(nact_ref[0], nb, body, 0)

        each(lambda cp: cp.start())
        each(lambda cp: cp.wait())


def _expert_ffn(buf, wgu, wdn, layer, tables):
    n_rows, half = buf.shape
    d = 2 * half
    blk = ROW_BLOCK
    nb = n_rows // blk
    f2 = wgu.shape[3]
    f = wdn.shape[2]
    grid_spec = pltpu.PrefetchScalarGridSpec(
        num_scalar_prefetch=3,
        grid=(N_EXPERTS,),
        in_specs=[pl.BlockSpec(memory_space=pl.ANY),
                  pl.BlockSpec((1, 1, d, f2), lambda e, *_: (layer, e, 0, 0)),
                  pl.BlockSpec((1, 1, f, d), lambda e, *_: (layer, e, 0, 0))],
        out_specs=pl.BlockSpec(memory_space=pl.ANY),
        scratch_shapes=[pltpu.VMEM((d, f2), BF16), pltpu.VMEM((f, d), BF16),
                        pltpu.VMEM((2, blk, half), jnp.uint32), pltpu.VMEM((2, blk, half), jnp.uint32),
                        pltpu.SemaphoreType.DMA((2,)), pltpu.SemaphoreType.DMA((2,)), pltpu.SemaphoreType.DMA],
    )
    return pl.pallas_call(
        functools.partial(_ffn_kernel, nb=nb, blk=blk),
        grid_spec=grid_spec,
        out_shape=jax.ShapeDtypeStruct((n_rows, half), jnp.uint32),
        compiler_params=_params("arbitrary"),
        name="expert_ffn",
    )(tables["first_block"], tables["n_blocks"], tables["n_act"], buf, wgu, wdn)


def _store_class_views(xt, scr, view_refs, views):
    tm, d = xt.shape
    if views:
        for j in range(d // 128):
            scr[j] = xt[:, j * 128:(j + 1) * 128]
    for ref, r in zip(view_refs, views):
        for cls in range(r):
            for j in range(d // 128):
                lanes = slice(cls * d + j * 128, cls * d + (j + 1) * 128)
                ref[:, lanes] = scr[j, pl.ds(cls, tm // r, stride=r), :].astype(BF16)


def _views_kernel(x_ref, xb_ref, *rest, views):
    view_refs, scr = rest[:len(views)], rest[len(views)]
    xt = x_ref[...]
    xb_ref[...] = xt.astype(BF16)
    _store_class_views(xt, scr, view_refs, views)


def _input_views(xf, views):
    n, d = xf.shape
    tm = TOKEN_TILE
    tok = pl.BlockSpec((tm, d), lambda t: (t, 0))
    return pl.pallas_call(
        functools.partial(_views_kernel, views=views),
        grid=(n // tm,),
        in_specs=[tok],
        out_specs=[tok] + [pl.BlockSpec((tm // r, r * d), lambda t: (t, 0)) for r in views],
        out_shape=[jax.ShapeDtypeStruct((n, d), BF16)]
        + [jax.ShapeDtypeStruct((n // r, r * d), BF16) for r in views],
        scratch_shapes=[pltpu.VMEM((d // 128, tm, 128), F32)],
        compiler_params=_params("arbitrary"),
        name="input_views",
    )(xf)


def _merge_kernel(dst_ref, off_ref, len_ref, tot_ref, y_hbm, x1_ref, pos_ref, gt_ref, g_ref, b_ref,
                  x2_ref, x2b_ref, *rest, nt, alpha, views):
    view_refs, (ys, x2_scr, sem) = rest[:len(views)], rest[len(views):]
    t = pl.program_id(0)
    cur = t % 2
    tm, d = x1_ref.shape
    srows = ys.shape[1]

    def fetch(tile, buf):
        for e in range(N_EXPERTS):
            k = tile * N_EXPERTS + e
            n = pl.multiple_of(len_ref[k], SEG_ALIGN)
            dst = pl.multiple_of(off_ref[k], SEG_ALIGN)
            src = pl.multiple_of(dst_ref[k], SEG_ALIGN)

            @pl.when(n > 0)
            def _():
                pltpu.make_async_copy(y_hbm.at[pl.ds(src, n)], ys.at[buf, pl.ds(dst, n)], sem.at[buf]).start()

    @pl.when(t == 0)
    def _():
        ys[...] = jnp.zeros_like(ys)
        fetch(t, cur)

    @pl.when(t + 1 < nt)
    def _():
        fetch(t + 1, 1 - cur)

    n_all = pl.multiple_of(tot_ref[t], SEG_ALIGN)
    pltpu.make_async_copy(y_hbm.at[pl.ds(0, n_all)], ys.at[cur, pl.ds(0, n_all)], sem.at[cur]).wait()

    y_lo, y_hi = _unpack_halves(ys[cur])
    pos = pos_ref[...]
    gt = gt_ref[...]
    lane = lax.broadcasted_iota(jnp.int32, (tm, srows), 1)
    sel = (jnp.where(lane == pos[:, 0:1], gt[:, 0:1], 0.0)
           + jnp.where(lane == pos[:, 1:2], gt[:, 1:2], 0.0)).astype(BF16)
    m = jnp.concatenate([jnp.dot(sel, y_lo, preferred_element_type=F32),
                         jnp.dot(sel, y_hi, preferred_element_type=F32)], axis=1)
    x2 = _layer_norm(alpha * x1_ref[...] + m, g_ref[...], b_ref[...])
    x2_ref[...] = x2
    x2b_ref[...] = x2.astype(BF16)
    _store_class_views(x2, x2_scr, view_refs, views)


def _expert_merge(y_buf, pos_t, gate_t, x1, g, b, alpha, tables, views):
    n, d = x1.shape
    tm = TOKEN_TILE
    nt = n // tm
    tok = pl.BlockSpec((tm, d), lambda t, *_: (t, 0))
    pair = pl.BlockSpec((tm, 2), lambda t, *_: (t, 0))
    vec = pl.BlockSpec((1, d), lambda t, *_: (0, 0))
    grid_spec = pltpu.PrefetchScalarGridSpec(
        num_scalar_prefetch=4,
        grid=(nt,),
        in_specs=[pl.BlockSpec(memory_space=pl.ANY), tok, pair, pair, vec, vec],
        out_specs=[tok, tok] + [pl.BlockSpec((tm // r, r * d), lambda t, *_: (t, 0)) for r in views],
        scratch_shapes=[pltpu.VMEM((2, SORTED_ROWS, d // 2), jnp.uint32), pltpu.VMEM((d // 128, tm, 128), F32),
                        pltpu.SemaphoreType.DMA((2,))],
    )
    return pl.pallas_call(
        functools.partial(_merge_kernel, nt=nt, alpha=alpha, views=views),
        grid_spec=grid_spec,
        out_shape=[jax.ShapeDtypeStruct((n, d), F32), jax.ShapeDtypeStruct((n, d), BF16)]
        + [jax.ShapeDtypeStruct((n // r, r * d), BF16) for r in views],
        compiler_params=_params("arbitrary"),
        name="expert_merge",
    )(tables["dst"], tables["off"], tables["len"], tables["tot"], y_buf, x1, pos_t, gate_t, g, b)


def _routing_tables(counts):
    blk = ROW_BLOCK
    nt = counts.shape[0]
    seg = (counts + SEG_ALIGN - 1) // SEG_ALIGN * SEG_ALIGN
    off = jnp.cumsum(seg, axis=1) - seg
    used = jnp.sum(seg, axis=0)
    region = (used + blk - 1) // blk * blk
    region_end = jnp.cumsum(region)
    region_start = region_end - region
    dst = region_start[None, :] + jnp.cumsum(seg, axis=0) - seg
    n_act = region_end[-1] // blk
    i32 = lambda v: v.astype(jnp.int32)
    return dict(dst=i32(dst.reshape(-1)), off=i32(off.reshape(-1)), len=i32(seg.reshape(-1)),
                tot=i32(jnp.sum(seg, axis=1)), fill_row=i32(region_start + used), fill_len=i32(region - used),
                n_act=i32(n_act.reshape(1)), first_block=i32(region_start // blk), n_blocks=i32(region // blk))


def kernel(x, w_in, w_attn_out, w_conv_out, w_o, conv_w, conv_b, conv_ln_g, conv_ln_b, ln_mix_g, ln_mix_b,
           expert_w_gate_up, expert_w_down, ln_ffn_g, ln_ffn_b, router_w, rel_bias):
    batch, seq, d = x.shape
    depth = w_in.shape[0]
    n = batch * seq
    aw, gw = ATTN_WIDTH, GROUP_WIDTH
    assert d % 256 == 0 and n % 1024 == 0
    for window, r in DILATION_GROUPS:
        assert window // r == BAND and seq % (r * BAND) == 0
    alpha = (2 * depth) ** 0.25
    scale = HEAD_DIM ** -0.5

    biases = [_band_bias(rel_bias, gi, r) for gi, (_, r) in enumerate(DILATION_GROUPS)]
    rwt = router_w.T.astype(BF16)
    nt = n // TOKEN_TILE
    nb = -(-(2 * n + N_EXPERTS * (nt * (SEG_ALIGN - 1) + ROW_BLOCK - 1)) // ROW_BLOCK)
    dilations = tuple(r for _, r in DILATION_GROUPS if r > 1)

    xf = x.reshape(n, d)
    xb, *xv = _input_views(xf, dilations)
    xviews = dict(zip(dilations, xv))
    for l in range(depth):
        wl = w_in[l]

        def qkv_cols(gi):
            return jnp.concatenate([wl[:, gi * gw:(gi + 1) * gw] * scale,
                                    wl[:, aw + gi * gw:aw + (gi + 1) * gw],
                                    wl[:, 2 * aw + gi * gw:2 * aw + (gi + 1) * gw]], axis=1)

        w0 = jnp.concatenate([wl[:, 3 * aw + d:], wl[:, 3 * aw:3 * aw + d], qkv_cols(0)], axis=1).astype(BF16)
        proj0 = _project(xb, w0, 512)
        outs, lses = [], []
        for gi, (_, r) in enumerate(DILATION_GROUPS):
            if r == 1:
                qkv, col0 = proj0.reshape(batch, seq, proj0.shape[1]), (3 * d) // gw
            else:
                qkv = _project_dilated(xviews[r], qkv_cols(gi).astype(BF16), batch, seq, r)
                qkv, col0 = qkv.reshape(batch * r, seq // r, 3 * gw), 0
            o, lse = _attention(qkv, biases[gi], batch, seq, r, col0)
            outs.append(o)
            lses.append(lse)

        lw = dict(wao=w_attn_out[l].astype(BF16), cw=conv_w[l], cb=conv_b[l][None], clg=conv_ln_g[l][None],
                  clb=conv_ln_b[l][None], wco=w_conv_out[l].astype(BF16), wo=w_o[l].astype(BF16),
                  lng=ln_mix_g[l][None], lnb=ln_mix_b[l][None], rwt=rwt, alpha=alpha)
        x1, x1b, gate, pos, cnt = _mixer_tail(outs, lses, proj0, xf, lw, batch, seq)

        tables = _routing_tables(cnt[:, :, 0].astype(jnp.int32))
        buf = _dispatch(x1b, pos, tables, nb)
        y_buf = _expert_ffn(buf, expert_w_gate_up, expert_w_down, l, tables)
        views = dilations if l + 1 < depth else ()
        xf, xb, *xv = _expert_merge(y_buf, pos.T, gate.T, x1, ln_ffn_g[l][None], ln_ffn_b[l][None], alpha,
                                    tables, views)
        xviews = dict(zip(views, xv))
    return xf.reshape(batch, seq, d)
```

```python
import functools
import math

import jax
import jax.numpy as jnp
from jax import lax
from jax.experimental import pallas as pl
from jax.experimental.pallas import tpu as pltpu

F32 = jnp.float32
BF16 = jnp.bfloat16

HEAD_DIM = 64
HEADS_PER_GROUP = 4
GROUP_WIDTH = HEADS_PER_GROUP * HEAD_DIM
DILATION_GROUPS = ((128, 1), (512, 4), (2048, 16))
N_GROUPS = len(DILATION_GROUPS)
ATTN_WIDTH = N_GROUPS * GROUP_WIDTH
BAND = 128
CONV_KERNEL = 31
CONV_HALO = 32
CONV_TAIL = 16
N_BUCKETS = 32
MAX_DISTANCE = 2048
N_EXPERTS = 16
EXPERTS_PER_GROUP = 4
LN_EPS = 1e-5
MASKED = -1e30

ROW_DMA_PRIORITY = 1
PROJ_ROWS = 2048
ROW_BLOCK = 256
TOKEN_TILE = 256
SEG_ALIGN = 8
SORTED_ROWS = -(-(2 * TOKEN_TILE + N_EXPERTS * (SEG_ALIGN - 1)) // 128) * 128
VMEM_LIMIT = 56 * 1024 * 1024


def _sigmoid(v):
    return 1.0 / (1.0 + jnp.exp(-v))


def _layer_norm(z, g, b):
    mu = jnp.mean(z, axis=-1, keepdims=True)
    zc = z - mu
    var = jnp.mean(zc * zc, axis=-1, keepdims=True)
    return zc * lax.rsqrt(var + LN_EPS) * g + b


def _params(*sem):
    return pltpu.CompilerParams(dimension_semantics=sem, vmem_limit_bytes=VMEM_LIMIT)


def _mm_kernel(x_ref, w_ref, o_ref):
    o_ref[...] = jnp.dot(x_ref[...], w_ref[...], preferred_element_type=F32).astype(o_ref.dtype)


def _project(xb, w, tm):
    n, d = xb.shape
    c = w.shape[1]
    return pl.pallas_call(
        _mm_kernel,
        grid=(n // tm,),
        in_specs=[pl.BlockSpec((tm, d), lambda i: (i, 0)),
                  pl.BlockSpec((d, c), lambda i: (0, 0))],
        out_specs=pl.BlockSpec((tm, c), lambda i: (i, 0)),
        out_shape=jax.ShapeDtypeStruct((n, c), BF16),
        compiler_params=_params("arbitrary"),
        name="proj_natural",
    )(xb, w)


def _project_dilated(xv, w, batch, seq, r):
    d, c = w.shape
    n = xv.shape[0] * r
    sub = seq // r
    tm = min(PROJ_ROWS, sub)
    nl = sub // tm
    cps = max(1, min(r, PROJ_ROWS // sub))
    assert r % cps == 0

    def mm_classes(x_ref, w_ref, o_ref):
        for k in range(cps):
            o_ref[k * tm:(k + 1) * tm, :] = jnp.dot(x_ref[:, k * d:(k + 1) * d], w_ref[...],
                                                    preferred_element_type=F32).astype(o_ref.dtype)

    return pl.pallas_call(
        mm_classes,
        grid=(batch, r // cps, nl),
        in_specs=[pl.BlockSpec((tm, cps * d), lambda b, cg, j: (b * nl + j, cg)),
                  pl.BlockSpec((d, c), lambda b, cg, j: (0, 0))],
        out_specs=pl.BlockSpec((cps * tm, c), lambda b, cg, j: ((b * (r // cps) + cg) * nl + j, 0)),
        out_shape=jax.ShapeDtypeStruct((n, c), BF16),
        compiler_params=_params("arbitrary", "arbitrary", "arbitrary"),
        name=f"proj_dilated_{r}",
    )(xv, w)


def _attn_kernel(q_ref, kp_ref, vp_ref, kc_ref, vc_ref, bias_ref, o_ref, lse_ref, k_scr, v_scr, *, nqb):
    i = pl.program_id(1)
    k_scr[0:BAND, :] = kp_ref[0]
    k_scr[BAND:, :] = kc_ref[0]
    v_scr[0:BAND, :] = vp_ref[0]
    v_scr[BAND:, :] = vc_ref[0]
    col = lax.broadcasted_iota(jnp.int32, (BAND, 2 * BAND), 1)
    first_head = lax.broadcasted_iota(jnp.int32, (BAND, 2 * HEAD_DIM), 1) < HEAD_DIM

    def body(j, carry):
        r0 = pl.multiple_of(j * BAND, BAND)
        q = q_ref[0, pl.ds(r0, BAND), :]
        kk = k_scr[pl.ds(r0, 2 * BAND), :]
        vv = v_scr[pl.ds(r0, 2 * BAND), :]
        no_prev = jnp.logical_and(jnp.logical_and(i == 0, j == 0), col < BAND)
        outs, lses = [], []
        for pair in range(HEADS_PER_GROUP // 2):
            sl = slice(pair * 2 * HEAD_DIM, (pair + 1) * 2 * HEAD_DIM)
            q2, k2, v2 = q[:, sl], kk[:, sl], vv[:, sl]
            o_pair, lse_pair = None, None
            for hh in range(2):
                mine = first_head if hh == 0 else jnp.logical_not(first_head)
                qm = jnp.where(mine, q2, jnp.zeros_like(q2))
                s = lax.dot_general(qm, k2, (((1,), (1,)), ((), ())), preferred_element_type=F32)
                s = s + bias_ref[2 * pair + hh]
                s = jnp.where(no_prev, MASKED, s)
                m = jnp.max(s, axis=-1, keepdims=True)
                p = jnp.exp(s - m)
                den = jnp.sum(p, axis=-1, keepdims=True)
                o = jnp.dot(p.astype(BF16), v2, preferred_element_type=F32) * (1.0 / den)
                lse = jnp.broadcast_to(m + jnp.log(den), (BAND, 2 * HEAD_DIM))
                o_pair = o if hh == 0 else jnp.where(first_head, o_pair, o)
                lse_pair = lse if hh == 0 else jnp.where(first_head, lse_pair, lse)
            outs.append(o_pair)
            lses.append(lse_pair)
        o_ref[0, pl.ds(r0, BAND), :] = jnp.concatenate(outs, axis=1).astype(o_ref.dtype)
        lse_ref[0, pl.ds(r0, BAND), :] = jnp.concatenate(lses, axis=1)
        return carry

    lax.fori_loop(0, nqb, body, 0, unroll=math.gcd(nqb, 8))


def _attention(qkv, bias, batch, seq, r, col0):
    sub = seq // r
    nqb = min(8, sub // BAND)
    rows = nqb * BAND
    nt = sub // rows
    gw = GROUP_WIDTH

    def cur(c):
        return pl.BlockSpec((1, rows, gw), lambda bc, i: (bc, i, c))

    def prev(c):
        return pl.BlockSpec((1, BAND, gw), lambda bc, i: (bc, jnp.maximum(i * nqb - 1, 0), c))

    out_spec = pl.BlockSpec((1, rows, gw), lambda bc, i: (bc // r, i, bc % r))
    o, lse = pl.pallas_call(
        functools.partial(_attn_kernel, nqb=nqb),
        grid=(batch * r, nt),
        in_specs=[cur(col0), prev(col0 + 1), prev(col0 + 2), cur(col0 + 1), cur(col0 + 2),
                  pl.BlockSpec((HEADS_PER_GROUP, BAND, 2 * BAND), lambda bc, i: (0, 0, 0))],
        out_specs=[out_spec, out_spec],
        out_shape=[jax.ShapeDtypeStruct((batch, sub, r * gw), BF16),
                   jax.ShapeDtypeStruct((batch, sub, r * gw), F32)],
        scratch_shapes=[pltpu.VMEM((rows + BAND, gw), BF16), pltpu.VMEM((rows + BAND, gw), BF16)],
        compiler_params=_params("arbitrary", "arbitrary"),
        name=f"attn_dilation_{r}",
    )(qkv, qkv, qkv, qkv, qkv, bias)
    return o.reshape(batch * sub, r * gw), lse.reshape(batch * sub, r * gw)


def _t5_bucket(dist):
    max_exact = N_BUCKETS // 2
    n = jnp.maximum(dist, 0)
    nf = jnp.maximum(n, 1).astype(F32)
    large = max_exact + (jnp.log(nf / max_exact) / math.log(MAX_DISTANCE / max_exact)
                         * (N_BUCKETS - max_exact)).astype(jnp.int32)
    large = jnp.minimum(large, N_BUCKETS - 1)
    return jnp.where(n < max_exact, n, large)


def _band_bias(rel_bias, gi, r):
    qi = jnp.arange(BAND)[:, None]
    kj = jnp.arange(2 * BAND)[None, :]
    dist = qi + BAND - kj
    bucket = _t5_bucket(dist * r)
    valid = (dist >= 0) & (dist <= BAND)
    b = jnp.full((HEADS_PER_GROUP, BAND, 2 * BAND), MASKED, F32)
    for k in range(N_BUCKETS):
        row = rel_bias[k, gi * HEADS_PER_GROUP:(gi + 1) * HEADS_PER_GROUP].astype(F32)
        b = jnp.where(((bucket == k) & valid)[None], row[:, None, None], b)
    return b


def _post_kernel(o0_ref, o1_ref, o2_ref, l0_ref, l1_ref, l2_ref, u_ref, uh_ref, g_ref, x_ref,
                 wao_ref, cw_ref, cb_ref, clg_ref, clb_ref, wco_ref, wo_ref, lng_ref, lnb_ref, rwt_ref,
                 x1_ref, x1b_ref, gate_ref, pos_ref, cnt_ref,
                 hs_scr, conv_scr, ys_scr, nat_scr, *, tm, seq, alpha):
    i = pl.program_id(0)
    d = x_ref.shape[1]
    c = d // 2
    gw = GROUP_WIDTH

    def natural(ref, slot, r):
        if r == 1:
            return ref[...].astype(F32)
        nl = gw // 128
        for cls in range(r):
            for j in range(nl):
                lanes = slice(cls * gw + j * 128, cls * gw + (j + 1) * 128)
                nat_scr[slot * nl + j, pl.ds(cls, tm // r, stride=r), :] = ref[:, lanes].astype(F32)
        return jnp.concatenate([nat_scr[slot * nl + j] for j in range(nl)], axis=1)

    rs = [r for _, r in DILATION_GROUPS]
    o0, o1, o2 = (natural(ref, s, r) for s, (ref, r) in enumerate(zip((o0_ref, o1_ref, o2_ref), rs)))
    l0, l1, l2 = (natural(ref, 3 + s, r) for s, (ref, r) in enumerate(zip((l0_ref, l1_ref, l2_ref), rs)))
    mx = jnp.maximum(jnp.maximum(l0, l1), l2)
    e0, e1, e2 = jnp.exp(l0 - mx), jnp.exp(l1 - mx), jnp.exp(l2 - mx)
    attn = (e0 * o0 + e1 * o1 + e2 * o2) * (1.0 / (e0 + e1 + e2))
    attn_branch = jnp.dot(attn.astype(BF16), wao_ref[...], preferred_element_type=F32)

    u = u_ref[...].astype(F32)
    uh = uh_ref[...].astype(F32)
    hh = uh[:, :c] * _sigmoid(uh[:, c:])
    seq_start = (i * tm) % seq == 0
    hs_scr[0:CONV_HALO, :] = jnp.where(seq_start, 0.0, hh)
    hs_scr[CONV_HALO:CONV_HALO + tm, :] = u[:, :c] * _sigmoid(u[:, c:])
    hs_scr[CONV_HALO + tm:, :] = jnp.zeros((CONV_TAIL, c), F32)
    off = CONV_HALO - (CONV_KERNEL - 1)
    rows_y = tm + 8
    for cc in range(c // 128):
        ls = slice(cc * 128, (cc + 1) * 128)
        acc = None
        for s in range(8):
            y = None
            for a in range((off + CONV_KERNEL + 7) // 8):
                k = 8 * a + s - off
                if 0 <= k < CONV_KERNEL:
                    term = hs_scr[8 * a:8 * a + rows_y, ls] * cw_ref[k:k + 1, ls]
                    y = term if y is None else y + term
            if s == 0:
                acc = y[0:tm]
            else:
                ys_scr[s] = y
                acc = acc + ys_scr[s, s:s + tm, :]
        conv_scr[:, ls] = acc
    hc = _layer_norm(conv_scr[...] + cb_ref[...], clg_ref[...], clb_ref[...])
    hc = hc * _sigmoid(hc)
    conv_branch = jnp.dot(hc.astype(BF16), wco_ref[...], preferred_element_type=F32)

    gates = _sigmoid(g_ref[...].astype(F32))
    merged = gates[:, :d] * attn_branch + gates[:, d:] * conv_branch
    hmix = jnp.dot(merged.astype(BF16), wo_ref[...], preferred_element_type=F32)
    x1 = _layer_norm(alpha * x_ref[...] + hmix, lng_ref[...], lnb_ref[...])
    x1_ref[...] = x1
    x1b = x1.astype(BF16)
    x1b_ref[...] = x1b

    logits = lax.dot_general(rwt_ref[...], x1b, (((1,), (1,)), ((), ())), preferred_element_type=F32)
    ex = jnp.exp(logits - jnp.max(logits, axis=0, keepdims=True))
    rows = [ex[e:e + 1, :] for e in range(N_EXPERTS)]
    best, gsel = None, None
    for g in range(N_EXPERTS // EXPERTS_PER_GROUP):
        v = rows[g * EXPERTS_PER_GROUP:(g + 1) * EXPERTS_PER_GROUP]
        score = None
        for a in range(EXPERTS_PER_GROUP):
            for b in range(a + 1, EXPERTS_PER_GROUP):
                ps = v[a] + v[b]
                score = ps if score is None else jnp.maximum(score, ps)
        if best is None:
            best, gsel = score, jnp.zeros_like(score, dtype=jnp.int32)
        else:
            upd = score > best
            gsel = jnp.where(upd, g, gsel)
            best = jnp.where(upd, score, best)
    vals = []
    for j in range(EXPERTS_PER_GROUP):
        vj = rows[j]
        for g in range(1, N_EXPERTS // EXPERTS_PER_GROUP):
            vj = jnp.where(gsel == g, rows[g * EXPERTS_PER_GROUP + j], vj)
        vals.append(vj)
    v1, i1 = vals[0], jnp.zeros_like(gsel)
    for j in range(1, EXPERTS_PER_GROUP):
        upd = vals[j] > v1
        i1 = jnp.where(upd, j, i1)
        v1 = jnp.where(upd, vals[j], v1)
    v2, i2 = jnp.full_like(v1, -1.0), jnp.zeros_like(gsel)
    for j in range(EXPERTS_PER_GROUP):
        upd = jnp.logical_and(i1 != j, vals[j] > v2)
        i2 = jnp.where(upd, j, i2)
        v2 = jnp.where(upd, vals[j], v2)
    ea = gsel * EXPERTS_PER_GROUP + i1
    eb = gsel * EXPERTS_PER_GROUP + i2
    inv = 1.0 / (v1 + v2)
    gate_ref[...] = jnp.concatenate([v1 * inv, v2 * inv], axis=0)

    eid = lax.broadcasted_iota(jnp.int32, (N_EXPERTS, tm), 0)
    hit_a = eid == ea
    hit_b = eid == eb
    onehot = jnp.where(jnp.logical_or(hit_a, hit_b), 1.0, 0.0)
    before = (lax.broadcasted_iota(jnp.int32, (tm, tm), 0)
              < lax.broadcasted_iota(jnp.int32, (tm, tm), 1))
    upper = jnp.where(before, 1.0, 0.0).astype(BF16)
    rank = jnp.dot(onehot.astype(BF16), upper, preferred_element_type=F32)
    count = jnp.sum(onehot, axis=1, keepdims=True)
    seg_len = jnp.floor((count + (SEG_ALIGN - 1)) * (1.0 / SEG_ALIGN)) * SEG_ALIGN
    lower = (lax.broadcasted_iota(jnp.int32, (N_EXPERTS, N_EXPERTS), 1)
             < lax.broadcasted_iota(jnp.int32, (N_EXPERTS, N_EXPERTS), 0))
    seg_off = jnp.dot(jnp.where(lower, 1.0, 0.0).astype(BF16),
                      jnp.broadcast_to(seg_len, (N_EXPERTS, 128)).astype(BF16),
                      preferred_element_type=F32)[:, 0:1]
    row = rank + seg_off
    pa = jnp.sum(jnp.where(hit_a, row, 0.0), axis=0, keepdims=True)
    pb = jnp.sum(jnp.where(hit_b, row, 0.0), axis=0, keepdims=True)
    pos_ref[...] = jnp.concatenate([pa, pb], axis=0).astype(jnp.int32)
    cnt_ref[0] = jnp.broadcast_to(count, (N_EXPERTS, 128))


def _mixer_tail(o, lse, proj0, x, lw, batch, seq):
    n, d = x.shape
    tm = TOKEN_TILE
    nt = n // tm
    gw = GROUP_WIDTH
    hb = tm // CONV_HALO

    def tok(width):
        return pl.BlockSpec((tm, width), lambda i: (i, 0))

    def const(shape):
        return pl.BlockSpec(shape, lambda i: (0,) * len(shape))

    def grouped(r):
        return pl.BlockSpec((tm // r, r * gw), lambda i: (i, 0))

    rs = [r for _, r in DILATION_GROUPS]
    in_specs = (
        [grouped(r) for r in rs] * 2
        + [pl.BlockSpec((tm, d), lambda i: (i, 2)),
           pl.BlockSpec((CONV_HALO, d), lambda i: (jnp.maximum(i * hb - 1, 0), 2)),
           pl.BlockSpec((tm, 2 * d), lambda i: (i, 0)),
           tok(d)]
        + [const(lw[k].shape) for k in ("wao", "cw", "cb", "clg", "clb", "wco", "wo", "lng", "lnb", "rwt")]
    )
    lane_row = pl.BlockSpec((2, tm), lambda i: (0, i))
    out_specs = [tok(d), tok(d), lane_row, lane_row,
                 pl.BlockSpec((1, N_EXPERTS, 128), lambda i: (i, 0, 0))]
    out_shape = [jax.ShapeDtypeStruct((n, d), F32), jax.ShapeDtypeStruct((n, d), BF16),
                 jax.ShapeDtypeStruct((2, n), F32), jax.ShapeDtypeStruct((2, n), jnp.int32),
                 jax.ShapeDtypeStruct((nt, N_EXPERTS, 128), F32)]
    alpha = lw["alpha"]
    return pl.pallas_call(
        functools.partial(_post_kernel, tm=tm, seq=seq, alpha=alpha),
        grid=(nt,),
        in_specs=in_specs,
        out_specs=out_specs,
        out_shape=out_shape,
        scratch_shapes=[pltpu.VMEM((CONV_HALO + tm + CONV_TAIL, d // 2), F32), pltpu.VMEM((tm, d // 2), F32),
                        pltpu.VMEM((8, tm + 8, 128), F32),
                        pltpu.VMEM((2 * N_GROUPS * (gw // 128), tm, 128), F32)],
        compiler_params=_params("arbitrary"),
        name="mixer_tail",
    )(o[0], o[1], o[2], lse[0], lse[1], lse[2], proj0, proj0, proj0, x,
      *[lw[k] for k in ("wao", "cw", "cb", "clg", "clb", "wco", "wo", "lng", "lnb", "rwt")])


def _pack_halves(v):
    h = v.shape[1] // 2
    bits = lax.bitcast_convert_type(v, jnp.uint32)
    return (bits[:, :h] >> 16) | (bits[:, h:] & jnp.uint32(0xFFFF0000))


def _unpack_halves(w):
    lo = lax.bitcast_convert_type(w << 16, F32).astype(BF16)
    hi = lax.bitcast_convert_type(w & jnp.uint32(0xFFFF0000), F32).astype(BF16)
    return lo, hi


def _dispatch_kernel(dst_ref, off_ref, len_ref, tot_ref, fill_row_ref, fill_len_ref, nact_ref,
                     x_ref, pos_ref, buf_hbm, xs, zeros, sem, zsem, *, nt, nb, blk):
    t = pl.program_id(0)
    cur = t % 2
    srows = xs.shape[1]

    def tile_wait(tile, buf):
        n = pl.multiple_of(tot_ref[tile], SEG_ALIGN)
        pltpu.make_async_copy(xs.at[buf, pl.ds(0, n)], buf_hbm.at[pl.ds(0, n)], sem.at[buf]).wait()

    @pl.when(t >= 2)
    def _():
        tile_wait(t - 2, cur)

    pos = pos_ref[...]
    row = lax.broadcasted_iota(jnp.int32, (srows, pos.shape[1]), 0)
    hit = jnp.logical_or(row == pos[0:1, :], row == pos[1:2, :])
    sel = jnp.where(hit, 1.0, 0.0).astype(BF16)
    xs[cur] = _pack_halves(jnp.dot(sel, x_ref[...], preferred_element_type=F32))

    for e in range(N_EXPERTS):
        k = t * N_EXPERTS + e
        n = pl.multiple_of(len_ref[k], SEG_ALIGN)
        src = pl.multiple_of(off_ref[k], SEG_ALIGN)
        dst = pl.multiple_of(dst_ref[k], SEG_ALIGN)

        @pl.when(n > 0)
        def _():
            pltpu.make_async_copy(xs.at[cur, pl.ds(src, n)], buf_hbm.at[pl.ds(dst, n)], sem.at[cur]).start()

    @pl.when(t == nt - 1)
    def _():
        if nt >= 2:
            tile_wait(t - 1, 1 - cur)
        tile_wait(t, cur)
        zeros[...] = jnp.zeros_like(zeros)

        def region_fill(e):
            n = pl.multiple_of(fill_len_ref[e], SEG_ALIGN)
            dst = pl.multiple_of(fill_row_ref[e], SEG_ALIGN)
            return n, pltpu.make_async_copy(zeros.at[pl.ds(0, n)], buf_hbm.at[pl.ds(dst, n)], zsem)

        def block_fill(b):
            return pltpu.make_async_copy(zeros, buf_hbm.at[pl.ds(pl.multiple_of(b * blk, blk), blk)], zsem)

        def each_block(fn):
            def body(b, carry):
                fn(block_fill(b))
                return carry
            lax.fori_loop(nact_ref[0], nb, body, 0)

        for e in range(N_EXPERTS):
            n, cp = region_fill(e)
            pl.when(n > 0)(cp.start)
        each_block(lambda cp: cp.start())
        for e in range(N_EXPERTS):
            n, cp = region_fill(e)
            pl.when(n > 0)(cp.wait)
        each_block(lambda cp: cp.wait())


def _dispatch(x1b, pos, tables, nb):
    n, d = x1b.shape
    tm = TOKEN_TILE
    nt = n // tm
    blk = ROW_BLOCK
    grid_spec = pltpu.PrefetchScalarGridSpec(
        num_scalar_prefetch=7,
        grid=(nt,),
        in_specs=[pl.BlockSpec((tm, d), lambda t, *_: (t, 0)),
                  pl.BlockSpec((2, tm), lambda t, *_: (0, t))],
        out_specs=pl.BlockSpec(memory_space=pl.ANY),
        scratch_shapes=[pltpu.VMEM((2, SORTED_ROWS, d // 2), jnp.uint32), pltpu.VMEM((blk, d // 2), jnp.uint32),
                        pltpu.SemaphoreType.DMA((2,)), pltpu.SemaphoreType.DMA],
    )
    return pl.pallas_call(
        functools.partial(_dispatch_kernel, nt=nt, nb=nb, blk=blk),
        grid_spec=grid_spec,
        out_shape=jax.ShapeDtypeStruct((nb * blk, d // 2), jnp.uint32),
        compiler_params=_params("arbitrary"),
        name="expert_dispatch",
    )(tables["dst"], tables["off"], tables["len"], tables["tot"], tables["fill_row"], tables["fill_len"],
      tables["n_act"], x1b, pos)


def _ffn_kernel(first_ref, count_ref, nact_ref, xs_hbm, wgu_ref, wdn_ref, y_hbm,
                wgu_b, wdn_b, xbuf, obuf, h_scr, isem, osem, zsem, *, nb, blk):
    e = pl.program_id(0)
    f = wdn_ref.shape[2]
    half = xbuf.shape[2]
    b0 = first_ref[e]
    n_blocks = count_ref[e]

    def rows(j):
        return pl.ds(pl.multiple_of((b0 + j) * blk, blk), blk)

    def load(j, buf):
        return pltpu.make_async_copy(xs_hbm.at[rows(j)], xbuf.at[buf], isem.at[buf])

    def store(j, buf):
        return pltpu.make_async_copy(obuf.at[buf], y_hbm.at[rows(j)], osem.at[buf])

    def prefetch(j):
        @pl.when(j < n_blocks)
        def _():
            load(j, j % 2).start(priority=ROW_DMA_PRIORITY)

    def up(j):
        buf = j % 2
        x_lo, x_hi = _unpack_halves(xbuf[buf])
        h_scr[buf] = (jnp.dot(x_lo, wgu_b[0:half, :], preferred_element_type=F32)
                      + jnp.dot(x_hi, wgu_b[half:, :], preferred_element_type=F32))

    def release(j):
        @pl.when(j >= 2)
        def _():
            store(j - 2, j % 2).wait()

    def down(j):
        buf = j % 2
        a = h_scr[buf, :, 0:f]
        act = (a * _sigmoid(a) * h_scr[buf, :, f:]).astype(BF16)
        y = jnp.dot(act, wdn_b[...], preferred_element_type=F32)
        obuf[buf] = _pack_halves(y.astype(BF16).astype(F32))
        store(j, buf).start(priority=ROW_DMA_PRIORITY)

    @pl.when(n_blocks > 0)
    def _():
        load(0, 0).start(priority=ROW_DMA_PRIORITY)
        wgu_b[...] = wgu_ref[0, 0].astype(BF16)
        wdn_b[...] = wdn_ref[0, 0].astype(BF16)
        prefetch(1)
        load(0, 0).wait()
        up(0)

    def block(j, carry):
        prefetch(j + 1)
        release(j - 1)
        load(j, j % 2).wait()
        down(j - 1)
        up(j)
        return carry

    lax.fori_loop(1, n_blocks, block, 0)

    @pl.when(n_blocks > 0)
    def _():
        release(n_blocks - 1)
        down(n_blocks - 1)

    @pl.when(n_blocks >= 2)
    def _():
        store(n_blocks - 2, n_blocks % 2).wait()

    @pl.when(n_blocks >= 1)
    def _():
        store(n_blocks - 1, (n_blocks - 1) % 2).wait()

    @pl.when(e == pl.num_programs(0) - 1)
    def _():
        xbuf[0] = jnp.zeros((blk, half), jnp.uint32)

        def fill(b):
            return pltpu.make_async_copy(xbuf.at[0], y_hbm.at[pl.ds(pl.multiple_of(b * blk, blk), blk)], zsem)

        def each(fn):
            def body(b, carry):
                fn(fill(b))
                return carry
            lax.fori_loop(nact_ref[0], nb, body, 0)

        each(lambda cp: cp.start())
        each(lambda cp: cp.wait())


def _expert_ffn(buf, wgu, wdn, layer, tables):
    n_rows, half = buf.shape
    d = 2 * half
    blk = ROW_BLOCK
    nb = n_rows // blk
    f2 = wgu.shape[3]
    f = wdn.shape[2]
    grid_spec = pltpu.PrefetchScalarGridSpec(
        num_scalar_prefetch=3,
        grid=(N_EXPERTS,),
        in_specs=[pl.BlockSpec(memory_space=pl.ANY),
                  pl.BlockSpec((1, 1, d, f2), lambda e, *_: (layer, e, 0, 0)),
                  pl.BlockSpec((1, 1, f, d), lambda e, *_: (layer, e, 0, 0))],
        out_specs=pl.BlockSpec(memory_space=pl.ANY),
        scratch_shapes=[pltpu.VMEM((d, f2), BF16), pltpu.VMEM((f, d), BF16),
                        pltpu.VMEM((2, blk, half), jnp.uint32), pltpu.VMEM((2, blk, half), jnp.uint32),
                        pltpu.VMEM((2, blk, f2), F32),
                        pltpu.SemaphoreType.DMA((2,)), pltpu.SemaphoreType.DMA((2,)), pltpu.SemaphoreType.DMA],
    )
    return pl.pallas_call(
        functools.partial(_ffn_kernel, nb=nb, blk=blk),
        grid_spec=grid_spec,
        out_shape=jax.ShapeDtypeStruct((n_rows, half), jnp.uint32),
        compiler_params=_params("arbitrary"),
        name="expert_ffn",
    )(tables["first_block"], tables["n_blocks"], tables["n_act"], buf, wgu, wdn)


def _store_class_views(xt, scr, view_refs, views):
    tm, d = xt.shape
    if views:
        for j in range(d // 128):
            scr[j] = xt[:, j * 128:(j + 1) * 128]
    for ref, r in zip(view_refs, views):
        for cls in range(r):
            for j in range(d // 128):
                lanes = slice(cls * d + j * 128, cls * d + (j + 1) * 128)
                ref[:, lanes] = scr[j, pl.ds(cls, tm // r, stride=r), :].astype(BF16)


def _views_kernel(x_ref, xb_ref, *rest, views):
    view_refs, scr = rest[:len(views)], rest[len(views)]
    xt = x_ref[...]
    xb_ref[...] = xt.astype(BF16)
    _store_class_views(xt, scr, view_refs, views)


def _input_views(xf, views):
    n, d = xf.shape
    tm = TOKEN_TILE
    tok = pl.BlockSpec((tm, d), lambda t: (t, 0))
    return pl.pallas_call(
        functools.partial(_views_kernel, views=views),
        grid=(n // tm,),
        in_specs=[tok],
        out_specs=[tok] + [pl.BlockSpec((tm // r, r * d), lambda t: (t, 0)) for r in views],
        out_shape=[jax.ShapeDtypeStruct((n, d), BF16)]
        + [jax.ShapeDtypeStruct((n // r, r * d), BF16) for r in views],
        scratch_shapes=[pltpu.VMEM((d // 128, tm, 128), F32)],
        compiler_params=_params("arbitrary"),
        name="input_views",
    )(xf)


def _merge_kernel(dst_ref, off_ref, len_ref, tot_ref, y_hbm, x1_ref, pos_ref, gt_ref, g_ref, b_ref,
                  x2_ref, x2b_ref, *rest, nt, alpha, views):
    view_refs, (ys, x2_scr, sem) = rest[:len(views)], rest[len(views):]
    t = pl.program_id(0)
    cur = t % 2
    tm, d = x1_ref.shape
    srows = ys.shape[1]

    def fetch(tile, buf):
        for e in range(N_EXPERTS):
            k = tile * N_EXPERTS + e
            n = pl.multiple_of(len_ref[k], SEG_ALIGN)
            dst = pl.multiple_of(off_ref[k], SEG_ALIGN)
            src = pl.multiple_of(dst_ref[k], SEG_ALIGN)

            @pl.when(n > 0)
            def _():
                pltpu.make_async_copy(y_hbm.at[pl.ds(src, n)], ys.at[buf, pl.ds(dst, n)], sem.at[buf]).start()

    @pl.when(t == 0)
    def _():
        ys[...] = jnp.zeros_like(ys)
        fetch(t, cur)

    @pl.when(t + 1 < nt)
    def _():
        fetch(t + 1, 1 - cur)

    n_all = pl.multiple_of(tot_ref[t], SEG_ALIGN)
    pltpu.make_async_copy(y_hbm.at[pl.ds(0, n_all)], ys.at[cur, pl.ds(0, n_all)], sem.at[cur]).wait()

    y_lo, y_hi = _unpack_halves(ys[cur])
    pos = pos_ref[...]
    gt = gt_ref[...]
    lane = lax.broadcasted_iota(jnp.int32, (tm, srows), 1)
    sel = (jnp.where(lane == pos[:, 0:1], gt[:, 0:1], 0.0)
           + jnp.where(lane == pos[:, 1:2], gt[:, 1:2], 0.0)).astype(BF16)
    m = jnp.concatenate([jnp.dot(sel, y_lo, preferred_element_type=F32),
                         jnp.dot(sel, y_hi, preferred_element_type=F32)], axis=1)
    x2 = _layer_norm(alpha * x1_ref[...] + m, g_ref[...], b_ref[...])
    x2_ref[...] = x2
    x2b_ref[...] = x2.astype(BF16)
    _store_class_views(x2, x2_scr, view_refs, views)


def _expert_merge(y_buf, pos_t, gate_t, x1, g, b, alpha, tables, views):
    n, d = x1.shape
    tm = TOKEN_TILE
    nt = n // tm
    tok = pl.BlockSpec((tm, d), lambda t, *_: (t, 0))
    pair = pl.BlockSpec((tm, 2), lambda t, *_: (t, 0))
    vec = pl.BlockSpec((1, d), lambda t, *_: (0, 0))
    grid_spec = pltpu.PrefetchScalarGridSpec(
        num_scalar_prefetch=4,
        grid=(nt,),
        in_specs=[pl.BlockSpec(memory_space=pl.ANY), tok, pair, pair, vec, vec],
        out_specs=[tok, tok] + [pl.BlockSpec((tm // r, r * d), lambda t, *_: (t, 0)) for r in views],
        scratch_shapes=[pltpu.VMEM((2, SORTED_ROWS, d // 2), jnp.uint32), pltpu.VMEM((d // 128, tm, 128), F32),
                        pltpu.SemaphoreType.DMA((2,))],
    )
    return pl.pallas_call(
        functools.partial(_merge_kernel, nt=nt, alpha=alpha, views=views),
        grid_spec=grid_spec,
        out_shape=[jax.ShapeDtypeStruct((n, d), F32), jax.ShapeDtypeStruct((n, d), BF16)]
        + [jax.ShapeDtypeStruct((n // r, r * d), BF16) for r in views],
        compiler_params=_params("arbitrary"),
        name="expert_merge",
    )(tables["dst"], tables["off"], tables["len"], tables["tot"], y_buf, x1, pos_t, gate_t, g, b)


def _routing_tables(counts):
    blk = ROW_BLOCK
    nt = counts.shape[0]
    seg = (counts + SEG_ALIGN - 1) // SEG_ALIGN * SEG_ALIGN
    off = jnp.cumsum(seg, axis=1) - seg
    used = jnp.sum(seg, axis=0)
    region = (used + blk - 1) // blk * blk
    region_end = jnp.cumsum(region)
    region_start = region_end - region
    dst = region_start[None, :] + jnp.cumsum(seg, axis=0) - seg
    n_act = region_end[-1] // blk
    i32 = lambda v: v.astype(jnp.int32)
    return dict(dst=i32(dst.reshape(-1)), off=i32(off.reshape(-1)), len=i32(seg.reshape(-1)),
                tot=i32(jnp.sum(seg, axis=1)), fill_row=i32(region_start + used), fill_len=i32(region - used),
                n_act=i32(n_act.reshape(1)), first_block=i32(region_start // blk), n_blocks=i32(region // blk))


def kernel(x, w_in, w_attn_out, w_conv_out, w_o, conv_w, conv_b, conv_ln_g, conv_ln_b, ln_mix_g, ln_mix_b,
           expert_w_gate_up, expert_w_down, ln_ffn_g, ln_ffn_b, router_w, rel_bias):
    batch, seq, d = x.shape
    depth = w_in.shape[0]
    n = batch * seq
    aw, gw = ATTN_WIDTH, GROUP_WIDTH
    assert d % 256 == 0 and n % 1024 == 0
    for window, r in DILATION_GROUPS:
        assert window // r == BAND and seq % (r * BAND) == 0
    alpha = (2 * depth) ** 0.25
    scale = HEAD_DIM ** -0.5

    biases = [_band_bias(rel_bias, gi, r) for gi, (_, r) in enumerate(DILATION_GROUPS)]
    rwt = router_w.T.astype(BF16)
    nt = n // TOKEN_TILE
    nb = -(-(2 * n + N_EXPERTS * (nt * (SEG_ALIGN - 1) + ROW_BLOCK - 1)) // ROW_BLOCK)
    dilations = tuple(r for _, r in DILATION_GROUPS if r > 1)

    xf = x.reshape(n, d)
    xb, *xv = _input_views(xf, dilations)
    xviews = dict(zip(dilations, xv))
    for l in range(depth):
        wl = w_in[l]

        def qkv_cols(gi):
            return jnp.concatenate([wl[:, gi * gw:(gi + 1) * gw] * scale,
                                    wl[:, aw + gi * gw:aw + (gi + 1) * gw],
                                    wl[:, 2 * aw + gi * gw:2 * aw + (gi + 1) * gw]], axis=1)

        w0 = jnp.concatenate([wl[:, 3 * aw + d:], wl[:, 3 * aw:3 * aw + d], qkv_cols(0)], axis=1).astype(BF16)
        proj0 = _project(xb, w0, 512)
        outs, lses = [], []
        for gi, (_, r) in enumerate(DILATION_GROUPS):
            if r == 1:
                qkv, col0 = proj0.reshape(batch, seq, proj0.shape[1]), (3 * d) // gw
            else:
                qkv = _project_dilated(xviews[r], qkv_cols(gi).astype(BF16), batch, seq, r)
                qkv, col0 = qkv.reshape(batch * r, seq // r, 3 * gw), 0
            o, lse = _attention(qkv, biases[gi], batch, seq, r, col0)
            outs.append(o)
            lses.append(lse)

        lw = dict(wao=w_attn_out[l].astype(BF16), cw=conv_w[l], cb=conv_b[l][None], clg=conv_ln_g[l][None],
                  clb=conv_ln_b[l][None], wco=w_conv_out[l].astype(BF16), wo=w_o[l].astype(BF16),
                  lng=ln_mix_g[l][None], lnb=ln_mix_b[l][None], rwt=rwt, alpha=alpha)
        x1, x1b, gate, pos, cnt = _mixer_tail(outs, lses, proj0, xf, lw, batch, seq)

        tables = _routing_tables(cnt[:, :, 0].astype(jnp.int32))
        buf = _dispatch(x1b, pos, tables, nb)
        y_buf = _expert_ffn(buf, expert_w_gate_up, expert_w_down, l, tables)
        views = dilations if l + 1 < depth else ()
        xf, xb, *xv = _expert_merge(y_buf, pos.T, gate.T, x1, ln_ffn_g[l][None], ln_ffn_b[l][None], alpha,
                                    tables, views)
        xviews = dict(zip(views, xv))
    return xf.reshape(batch, seq, d)
```

```python
import functools
import math

import jax
import jax.numpy as jnp
from jax import lax
from jax.experimental import pallas as pl
from jax.experimental.pallas import tpu as pltpu

F32 = jnp.float32
BF16 = jnp.bfloat16

HEAD_DIM = 64
HEADS_PER_GROUP = 4
GROUP_WIDTH = HEADS_PER_GROUP * HEAD_DIM
DILATION_GROUPS = ((128, 1), (512, 4), (2048, 16))
N_GROUPS = len(DILATION_GROUPS)
ATTN_WIDTH = N_GROUPS * GROUP_WIDTH
BAND = 128
CONV_KERNEL = 31
CONV_HALO = 32
CONV_TAIL = 16
N_BUCKETS = 32
MAX_DISTANCE = 2048
N_EXPERTS = 16
EXPERTS_PER_GROUP = 4
LN_EPS = 1e-5
MASKED = -1e30

WEIGHT_DMA_PRIORITY = 1
PROJ_ROWS = 2048
ROW_BLOCK = 256
TOKEN_TILE = 256
SEG_ALIGN = 8
SORTED_ROWS = -(-(2 * TOKEN_TILE + N_EXPERTS * (SEG_ALIGN - 1)) // 128) * 128
VMEM_LIMIT = 56 * 1024 * 1024


def _sigmoid(v):
    return 1.0 / (1.0 + jnp.exp(-v))


def _layer_norm(z, g, b):
    mu = jnp.mean(z, axis=-1, keepdims=True)
    zc = z - mu
    var = jnp.mean(zc * zc, axis=-1, keepdims=True)
    return zc * lax.rsqrt(var + LN_EPS) * g + b


def _params(*sem):
    return pltpu.CompilerParams(dimension_semantics=sem, vmem_limit_bytes=VMEM_LIMIT)


def _mm_kernel(x_ref, w_ref, o_ref):
    o_ref[...] = jnp.dot(x_ref[...], w_ref[...], preferred_element_type=F32).astype(o_ref.dtype)


def _project(xb, w, tm):
    n, d = xb.shape
    c = w.shape[1]
    return pl.pallas_call(
        _mm_kernel,
        grid=(n // tm,),
        in_specs=[pl.BlockSpec((tm, d), lambda i: (i, 0)),
                  pl.BlockSpec((d, c), lambda i: (0, 0))],
        out_specs=pl.BlockSpec((tm, c), lambda i: (i, 0)),
        out_shape=jax.ShapeDtypeStruct((n, c), BF16),
        compiler_params=_params("arbitrary"),
        name="proj_natural",
    )(xb, w)


def _project_dilated(xv, w, batch, seq, r):
    d, c = w.shape
    n = xv.shape[0] * r
    sub = seq // r
    tm = min(PROJ_ROWS, sub)
    nl = sub // tm
    cps = max(1, min(r, PROJ_ROWS // sub))
    assert r % cps == 0

    def mm_classes(x_ref, w_ref, o_ref):
        for k in range(cps):
            o_ref[k * tm:(k + 1) * tm, :] = jnp.dot(x_ref[:, k * d:(k + 1) * d], w_ref[...],
                                                    preferred_element_type=F32).astype(o_ref.dtype)

    return pl.pallas_call(
        mm_classes,
        grid=(batch, r // cps, nl),
        in_specs=[pl.BlockSpec((tm, cps * d), lambda b, cg, j: (b * nl + j, cg)),
                  pl.BlockSpec((d, c), lambda b, cg, j: (0, 0))],
        out_specs=pl.BlockSpec((cps * tm, c), lambda b, cg, j: ((b * (r // cps) + cg) * nl + j, 0)),
        out_shape=jax.ShapeDtypeStruct((n, c), BF16),
        compiler_params=_params("arbitrary", "arbitrary", "arbitrary"),
        name=f"proj_dilated_{r}",
    )(xv, w)


def _attn_kernel(q_ref, kp_ref, vp_ref, kc_ref, vc_ref, bias_ref, o_ref, lse_ref, k_scr, v_scr, *, nqb):
    i = pl.program_id(1)
    k_scr[0:BAND, :] = kp_ref[0]
    k_scr[BAND:, :] = kc_ref[0]
    v_scr[0:BAND, :] = vp_ref[0]
    v_scr[BAND:, :] = vc_ref[0]
    col = lax.broadcasted_iota(jnp.int32, (BAND, 2 * BAND), 1)
    first_head = lax.broadcasted_iota(jnp.int32, (BAND, 2 * HEAD_DIM), 1) < HEAD_DIM

    def body(j, carry):
        r0 = pl.multiple_of(j * BAND, BAND)
        q = q_ref[0, pl.ds(r0, BAND), :]
        kk = k_scr[pl.ds(r0, 2 * BAND), :]
        vv = v_scr[pl.ds(r0, 2 * BAND), :]
        no_prev = jnp.logical_and(jnp.logical_and(i == 0, j == 0), col < BAND)
        outs, lses = [], []
        for pair in range(HEADS_PER_GROUP // 2):
            sl = slice(pair * 2 * HEAD_DIM, (pair + 1) * 2 * HEAD_DIM)
            q2, k2, v2 = q[:, sl], kk[:, sl], vv[:, sl]
            o_pair, lse_pair = None, None
            for hh in range(2):
                mine = first_head if hh == 0 else jnp.logical_not(first_head)
                qm = jnp.where(mine, q2, jnp.zeros_like(q2))
                s = lax.dot_general(qm, k2, (((1,), (1,)), ((), ())), preferred_element_type=F32)
                s = s + bias_ref[2 * pair + hh]
                s = jnp.where(no_prev, MASKED, s)
                m = jnp.max(s, axis=-1, keepdims=True)
                p = jnp.exp(s - m)
                den = jnp.sum(p, axis=-1, keepdims=True)
                o = jnp.dot(p.astype(BF16), v2, preferred_element_type=F32) * (1.0 / den)
                lse = jnp.broadcast_to(m + jnp.log(den), (BAND, 2 * HEAD_DIM))
                o_pair = o if hh == 0 else jnp.where(first_head, o_pair, o)
                lse_pair = lse if hh == 0 else jnp.where(first_head, lse_pair, lse)
            outs.append(o_pair)
            lses.append(lse_pair)
        o_ref[0, pl.ds(r0, BAND), :] = jnp.concatenate(outs, axis=1).astype(o_ref.dtype)
        lse_ref[0, pl.ds(r0, BAND), :] = jnp.concatenate(lses, axis=1)
        return carry

    lax.fori_loop(0, nqb, body, 0, unroll=math.gcd(nqb, 8))


def _attention(qkv, bias, batch, seq, r, col0):
    sub = seq // r
    nqb = min(8, sub // BAND)
    rows = nqb * BAND
    nt = sub // rows
    gw = GROUP_WIDTH

    def cur(c):
        return pl.BlockSpec((1, rows, gw), lambda bc, i: (bc, i, c))

    def prev(c):
        return pl.BlockSpec((1, BAND, gw), lambda bc, i: (bc, jnp.maximum(i * nqb - 1, 0), c))

    out_spec = pl.BlockSpec((1, rows, gw), lambda bc, i: (bc // r, i, bc % r))
    o, lse = pl.pallas_call(
        functools.partial(_attn_kernel, nqb=nqb),
        grid=(batch * r, nt),
        in_specs=[cur(col0), prev(col0 + 1), prev(col0 + 2), cur(col0 + 1), cur(col0 + 2),
                  pl.BlockSpec((HEADS_PER_GROUP, BAND, 2 * BAND), lambda bc, i: (0, 0, 0))],
        out_specs=[out_spec, out_spec],
        out_shape=[jax.ShapeDtypeStruct((batch, sub, r * gw), BF16),
                   jax.ShapeDtypeStruct((batch, sub, r * gw), F32)],
        scratch_shapes=[pltpu.VMEM((rows + BAND, gw), BF16), pltpu.VMEM((rows + BAND, gw), BF16)],
        compiler_params=_params("arbitrary", "arbitrary"),
        name=f"attn_dilation_{r}",
    )(qkv, qkv, qkv, qkv, qkv, bias)
    return o.reshape(batch * sub, r * gw), lse.reshape(batch * sub, r * gw)


def _t5_bucket(dist):
    max_exact = N_BUCKETS // 2
    n = jnp.maximum(dist, 0)
    nf = jnp.maximum(n, 1).astype(F32)
    large = max_exact + (jnp.log(nf / max_exact) / math.log(MAX_DISTANCE / max_exact)
                         * (N_BUCKETS - max_exact)).astype(jnp.int32)
    large = jnp.minimum(large, N_BUCKETS - 1)
    return jnp.where(n < max_exact, n, large)


def _band_bias(rel_bias, gi, r):
    qi = jnp.arange(BAND)[:, None]
    kj = jnp.arange(2 * BAND)[None, :]
    dist = qi + BAND - kj
    bucket = _t5_bucket(dist * r)
    valid = (dist >= 0) & (dist <= BAND)
    b = jnp.full((HEADS_PER_GROUP, BAND, 2 * BAND), MASKED, F32)
    for k in range(N_BUCKETS):
        row = rel_bias[k, gi * HEADS_PER_GROUP:(gi + 1) * HEADS_PER_GROUP].astype(F32)
        b = jnp.where(((bucket == k) & valid)[None], row[:, None, None], b)
    return b


def _post_kernel(o0_ref, o1_ref, o2_ref, l0_ref, l1_ref, l2_ref, u_ref, uh_ref, g_ref, x_ref,
                 wao_ref, cw_ref, cb_ref, clg_ref, clb_ref, wco_ref, wo_ref, lng_ref, lnb_ref, rwt_ref,
                 x1_ref, x1b_ref, gate_ref, pos_ref, cnt_ref,
                 hs_scr, conv_scr, ys_scr, nat_scr, *, tm, seq, alpha):
    i = pl.program_id(0)
    d = x_ref.shape[1]
    c = d // 2
    gw = GROUP_WIDTH

    def natural(ref, slot, r):
        if r == 1:
            return ref[...].astype(F32)
        nl = gw // 128
        for cls in range(r):
            for j in range(nl):
                lanes = slice(cls * gw + j * 128, cls * gw + (j + 1) * 128)
                nat_scr[slot * nl + j, pl.ds(cls, tm // r, stride=r), :] = ref[:, lanes].astype(F32)
        return jnp.concatenate([nat_scr[slot * nl + j] for j in range(nl)], axis=1)

    rs = [r for _, r in DILATION_GROUPS]
    o0, o1, o2 = (natural(ref, s, r) for s, (ref, r) in enumerate(zip((o0_ref, o1_ref, o2_ref), rs)))
    l0, l1, l2 = (natural(ref, 3 + s, r) for s, (ref, r) in enumerate(zip((l0_ref, l1_ref, l2_ref), rs)))
    mx = jnp.maximum(jnp.maximum(l0, l1), l2)
    e0, e1, e2 = jnp.exp(l0 - mx), jnp.exp(l1 - mx), jnp.exp(l2 - mx)
    attn = (e0 * o0 + e1 * o1 + e2 * o2) * (1.0 / (e0 + e1 + e2))
    attn_branch = jnp.dot(attn.astype(BF16), wao_ref[...], preferred_element_type=F32)

    u = u_ref[...].astype(F32)
    uh = uh_ref[...].astype(F32)
    hh = uh[:, :c] * _sigmoid(uh[:, c:])
    seq_start = (i * tm) % seq == 0
    hs_scr[0:CONV_HALO, :] = jnp.where(seq_start, 0.0, hh)
    hs_scr[CONV_HALO:CONV_HALO + tm, :] = u[:, :c] * _sigmoid(u[:, c:])
    hs_scr[CONV_HALO + tm:, :] = jnp.zeros((CONV_TAIL, c), F32)
    off = CONV_HALO - (CONV_KERNEL - 1)
    rows_y = tm + 8
    for cc in range(c // 128):
        ls = slice(cc * 128, (cc + 1) * 128)
        acc = None
        for s in range(8):
            y = None
            for a in range((off + CONV_KERNEL + 7) // 8):
                k = 8 * a + s - off
                if 0 <= k < CONV_KERNEL:
                    term = hs_scr[8 * a:8 * a + rows_y, ls] * cw_ref[k:k + 1, ls]
                    y = term if y is None else y + term
            if s == 0:
                acc = y[0:tm]
            else:
                ys_scr[s] = y
                acc = acc + ys_scr[s, s:s + tm, :]
        conv_scr[:, ls] = acc
    hc = _layer_norm(conv_scr[...] + cb_ref[...], clg_ref[...], clb_ref[...])
    hc = hc * _sigmoid(hc)
    conv_branch = jnp.dot(hc.astype(BF16), wco_ref[...], preferred_element_type=F32)

    gates = _sigmoid(g_ref[...].astype(F32))
    merged = gates[:, :d] * attn_branch + gates[:, d:] * conv_branch
    hmix = jnp.dot(merged.astype(BF16), wo_ref[...], preferred_element_type=F32)
    x1 = _layer_norm(alpha * x_ref[...] + hmix, lng_ref[...], lnb_ref[...])
    x1_ref[...] = x1
    x1b = x1.astype(BF16)
    x1b_ref[...] = x1b

    logits = lax.dot_general(rwt_ref[...], x1b, (((1,), (1,)), ((), ())), preferred_element_type=F32)
    ex = jnp.exp(logits - jnp.max(logits, axis=0, keepdims=True))
    rows = [ex[e:e + 1, :] for e in range(N_EXPERTS)]
    best, gsel = None, None
    for g in range(N_EXPERTS // EXPERTS_PER_GROUP):
        v = rows[g * EXPERTS_PER_GROUP:(g + 1) * EXPERTS_PER_GROUP]
        score = None
        for a in range(EXPERTS_PER_GROUP):
            for b in range(a + 1, EXPERTS_PER_GROUP):
                ps = v[a] + v[b]
                score = ps if score is None else jnp.maximum(score, ps)
        if best is None:
            best, gsel = score, jnp.zeros_like(score, dtype=jnp.int32)
        else:
            upd = score > best
            gsel = jnp.where(upd, g, gsel)
            best = jnp.where(upd, score, best)
    vals = []
    for j in range(EXPERTS_PER_GROUP):
        vj = rows[j]
        for g in range(1, N_EXPERTS // EXPERTS_PER_GROUP):
            vj = jnp.where(gsel == g, rows[g * EXPERTS_PER_GROUP + j], vj)
        vals.append(vj)
    v1, i1 = vals[0], jnp.zeros_like(gsel)
    for j in range(1, EXPERTS_PER_GROUP):
        upd = vals[j] > v1
        i1 = jnp.where(upd, j, i1)
        v1 = jnp.where(upd, vals[j], v1)
    v2, i2 = jnp.full_like(v1, -1.0), jnp.zeros_like(gsel)
    for j in range(EXPERTS_PER_GROUP):
        upd = jnp.logical_and(i1 != j, vals[j] > v2)
        i2 = jnp.where(upd, j, i2)
        v2 = jnp.where(upd, vals[j], v2)
    ea = gsel * EXPERTS_PER_GROUP + i1
    eb = gsel * EXPERTS_PER_GROUP + i2
    inv = 1.0 / (v1 + v2)
    gate_ref[...] = jnp.concatenate([v1 * inv, v2 * inv], axis=0)

    eid = lax.broadcasted_iota(jnp.int32, (N_EXPERTS, tm), 0)
    hit_a = eid == ea
    hit_b = eid == eb
    onehot = jnp.where(jnp.logical_or(hit_a, hit_b), 1.0, 0.0)
    before = (lax.broadcasted_iota(jnp.int32, (tm, tm), 0)
              < lax.broadcasted_iota(jnp.int32, (tm, tm), 1))
    upper = jnp.where(before, 1.0, 0.0).astype(BF16)
    rank = jnp.dot(onehot.astype(BF16), upper, preferred_element_type=F32)
    count = jnp.sum(onehot, axis=1, keepdims=True)
    seg_len = jnp.floor((count + (SEG_ALIGN - 1)) * (1.0 / SEG_ALIGN)) * SEG_ALIGN
    lower = (lax.broadcasted_iota(jnp.int32, (N_EXPERTS, N_EXPERTS), 1)
             < lax.broadcasted_iota(jnp.int32, (N_EXPERTS, N_EXPERTS), 0))
    seg_off = jnp.dot(jnp.where(lower, 1.0, 0.0).astype(BF16),
                      jnp.broadcast_to(seg_len, (N_EXPERTS, 128)).astype(BF16),
                      preferred_element_type=F32)[:, 0:1]
    row = rank + seg_off
    pa = jnp.sum(jnp.where(hit_a, row, 0.0), axis=0, keepdims=True)
    pb = jnp.sum(jnp.where(hit_b, row, 0.0), axis=0, keepdims=True)
    pos_ref[...] = jnp.concatenate([pa, pb], axis=0).astype(jnp.int32)
    cnt_ref[0] = jnp.broadcast_to(count, (N_EXPERTS, 128))


def _mixer_tail(o, lse, proj0, x, lw, batch, seq):
    n, d = x.shape
    tm = TOKEN_TILE
    nt = n // tm
    gw = GROUP_WIDTH
    hb = tm // CONV_HALO

    def tok(width):
        return pl.BlockSpec((tm, width), lambda i: (i, 0))

    def const(shape):
        return pl.BlockSpec(shape, lambda i: (0,) * len(shape))

    def grouped(r):
        return pl.BlockSpec((tm // r, r * gw), lambda i: (i, 0))

    rs = [r for _, r in DILATION_GROUPS]
    in_specs = (
        [grouped(r) for r in rs] * 2
        + [pl.BlockSpec((tm, d), lambda i: (i, 2)),
           pl.BlockSpec((CONV_HALO, d), lambda i: (jnp.maximum(i * hb - 1, 0), 2)),
           pl.BlockSpec((tm, 2 * d), lambda i: (i, 0)),
           tok(d)]
        + [const(lw[k].shape) for k in ("wao", "cw", "cb", "clg", "clb", "wco", "wo", "lng", "lnb", "rwt")]
    )
    lane_row = pl.BlockSpec((2, tm), lambda i: (0, i))
    out_specs = [tok(d), tok(d), lane_row, lane_row,
                 pl.BlockSpec((1, N_EXPERTS, 128), lambda i: (i, 0, 0))]
    out_shape = [jax.ShapeDtypeStruct((n, d), F32), jax.ShapeDtypeStruct((n, d), BF16),
                 jax.ShapeDtypeStruct((2, n), F32), jax.ShapeDtypeStruct((2, n), jnp.int32),
                 jax.ShapeDtypeStruct((nt, N_EXPERTS, 128), F32)]
    alpha = lw["alpha"]
    return pl.pallas_call(
        functools.partial(_post_kernel, tm=tm, seq=seq, alpha=alpha),
        grid=(nt,),
        in_specs=in_specs,
        out_specs=out_specs,
        out_shape=out_shape,
        scratch_shapes=[pltpu.VMEM((CONV_HALO + tm + CONV_TAIL, d // 2), F32), pltpu.VMEM((tm, d // 2), F32),
                        pltpu.VMEM((8, tm + 8, 128), F32),
                        pltpu.VMEM((2 * N_GROUPS * (gw // 128), tm, 128), F32)],
        compiler_params=_params("arbitrary"),
        name="mixer_tail",
    )(o[0], o[1], o[2], lse[0], lse[1], lse[2], proj0, proj0, proj0, x,
      *[lw[k] for k in ("wao", "cw", "cb", "clg", "clb", "wco", "wo", "lng", "lnb", "rwt")])


def _pack_halves(v):
    h = v.shape[1] // 2
    bits = lax.bitcast_convert_type(v, jnp.uint32)
    return (bits[:, :h] >> 16) | (bits[:, h:] & jnp.uint32(0xFFFF0000))


def _unpack_halves(w):
    lo = lax.bitcast_convert_type(w << 16, F32).astype(BF16)
    hi = lax.bitcast_convert_type(w & jnp.uint32(0xFFFF0000), F32).astype(BF16)
    return lo, hi


def _dispatch_kernel(dst_ref, off_ref, len_ref, tot_ref, fill_row_ref, fill_len_ref, nact_ref,
                     x_ref, pos_ref, buf_hbm, xs, zeros, sem, zsem, *, nt, nb, blk):
    t = pl.program_id(0)
    cur = t % 2
    srows = xs.shape[1]

    def tile_wait(tile, buf):
        n = pl.multiple_of(tot_ref[tile], SEG_ALIGN)
        pltpu.make_async_copy(xs.at[buf, pl.ds(0, n)], buf_hbm.at[pl.ds(0, n)], sem.at[buf]).wait()

    @pl.when(t >= 2)
    def _():
        tile_wait(t - 2, cur)

    pos = pos_ref[...]
    row = lax.broadcasted_iota(jnp.int32, (srows, pos.shape[1]), 0)
    hit = jnp.logical_or(row == pos[0:1, :], row == pos[1:2, :])
    sel = jnp.where(hit, 1.0, 0.0).astype(BF16)
    xs[cur] = _pack_halves(jnp.dot(sel, x_ref[...], preferred_element_type=F32))

    for e in range(N_EXPERTS):
        k = t * N_EXPERTS + e
        n = pl.multiple_of(len_ref[k], SEG_ALIGN)
        src = pl.multiple_of(off_ref[k], SEG_ALIGN)
        dst = pl.multiple_of(dst_ref[k], SEG_ALIGN)

        @pl.when(n > 0)
        def _():
            pltpu.make_async_copy(xs.at[cur, pl.ds(src, n)], buf_hbm.at[pl.ds(dst, n)], sem.at[cur]).start()

    @pl.when(t == nt - 1)
    def _():
        if nt >= 2:
            tile_wait(t - 1, 1 - cur)
        tile_wait(t, cur)
        zeros[...] = jnp.zeros_like(zeros)

        def region_fill(e):
            n = pl.multiple_of(fill_len_ref[e], SEG_ALIGN)
            dst = pl.multiple_of(fill_row_ref[e], SEG_ALIGN)
            return n, pltpu.make_async_copy(zeros.at[pl.ds(0, n)], buf_hbm.at[pl.ds(dst, n)], zsem)

        def block_fill(b):
            return pltpu.make_async_copy(zeros, buf_hbm.at[pl.ds(pl.multiple_of(b * blk, blk), blk)], zsem)

        def each_block(fn):
            def body(b, carry):
                fn(block_fill(b))
                return carry
            lax.fori_loop(nact_ref[0], nb, body, 0)

        for e in range(N_EXPERTS):
            n, cp = region_fill(e)
            pl.when(n > 0)(cp.start)
        each_block(lambda cp: cp.start())
        for e in range(N_EXPERTS):
            n, cp = region_fill(e)
            pl.when(n > 0)(cp.wait)
        each_block(lambda cp: cp.wait())


def _dispatch(x1b, pos, tables, nb):
    n, d = x1b.shape
    tm = TOKEN_TILE
    nt = n // tm
    blk = ROW_BLOCK
    grid_spec = pltpu.PrefetchScalarGridSpec(
        num_scalar_prefetch=7,
        grid=(nt,),
        in_specs=[pl.BlockSpec((tm, d), lambda t, *_: (t, 0)),
                  pl.BlockSpec((2, tm), lambda t, *_: (0, t))],
        out_specs=pl.BlockSpec(memory_space=pl.ANY),
        scratch_shapes=[pltpu.VMEM((2, SORTED_ROWS, d // 2), jnp.uint32), pltpu.VMEM((blk, d // 2), jnp.uint32),
                        pltpu.SemaphoreType.DMA((2,)), pltpu.SemaphoreType.DMA],
    )
    return pl.pallas_call(
        functools.partial(_dispatch_kernel, nt=nt, nb=nb, blk=blk),
        grid_spec=grid_spec,
        out_shape=jax.ShapeDtypeStruct((nb * blk, d // 2), jnp.uint32),
        compiler_params=_params("arbitrary"),
        name="expert_dispatch",
    )(tables["dst"], tables["off"], tables["len"], tables["tot"], tables["fill_row"], tables["fill_len"],
      tables["n_act"], x1b, pos)


def _ffn_kernel(first_ref, count_ref, nact_ref, xs_hbm, wgu_hbm, wdn_hbm, y_hbm,
                wgu_f, wdn_f, wgu_b, wdn_b, xbuf, obuf, h_scr, wsem, isem, osem, zsem, *, nb, blk, layer):
    e = pl.program_id(0)
    n_exp = pl.num_programs(0)
    f = wdn_f.shape[1]
    half = xbuf.shape[2]
    b0 = first_ref[e]
    n_blocks = count_ref[e]

    def weights(ex, slot):
        return (pltpu.make_async_copy(wgu_hbm.at[layer, ex], wgu_f.at[slot], wsem.at[slot]),
                pltpu.make_async_copy(wdn_hbm.at[layer, ex], wdn_f.at[slot], wsem.at[slot]))

    def fetch_weights(ex):
        for cp in weights(ex, ex % 2):
            cp.start(priority=WEIGHT_DMA_PRIORITY)

    @pl.when(e == 0)
    def _():
        fetch_weights(e)

    for cp in weights(e, e % 2):
        cp.wait()

    @pl.when(e + 1 < n_exp)
    def _():
        fetch_weights(e + 1)

    def rows(j):
        return pl.ds(pl.multiple_of((b0 + j) * blk, blk), blk)

    def load(j, buf):
        return pltpu.make_async_copy(xs_hbm.at[rows(j)], xbuf.at[buf], isem.at[buf])

    def store(j, buf):
        return pltpu.make_async_copy(obuf.at[buf], y_hbm.at[rows(j)], osem.at[buf])

    def prefetch(j):
        @pl.when(j < n_blocks)
        def _():
            load(j, j % 2).start()

    def up(j):
        buf = j % 2
        x_lo, x_hi = _unpack_halves(xbuf[buf])
        h_scr[buf] = (jnp.dot(x_lo, wgu_b[0:half, :], preferred_element_type=F32)
                      + jnp.dot(x_hi, wgu_b[half:, :], preferred_element_type=F32))

    def release(j):
        @pl.when(j >= 2)
        def _():
            store(j - 2, j % 2).wait()

    def down(j):
        buf = j % 2
        a = h_scr[buf, :, 0:f]
        act = (a * _sigmoid(a) * h_scr[buf, :, f:]).astype(BF16)
        y = jnp.dot(act, wdn_b[...], preferred_element_type=F32)
        obuf[buf] = _pack_halves(y.astype(BF16).astype(F32))
        store(j, buf).start()

    @pl.when(n_blocks > 0)
    def _():
        load(0, 0).start()
        wgu_b[...] = wgu_f[e % 2].astype(BF16)
        wdn_b[...] = wdn_f[e % 2].astype(BF16)
        prefetch(1)
        load(0, 0).wait()
        up(0)

    def block(j, carry):
        prefetch(j + 1)
        release(j - 1)
        load(j, j % 2).wait()
        down(j - 1)
        up(j)
        return carry

    lax.fori_loop(1, n_blocks, block, 0)

    @pl.when(n_blocks > 0)
    def _():
        release(n_blocks - 1)
        down(n_blocks - 1)

    @pl.when(n_blocks >= 2)
    def _():
        store(n_blocks - 2, n_blocks % 2).wait()

    @pl.when(n_blocks >= 1)
    def _():
        store(n_blocks - 1, (n_blocks - 1) % 2).wait()

    @pl.when(e == pl.num_programs(0) - 1)
    def _():
        xbuf[0] = jnp.zeros((blk, half), jnp.uint32)

        def fill(b):
            return pltpu.make_async_copy(xbuf.at[0], y_hbm.at[pl.ds(pl.multiple_of(b * blk, blk), blk)], zsem)

        def each(fn):
            def body(b, carry):
                fn(fill(b))
                return carry
            lax.fori_loop(nact_ref[0], nb, body, 0)

        each(lambda cp: cp.start())
        each(lambda cp: cp.wait())


def _expert_ffn(buf, wgu, wdn, layer, tables):
    n_rows, half = buf.shape
    d = 2 * half
    blk = ROW_BLOCK
    nb = n_rows // blk
    f2 = wgu.shape[3]
    f = wdn.shape[2]
    grid_spec = pltpu.PrefetchScalarGridSpec(
        num_scalar_prefetch=3,
        grid=(N_EXPERTS,),
        in_specs=[pl.BlockSpec(memory_space=pl.ANY), pl.BlockSpec(memory_space=pl.ANY),
                  pl.BlockSpec(memory_space=pl.ANY)],
        out_specs=pl.BlockSpec(memory_space=pl.ANY),
        scratch_shapes=[pltpu.VMEM((2, d, f2), F32), pltpu.VMEM((2, f, d), F32),
                        pltpu.VMEM((d, f2), BF16), pltpu.VMEM((f, d), BF16),
                        pltpu.VMEM((2, blk, half), jnp.uint32), pltpu.VMEM((2, blk, half), jnp.uint32),
                        pltpu.VMEM((2, blk, f2), F32),
                        pltpu.SemaphoreType.DMA((2,)), pltpu.SemaphoreType.DMA((2,)),
                        pltpu.SemaphoreType.DMA((2,)), pltpu.SemaphoreType.DMA],
    )
    return pl.pallas_call(
        functools.partial(_ffn_kernel, nb=nb, blk=blk, layer=layer),
        grid_spec=grid_spec,
        out_shape=jax.ShapeDtypeStruct((n_rows, half), jnp.uint32),
        compiler_params=_params("arbitrary"),
        name="expert_ffn",
    )(tables["first_block"], tables["n_blocks"], tables["n_act"], buf, wgu, wdn)


def _store_class_views(xt, scr, view_refs, views):
    tm, d = xt.shape
    if views:
        for j in range(d // 128):
            scr[j] = xt[:, j * 128:(j + 1) * 128]
    for ref, r in zip(view_refs, views):
        for cls in range(r):
            for j in range(d // 128):
                lanes = slice(cls * d + j * 128, cls * d + (j + 1) * 128)
                ref[:, lanes] = scr[j, pl.ds(cls, tm // r, stride=r), :].astype(BF16)


def _views_kernel(x_ref, xb_ref, *rest, views):
    view_refs, scr = rest[:len(views)], rest[len(views)]
    xt = x_ref[...]
    xb_ref[...] = xt.astype(BF16)
    _store_class_views(xt, scr, view_refs, views)


def _input_views(xf, views):
    n, d = xf.shape
    tm = TOKEN_TILE
    tok = pl.BlockSpec((tm, d), lambda t: (t, 0))
    return pl.pallas_call(
        functools.partial(_views_kernel, views=views),
        grid=(n // tm,),
        in_specs=[tok],
        out_specs=[tok] + [pl.BlockSpec((tm // r, r * d), lambda t: (t, 0)) for r in views],
        out_shape=[jax.ShapeDtypeStruct((n, d), BF16)]
        + [jax.ShapeDtypeStruct((n // r, r * d), BF16) for r in views],
        scratch_shapes=[pltpu.VMEM((d // 128, tm, 128), F32)],
        compiler_params=_params("arbitrary"),
        name="input_views",
    )(xf)


def _merge_kernel(dst_ref, off_ref, len_ref, tot_ref, y_hbm, x1_ref, pos_ref, gt_ref, g_ref, b_ref,
                  x2_ref, x2b_ref, *rest, nt, alpha, views):
    view_refs, (ys, x2_scr, sem) = rest[:len(views)], rest[len(views):]
    t = pl.program_id(0)
    cur = t % 2
    tm, d = x1_ref.shape
    srows = ys.shape[1]

    def fetch(tile, buf):
        for e in range(N_EXPERTS):
            k = tile * N_EXPERTS + e
            n = pl.multiple_of(len_ref[k], SEG_ALIGN)
            dst = pl.multiple_of(off_ref[k], SEG_ALIGN)
            src = pl.multiple_of(dst_ref[k], SEG_ALIGN)

            @pl.when(n > 0)
            def _():
                pltpu.make_async_copy(y_hbm.at[pl.ds(src, n)], ys.at[buf, pl.ds(dst, n)], sem.at[buf]).start()

    @pl.when(t == 0)
    def _():
        ys[...] = jnp.zeros_like(ys)
        fetch(t, cur)

    @pl.when(t + 1 < nt)
    def _():
        fetch(t + 1, 1 - cur)

    n_all = pl.multiple_of(tot_ref[t], SEG_ALIGN)
    pltpu.make_async_copy(y_hbm.at[pl.ds(0, n_all)], ys.at[cur, pl.ds(0, n_all)], sem.at[cur]).wait()

    y_lo, y_hi = _unpack_halves(ys[cur])
    pos = pos_ref[...]
    gt = gt_ref[...]
    lane = lax.broadcasted_iota(jnp.int32, (tm, srows), 1)
    sel = (jnp.where(lane == pos[:, 0:1], gt[:, 0:1], 0.0)
           + jnp.where(lane == pos[:, 1:2], gt[:, 1:2], 0.0)).astype(BF16)
    m = jnp.concatenate([jnp.dot(sel, y_lo, preferred_element_type=F32),
                         jnp.dot(sel, y_hi, preferred_element_type=F32)], axis=1)
    x2 = _layer_norm(alpha * x1_ref[...] + m, g_ref[...], b_ref[...])
    x2_ref[...] = x2
    x2b_ref[...] = x2.astype(BF16)
    _store_class_views(x2, x2_scr, view_refs, views)


def _expert_merge(y_buf, pos_t, gate_t, x1, g, b, alpha, tables, views):
    n, d = x1.shape
    tm = TOKEN_TILE
    nt = n // tm
    tok = pl.BlockSpec((tm, d), lambda t, *_: (t, 0))
    pair = pl.BlockSpec((tm, 2), lambda t, *_: (t, 0))
    vec = pl.BlockSpec((1, d), lambda t, *_: (0, 0))
    grid_spec = pltpu.PrefetchScalarGridSpec(
        num_scalar_prefetch=4,
        grid=(nt,),
        in_specs=[pl.BlockSpec(memory_space=pl.ANY), tok, pair, pair, vec, vec],
        out_specs=[tok, tok] + [pl.BlockSpec((tm // r, r * d), lambda t, *_: (t, 0)) for r in views],
        scratch_shapes=[pltpu.VMEM((2, SORTED_ROWS, d // 2), jnp.uint32), pltpu.VMEM((d // 128, tm, 128), F32),
                        pltpu.SemaphoreType.DMA((2,))],
    )
    return pl.pallas_call(
        functools.partial(_merge_kernel, nt=nt, alpha=alpha, views=views),
        grid_spec=grid_spec,
        out_shape=[jax.ShapeDtypeStruct((n, d), F32), jax.ShapeDtypeStruct((n, d), BF16)]
        + [jax.ShapeDtypeStruct((n // r, r * d), BF16) for r in views],
        compiler_params=_params("arbitrary"),
        name="expert_merge",
    )(tables["dst"], tables["off"], tables["len"], tables["tot"], y_buf, x1, pos_t, gate_t, g, b)


def _routing_tables(counts):
    blk = ROW_BLOCK
    nt = counts.shape[0]
    seg = (counts + SEG_ALIGN - 1) // SEG_ALIGN * SEG_ALIGN
    off = jnp.cumsum(seg, axis=1) - seg
    used = jnp.sum(seg, axis=0)
    region = (used + blk - 1) // blk * blk
    region_end = jnp.cumsum(region)
    region_start = region_end - region
    dst = region_start[None, :] + jnp.cumsum(seg, axis=0) - seg
    n_act = region_end[-1] // blk
    i32 = lambda v: v.astype(jnp.int32)
    return dict(dst=i32(dst.reshape(-1)), off=i32(off.reshape(-1)), len=i32(seg.reshape(-1)),
                tot=i32(jnp.sum(seg, axis=1)), fill_row=i32(region_start + used), fill_len=i32(region - used),
                n_act=i32(n_act.reshape(1)), first_block=i32(region_start // blk), n_blocks=i32(region // blk))


def kernel(x, w_in, w_attn_out, w_conv_out, w_o, conv_w, conv_b, conv_ln_g, conv_ln_b, ln_mix_g, ln_mix_b,
           expert_w_gate_up, expert_w_down, ln_ffn_g, ln_ffn_b, router_w, rel_bias):
    batch, seq, d = x.shape
    depth = w_in.shape[0]
    n = batch * seq
    aw, gw = ATTN_WIDTH, GROUP_WIDTH
    assert d % 256 == 0 and n % 1024 == 0
    for window, r in DILATION_GROUPS:
        assert window // r == BAND and seq % (r * BAND) == 0
    alpha = (2 * depth) ** 0.25
    scale = HEAD_DIM ** -0.5

    biases = [_band_bias(rel_bias, gi, r) for gi, (_, r) in enumerate(DILATION_GROUPS)]
    rwt = router_w.T.astype(BF16)
    nt = n // TOKEN_TILE
    nb = -(-(2 * n + N_EXPERTS * (nt * (SEG_ALIGN - 1) + ROW_BLOCK - 1)) // ROW_BLOCK)
    dilations = tuple(r for _, r in DILATION_GROUPS if r > 1)

    xf = x.reshape(n, d)
    xb, *xv = _input_views(xf, dilations)
    xviews = dict(zip(dilations, xv))
    for l in range(depth):
        wl = w_in[l]

        def qkv_cols(gi):
            return jnp.concatenate([wl[:, gi * gw:(gi + 1) * gw] * scale,
                                    wl[:, aw + gi * gw:aw + (gi + 1) * gw],
                                    wl[:, 2 * aw + gi * gw:2 * aw + (gi + 1) * gw]], axis=1)

        w0 = jnp.concatenate([wl[:, 3 * aw + d:], wl[:, 3 * aw:3 * aw + d], qkv_cols(0)], axis=1).astype(BF16)
        proj0 = _project(xb, w0, 512)
        outs, lses = [], []
        for gi, (_, r) in enumerate(DILATION_GROUPS):
            if r == 1:
                qkv, col0 = proj0.reshape(batch, seq, proj0.shape[1]), (3 * d) // gw
            else:
                qkv = _project_dilated(xviews[r], qkv_cols(gi).astype(BF16), batch, seq, r)
                qkv, col0 = qkv.reshape(batch * r, seq // r, 3 * gw), 0
            o, lse = _attention(qkv, biases[gi], batch, seq, r, col0)
            outs.append(o)
            lses.append(lse)

        lw = dict(wao=w_attn_out[l].astype(BF16), cw=conv_w[l], cb=conv_b[l][None], clg=conv_ln_g[l][None],
                  clb=conv_ln_b[l][None], wco=w_conv_out[l].astype(BF16), wo=w_o[l].astype(BF16),
                  lng=ln_mix_g[l][None], lnb=ln_mix_b[l][None], rwt=rwt, alpha=alpha)
        x1, x1b, gate, pos, cnt = _mixer_tail(outs, lses, proj0, xf, lw, batch, seq)

        tables = _routing_tables(cnt[:, :, 0].astype(jnp.int32))
        buf = _dispatch(x1b, pos, tables, nb)
        y_buf = _expert_ffn(buf, expert_w_gate_up, expert_w_down, l, tables)
        views = dilations if l + 1 < depth else ()
        xf, xb, *xv = _expert_merge(y_buf, pos.T, gate.T, x1, ln_ffn_g[l][None], ln_ffn_b[l][None], alpha,
                                    tables, views)
        xviews = dict(zip(views, xv))
    return xf.reshape(batch, seq, d)
```

```python
import functools
import math

import jax
import jax.numpy as jnp
from jax import lax
from jax.experimental import pallas as pl
from jax.experimental.pallas import tpu as pltpu

F32 = jnp.float32
BF16 = jnp.bfloat16

HEAD_DIM = 64
HEADS_PER_GROUP = 4
GROUP_WIDTH = HEADS_PER_GROUP * HEAD_DIM
DILATION_GROUPS = ((128, 1), (512, 4), (2048, 16))
N_GROUPS = len(DILATION_GROUPS)
ATTN_WIDTH = N_GROUPS * GROUP_WIDTH
BAND = 128
CONV_KERNEL = 31
CONV_HALO = 32
CONV_TAIL = 16
CONV_ROWS = 128
PROJ_TILE = 512
PROJ_PIECES = 5
N_BUCKETS = 32
MAX_DISTANCE = 2048
N_EXPERTS = 16
EXPERTS_PER_GROUP = 4
LN_EPS = 1e-5
MASKED = -1e30

WEIGHT_DMA_PRIORITY = 1
PROJ_ROWS = 2048
ROW_BLOCK = 256
TOKEN_TILE = 256
SEG_ALIGN = 8
SORTED_ROWS = -(-(2 * TOKEN_TILE + N_EXPERTS * (SEG_ALIGN - 1)) // 128) * 128
VMEM_LIMIT = 56 * 1024 * 1024


def _sigmoid(v):
    return 1.0 / (1.0 + jnp.exp(-v))


def _layer_norm(z, g, b):
    mu = jnp.mean(z, axis=-1, keepdims=True)
    zc = z - mu
    var = jnp.mean(zc * zc, axis=-1, keepdims=True)
    return zc * lax.rsqrt(var + LN_EPS) * g + b


def _params(*sem):
    return pltpu.CompilerParams(dimension_semantics=sem, vmem_limit_bytes=VMEM_LIMIT)


def _proj_conv_kernel(x_ref, w_ref, cw_ref, cb_ref, clg_ref, clb_ref, o_ref, hc_ref,
                      u_scr, hs_scr, ys_scr, conv_scr, *, tm, nt, seq, u_col):
    s = pl.program_id(0)
    c = hc_ref.shape[1]
    prev, cur = (s + 1) % 2, s % 2

    @pl.when(s == 0)
    def _():
        u_scr[1] = jnp.zeros(u_scr.shape[1:], BF16)
        hs_scr[...] = jnp.zeros(hs_scr.shape, F32)

    u = u_scr[prev].astype(F32)
    seq_start = (jnp.maximum(s - 1, 0) * tm) % seq == 0
    hs_scr[0:CONV_HALO, :] = jnp.where(seq_start, 0.0, hs_scr[tm:tm + CONV_HALO, :])
    hs_scr[CONV_HALO:CONV_HALO + tm, :] = u[:, :c] * _sigmoid(u[:, c:])

    off = CONV_HALO - (CONV_KERNEL - 1)
    rows_y = CONV_ROWS + 8
    def conv_chunk(cc, rc, zero):
        ls = slice(cc * 128, (cc + 1) * 128)
        r0 = rc * CONV_ROWS
        acc = None
        for k in range(8):
            y = None
            for a in range((off + CONV_KERNEL + 7) // 8):
                tap = 8 * a + k - off
                if 0 <= tap < CONV_KERNEL:
                    term = hs_scr[r0 + 8 * a:r0 + 8 * a + rows_y, ls] * cw_ref[tap:tap + 1, ls]
                    y = term if y is None else y + term
            if k == 0:
                acc = y[0:CONV_ROWS] + zero
            else:
                ys_scr[k, rc * rows_y:(rc + 1) * rows_y, :] = y
                acc = acc + ys_scr[k, rc * rows_y + k:rc * rows_y + k + CONV_ROWS, :]
        conv_scr[r0:r0 + CONV_ROWS, ls] = acc

    cols = o_ref.shape[1]
    pieces = max(p for p in range(1, PROJ_PIECES + 1) if (cols // 128) % p == 0)
    pw = cols // pieces
    chunks = [(cc, rc) for cc in range(c // 128) for rc in range(tm // CONV_ROWS)]
    per_piece = -(-len(chunks) // pieces)
    for p in range(pieces):
        piece = jnp.dot(x_ref[...], w_ref[:, p * pw:(p + 1) * pw], preferred_element_type=F32)
        o_ref[:, p * pw:(p + 1) * pw] = piece.astype(o_ref.dtype)
        bits = lax.bitcast_convert_type(piece[0:8, 0:128], jnp.uint32)
        zero8 = lax.bitcast_convert_type((bits >> 16) >> 16, F32)
        zero = jnp.concatenate([zero8] * (CONV_ROWS // 8), axis=0)
        for cc, rc in chunks[p * per_piece:(p + 1) * per_piece]:
            conv_chunk(cc, rc, zero)
    hc = _layer_norm(conv_scr[...] + cb_ref[...], clg_ref[...], clb_ref[...])
    hc_ref[...] = (hc * _sigmoid(hc)).astype(BF16)
    u_scr[cur] = o_ref[:, u_col:u_col + 2 * c]


def _project_and_conv(xb, w, conv, seq, u_col):
    n, d = xb.shape
    cols = w.shape[1]
    c = d // 2
    tm = PROJ_TILE
    nt = n // tm

    def early(s):
        return jnp.minimum(s, nt - 1)

    def late(s):
        return jnp.maximum(s - 1, 0)

    def const(shape):
        return pl.BlockSpec(shape, lambda s: (0,) * len(shape))

    return pl.pallas_call(
        functools.partial(_proj_conv_kernel, tm=tm, nt=nt, seq=seq, u_col=u_col),
        grid=(nt + 1,),
        in_specs=[pl.BlockSpec((tm, d), lambda s: (early(s), 0)), const((d, cols))]
        + [const(conv[k].shape) for k in ("cw", "cb", "clg", "clb")],
        out_specs=[pl.BlockSpec((tm, cols), lambda s: (early(s), 0)),
                   pl.BlockSpec((tm, c), lambda s: (late(s), 0))],
        out_shape=[jax.ShapeDtypeStruct((n, cols), BF16), jax.ShapeDtypeStruct((n, c), BF16)],
        scratch_shapes=[pltpu.VMEM((2, tm, 2 * c), BF16),
                        pltpu.VMEM((CONV_HALO + tm + CONV_TAIL, c), F32),
                        pltpu.VMEM((8, (tm // CONV_ROWS) * (CONV_ROWS + 8), 128), F32),
                        pltpu.VMEM((tm, c), F32)],
        compiler_params=_params("arbitrary"),
        name="proj_natural",
    )(xb, w, *[conv[k] for k in ("cw", "cb", "clg", "clb")])


def _project_dilated(xv, w, batch, seq, r):
    d, c = w.shape
    n = xv.shape[0] * r
    sub = seq // r
    tm = min(PROJ_ROWS, sub)
    nl = sub // tm
    cps = max(1, min(r, PROJ_ROWS // sub))
    assert r % cps == 0

    def mm_classes(x_ref, w_ref, o_ref):
        for k in range(cps):
            o_ref[k * tm:(k + 1) * tm, :] = jnp.dot(x_ref[:, k * d:(k + 1) * d], w_ref[...],
                                                    preferred_element_type=F32).astype(o_ref.dtype)

    return pl.pallas_call(
        mm_classes,
        grid=(batch, r // cps, nl),
        in_specs=[pl.BlockSpec((tm, cps * d), lambda b, cg, j: (b * nl + j, cg)),
                  pl.BlockSpec((d, c), lambda b, cg, j: (0, 0))],
        out_specs=pl.BlockSpec((cps * tm, c), lambda b, cg, j: ((b * (r // cps) + cg) * nl + j, 0)),
        out_shape=jax.ShapeDtypeStruct((n, c), BF16),
        compiler_params=_params("arbitrary", "arbitrary", "arbitrary"),
        name=f"proj_dilated_{r}",
    )(xv, w)


def _attn_kernel(q_ref, kp_ref, vp_ref, kc_ref, vc_ref, bias_ref, o_ref, lse_ref, k_scr, v_scr, *, nqb):
    i = pl.program_id(1)
    k_scr[0:BAND, :] = kp_ref[0]
    k_scr[BAND:, :] = kc_ref[0]
    v_scr[0:BAND, :] = vp_ref[0]
    v_scr[BAND:, :] = vc_ref[0]
    col = lax.broadcasted_iota(jnp.int32, (BAND, 2 * BAND), 1)
    first_head = lax.broadcasted_iota(jnp.int32, (BAND, 2 * HEAD_DIM), 1) < HEAD_DIM

    def body(j, carry):
        r0 = pl.multiple_of(j * BAND, BAND)
        q = q_ref[0, pl.ds(r0, BAND), :]
        kk = k_scr[pl.ds(r0, 2 * BAND), :]
        vv = v_scr[pl.ds(r0, 2 * BAND), :]
        no_prev = jnp.logical_and(jnp.logical_and(i == 0, j == 0), col < BAND)
        outs, lses = [], []
        for pair in range(HEADS_PER_GROUP // 2):
            sl = slice(pair * 2 * HEAD_DIM, (pair + 1) * 2 * HEAD_DIM)
            q2, k2, v2 = q[:, sl], kk[:, sl], vv[:, sl]
            o_pair, lse_pair = None, None
            for hh in range(2):
                mine = first_head if hh == 0 else jnp.logical_not(first_head)
                qm = jnp.where(mine, q2, jnp.zeros_like(q2))
                s = lax.dot_general(qm, k2, (((1,), (1,)), ((), ())), preferred_element_type=F32)
                s = s + bias_ref[2 * pair + hh]
                s = jnp.where(no_prev, MASKED, s)
                m = jnp.max(s, axis=-1, keepdims=True)
                p = jnp.exp(s - m)
                den = jnp.sum(p, axis=-1, keepdims=True)
                o = jnp.dot(p.astype(BF16), v2, preferred_element_type=F32) * (1.0 / den)
                lse = jnp.broadcast_to(m + jnp.log(den), (BAND, 2 * HEAD_DIM))
                o_pair = o if hh == 0 else jnp.where(first_head, o_pair, o)
                lse_pair = lse if hh == 0 else jnp.where(first_head, lse_pair, lse)
            outs.append(o_pair)
            lses.append(lse_pair)
        o_ref[0, pl.ds(r0, BAND), :] = jnp.concatenate(outs, axis=1).astype(o_ref.dtype)
        lse_ref[0, pl.ds(r0, BAND), :] = jnp.concatenate(lses, axis=1)
        return carry

    lax.fori_loop(0, nqb, body, 0, unroll=math.gcd(nqb, 8))


def _attention(qkv, bias, batch, seq, r, col0):
    sub = seq // r
    nqb = min(8, sub // BAND)
    rows = nqb * BAND
    nt = sub // rows
    gw = GROUP_WIDTH

    def cur(c):
        return pl.BlockSpec((1, rows, gw), lambda bc, i: (bc, i, c))

    def prev(c):
        return pl.BlockSpec((1, BAND, gw), lambda bc, i: (bc, jnp.maximum(i * nqb - 1, 0), c))

    out_spec = pl.BlockSpec((1, rows, gw), lambda bc, i: (bc // r, i, bc % r))
    o, lse = pl.pallas_call(
        functools.partial(_attn_kernel, nqb=nqb),
        grid=(batch * r, nt),
        in_specs=[cur(col0), prev(col0 + 1), prev(col0 + 2), cur(col0 + 1), cur(col0 + 2),
                  pl.BlockSpec((HEADS_PER_GROUP, BAND, 2 * BAND), lambda bc, i: (0, 0, 0))],
        out_specs=[out_spec, out_spec],
        out_shape=[jax.ShapeDtypeStruct((batch, sub, r * gw), BF16),
                   jax.ShapeDtypeStruct((batch, sub, r * gw), F32)],
        scratch_shapes=[pltpu.VMEM((rows + BAND, gw), BF16), pltpu.VMEM((rows + BAND, gw), BF16)],
        compiler_params=_params("arbitrary", "arbitrary"),
        name=f"attn_dilation_{r}",
    )(qkv, qkv, qkv, qkv, qkv, bias)
    return o.reshape(batch * sub, r * gw), lse.reshape(batch * sub, r * gw)


def _t5_bucket(dist):
    max_exact = N_BUCKETS // 2
    n = jnp.maximum(dist, 0)
    nf = jnp.maximum(n, 1).astype(F32)
    large = max_exact + (jnp.log(nf / max_exact) / math.log(MAX_DISTANCE / max_exact)
                         * (N_BUCKETS - max_exact)).astype(jnp.int32)
    large = jnp.minimum(large, N_BUCKETS - 1)
    return jnp.where(n < max_exact, n, large)


def _band_bias(rel_bias, gi, r):
    qi = jnp.arange(BAND)[:, None]
    kj = jnp.arange(2 * BAND)[None, :]
    dist = qi + BAND - kj
    bucket = _t5_bucket(dist * r)
    valid = (dist >= 0) & (dist <= BAND)
    b = jnp.full((HEADS_PER_GROUP, BAND, 2 * BAND), MASKED, F32)
    for k in range(N_BUCKETS):
        row = rel_bias[k, gi * HEADS_PER_GROUP:(gi + 1) * HEADS_PER_GROUP].astype(F32)
        b = jnp.where(((bucket == k) & valid)[None], row[:, None, None], b)
    return b


def _post_kernel(o0_ref, o1_ref, o2_ref, l0_ref, l1_ref, l2_ref, hc_ref, g_ref, x_ref,
                 wao_ref, wco_ref, wo_ref, lng_ref, lnb_ref, rwt_ref,
                 x1_ref, x1b_ref, gate_ref, pos_ref, cnt_ref, nat_scr, *, tm, alpha):
    d = x_ref.shape[1]
    gw = GROUP_WIDTH

    def natural(ref, slot, r):
        if r == 1:
            return ref[...].astype(F32)
        nl = gw // 128
        for cls in range(r):
            for j in range(nl):
                lanes = slice(cls * gw + j * 128, cls * gw + (j + 1) * 128)
                nat_scr[slot * nl + j, pl.ds(cls, tm // r, stride=r), :] = ref[:, lanes].astype(F32)
        return jnp.concatenate([nat_scr[slot * nl + j] for j in range(nl)], axis=1)

    rs = [r for _, r in DILATION_GROUPS]
    o0, o1, o2 = (natural(ref, s, r) for s, (ref, r) in enumerate(zip((o0_ref, o1_ref, o2_ref), rs)))
    l0, l1, l2 = (natural(ref, 3 + s, r) for s, (ref, r) in enumerate(zip((l0_ref, l1_ref, l2_ref), rs)))
    mx = jnp.maximum(jnp.maximum(l0, l1), l2)
    e0, e1, e2 = jnp.exp(l0 - mx), jnp.exp(l1 - mx), jnp.exp(l2 - mx)
    attn = (e0 * o0 + e1 * o1 + e2 * o2) * (1.0 / (e0 + e1 + e2))
    attn_branch = jnp.dot(attn.astype(BF16), wao_ref[...], preferred_element_type=F32)

    conv_branch = jnp.dot(hc_ref[...], wco_ref[...], preferred_element_type=F32)

    gates = _sigmoid(g_ref[...].astype(F32))
    merged = gates[:, :d] * attn_branch + gates[:, d:] * conv_branch
    hmix = jnp.dot(merged.astype(BF16), wo_ref[...], preferred_element_type=F32)
    x1 = _layer_norm(alpha * x_ref[...] + hmix, lng_ref[...], lnb_ref[...])
    x1_ref[...] = x1
    x1b = x1.astype(BF16)
    x1b_ref[...] = x1b

    logits = lax.dot_general(rwt_ref[...], x1b, (((1,), (1,)), ((), ())), preferred_element_type=F32)
    ex = jnp.exp(logits - jnp.max(logits, axis=0, keepdims=True))
    rows = [ex[e:e + 1, :] for e in range(N_EXPERTS)]
    best, gsel = None, None
    for g in range(N_EXPERTS // EXPERTS_PER_GROUP):
        v = rows[g * EXPERTS_PER_GROUP:(g + 1) * EXPERTS_PER_GROUP]
        score = None
        for a in range(EXPERTS_PER_GROUP):
            for b in range(a + 1, EXPERTS_PER_GROUP):
                ps = v[a] + v[b]
                score = ps if score is None else jnp.maximum(score, ps)
        if best is None:
            best, gsel = score, jnp.zeros_like(score, dtype=jnp.int32)
        else:
            upd = score > best
            gsel = jnp.where(upd, g, gsel)
            best = jnp.where(upd, score, best)
    vals = []
    for j in range(EXPERTS_PER_GROUP):
        vj = rows[j]
        for g in range(1, N_EXPERTS // EXPERTS_PER_GROUP):
            vj = jnp.where(gsel == g, rows[g * EXPERTS_PER_GROUP + j], vj)
        vals.append(vj)
    v1, i1 = vals[0], jnp.zeros_like(gsel)
    for j in range(1, EXPERTS_PER_GROUP):
        upd = vals[j] > v1
        i1 = jnp.where(upd, j, i1)
        v1 = jnp.where(upd, vals[j], v1)
    v2, i2 = jnp.full_like(v1, -1.0), jnp.zeros_like(gsel)
    for j in range(EXPERTS_PER_GROUP):
        upd = jnp.logical_and(i1 != j, vals[j] > v2)
        i2 = jnp.where(upd, j, i2)
        v2 = jnp.where(upd, vals[j], v2)
    ea = gsel * EXPERTS_PER_GROUP + i1
    eb = gsel * EXPERTS_PER_GROUP + i2
    inv = 1.0 / (v1 + v2)
    gate_ref[...] = jnp.concatenate([v1 * inv, v2 * inv], axis=0)

    eid = lax.broadcasted_iota(jnp.int32, (N_EXPERTS, tm), 0)
    hit_a = eid == ea
    hit_b = eid == eb
    onehot = jnp.where(jnp.logical_or(hit_a, hit_b), 1.0, 0.0)
    before = (lax.broadcasted_iota(jnp.int32, (tm, tm), 0)
              < lax.broadcasted_iota(jnp.int32, (tm, tm), 1))
    upper = jnp.where(before, 1.0, 0.0).astype(BF16)
    rank = jnp.dot(onehot.astype(BF16), upper, preferred_element_type=F32)
    count = jnp.sum(onehot, axis=1, keepdims=True)
    seg_len = jnp.floor((count + (SEG_ALIGN - 1)) * (1.0 / SEG_ALIGN)) * SEG_ALIGN
    lower = (lax.broadcasted_iota(jnp.int32, (N_EXPERTS, N_EXPERTS), 1)
             < lax.broadcasted_iota(jnp.int32, (N_EXPERTS, N_EXPERTS), 0))
    seg_off = jnp.dot(jnp.where(lower, 1.0, 0.0).astype(BF16),
                      jnp.broadcast_to(seg_len, (N_EXPERTS, 128)).astype(BF16),
                      preferred_element_type=F32)[:, 0:1]
    row = rank + seg_off
    pa = jnp.sum(jnp.where(hit_a, row, 0.0), axis=0, keepdims=True)
    pb = jnp.sum(jnp.where(hit_b, row, 0.0), axis=0, keepdims=True)
    pos_ref[...] = jnp.concatenate([pa, pb], axis=0).astype(jnp.int32)
    cnt_ref[0] = jnp.broadcast_to(count, (N_EXPERTS, 128))


def _mixer_tail(o, lse, proj0, hc, x, lw):
    n, d = x.shape
    tm = TOKEN_TILE
    nt = n // tm
    gw = GROUP_WIDTH

    def tok(width):
        return pl.BlockSpec((tm, width), lambda i: (i, 0))

    def const(shape):
        return pl.BlockSpec(shape, lambda i: (0,) * len(shape))

    def grouped(r):
        return pl.BlockSpec((tm // r, r * gw), lambda i: (i, 0))

    rs = [r for _, r in DILATION_GROUPS]
    in_specs = (
        [grouped(r) for r in rs] * 2
        + [tok(d // 2),
           pl.BlockSpec((tm, 2 * d), lambda i: (i, 0)),
           tok(d)]
        + [const(lw[k].shape) for k in ("wao", "wco", "wo", "lng", "lnb", "rwt")]
    )
    lane_row = pl.BlockSpec((2, tm), lambda i: (0, i))
    out_specs = [tok(d), tok(d), lane_row, lane_row,
                 pl.BlockSpec((1, N_EXPERTS, 128), lambda i: (i, 0, 0))]
    out_shape = [jax.ShapeDtypeStruct((n, d), F32), jax.ShapeDtypeStruct((n, d), BF16),
                 jax.ShapeDtypeStruct((2, n), F32), jax.ShapeDtypeStruct((2, n), jnp.int32),
                 jax.ShapeDtypeStruct((nt, N_EXPERTS, 128), F32)]
    alpha = lw["alpha"]
    return pl.pallas_call(
        functools.partial(_post_kernel, tm=tm, alpha=alpha),
        grid=(nt,),
        in_specs=in_specs,
        out_specs=out_specs,
        out_shape=out_shape,
        scratch_shapes=[pltpu.VMEM((2 * N_GROUPS * (gw // 128), tm, 128), F32)],
        compiler_params=_params("arbitrary"),
        name="mixer_tail",
    )(o[0], o[1], o[2], lse[0], lse[1], lse[2], hc, proj0, x,
      *[lw[k] for k in ("wao", "wco", "wo", "lng", "lnb", "rwt")])


def _pack_halves(v):
    h = v.shape[1] // 2
    bits = lax.bitcast_convert_type(v, jnp.uint32)
    return (bits[:, :h] >> 16) | (bits[:, h:] & jnp.uint32(0xFFFF0000))


def _unpack_halves(w):
    lo = lax.bitcast_convert_type(w << 16, F32).astype(BF16)
    hi = lax.bitcast_convert_type(w & jnp.uint32(0xFFFF0000), F32).astype(BF16)
    return lo, hi


def _dispatch_kernel(dst_ref, off_ref, len_ref, tot_ref, fill_row_ref, fill_len_ref, nact_ref,
                     x_ref, pos_ref, buf_hbm, xs, zeros, sem, zsem, *, nt, nb, blk):
    t = pl.program_id(0)
    cur = t % 2
    srows = xs.shape[1]

    def tile_wait(tile, buf):
        n = pl.multiple_of(tot_ref[tile], SEG_ALIGN)
        pltpu.make_async_copy(xs.at[buf, pl.ds(0, n)], buf_hbm.at[pl.ds(0, n)], sem.at[buf]).wait()

    @pl.when(t >= 2)
    def _():
        tile_wait(t - 2, cur)

    pos = pos_ref[...]
    row = lax.broadcasted_iota(jnp.int32, (srows, pos.shape[1]), 0)
    hit = jnp.logical_or(row == pos[0:1, :], row == pos[1:2, :])
    sel = jnp.where(hit, 1.0, 0.0).astype(BF16)
    xs[cur] = _pack_halves(jnp.dot(sel, x_ref[...], preferred_element_type=F32))

    for e in range(N_EXPERTS):
        k = t * N_EXPERTS + e
        n = pl.multiple_of(len_ref[k], SEG_ALIGN)
        src = pl.multiple_of(off_ref[k], SEG_ALIGN)
        dst = pl.multiple_of(dst_ref[k], SEG_ALIGN)

        @pl.when(n > 0)
        def _():
            pltpu.make_async_copy(xs.at[cur, pl.ds(src, n)], buf_hbm.at[pl.ds(dst, n)], sem.at[cur]).start()

    @pl.when(t == nt - 1)
    def _():
        if nt >= 2:
            tile_wait(t - 1, 1 - cur)
        tile_wait(t, cur)
        zeros[...] = jnp.zeros_like(zeros)

        def region_fill(e):
            n = pl.multiple_of(fill_len_ref[e], SEG_ALIGN)
            dst = pl.multiple_of(fill_row_ref[e], SEG_ALIGN)
            return n, pltpu.make_async_copy(zeros.at[pl.ds(0, n)], buf_hbm.at[pl.ds(dst, n)], zsem)

        def block_fill(b):
            return pltpu.make_async_copy(zeros, buf_hbm.at[pl.ds(pl.multiple_of(b * blk, blk), blk)], zsem)

        def each_block(fn):
            def body(b, carry):
                fn(block_fill(b))
                return carry
            lax.fori_loop(nact_ref[0], nb, body, 0)

        for e in range(N_EXPERTS):
            n, cp = region_fill(e)
            pl.when(n > 0)(cp.start)
        each_block(lambda cp: cp.start())
        for e in range(N_EXPERTS):
            n, cp = region_fill(e)
            pl.when(n > 0)(cp.wait)
        each_block(lambda cp: cp.wait())


def _dispatch(x1b, pos, tables, nb):
    n, d = x1b.shape
    tm = TOKEN_TILE
    nt = n // tm
    blk = ROW_BLOCK
    grid_spec = pltpu.PrefetchScalarGridSpec(
        num_scalar_prefetch=7,
        grid=(nt,),
        in_specs=[pl.BlockSpec((tm, d), lambda t, *_: (t, 0)),
                  pl.BlockSpec((2, tm), lambda t, *_: (0, t))],
        out_specs=pl.BlockSpec(memory_space=pl.ANY),
        scratch_shapes=[pltpu.VMEM((2, SORTED_ROWS, d // 2), jnp.uint32), pltpu.VMEM((blk, d // 2), jnp.uint32),
                        pltpu.SemaphoreType.DMA((2,)), pltpu.SemaphoreType.DMA],
    )
    return pl.pallas_call(
        functools.partial(_dispatch_kernel, nt=nt, nb=nb, blk=blk),
        grid_spec=grid_spec,
        out_shape=jax.ShapeDtypeStruct((nb * blk, d // 2), jnp.uint32),
        compiler_params=_params("arbitrary"),
        name="expert_dispatch",
    )(tables["dst"], tables["off"], tables["len"], tables["tot"], tables["fill_row"], tables["fill_len"],
      tables["n_act"], x1b, pos)


def _ffn_kernel(first_ref, count_ref, nact_ref, xs_hbm, wgu_hbm, wdn_hbm, y_hbm,
                wgu_f, wdn_f, wgu_b, wdn_b, xbuf, obuf, h_scr, wsem, isem, osem, zsem, *, nb, blk, layer):
    e = pl.program_id(0)
    n_exp = pl.num_programs(0)
    f = wdn_f.shape[1]
    half = xbuf.shape[2]
    b0 = first_ref[e]
    n_blocks = count_ref[e]

    def weights(ex, slot):
        return (pltpu.make_async_copy(wgu_hbm.at[layer, ex], wgu_f.at[slot], wsem.at[slot]),
                pltpu.make_async_copy(wdn_hbm.at[layer, ex], wdn_f.at[slot], wsem.at[slot]))

    def fetch_weights(ex):
        for cp in weights(ex, ex % 2):
            cp.start(priority=WEIGHT_DMA_PRIORITY)

    @pl.when(e == 0)
    def _():
        fetch_weights(e)

    for cp in weights(e, e % 2):
        cp.wait()

    @pl.when(e + 1 < n_exp)
    def _():
        fetch_weights(e + 1)

    def rows(j):
        return pl.ds(pl.multiple_of((b0 + j) * blk, blk), blk)

    def load(j, buf):
        return pltpu.make_async_copy(xs_hbm.at[rows(j)], xbuf.at[buf], isem.at[buf])

    def store(j, buf):
        return pltpu.make_async_copy(obuf.at[buf], y_hbm.at[rows(j)], osem.at[buf])

    def prefetch(j):
        @pl.when(j < n_blocks)
        def _():
            load(j, j % 2).start()

    def up(j):
        buf = j % 2
        x_lo, x_hi = _unpack_halves(xbuf[buf])
        h_scr[buf] = (jnp.dot(x_lo, wgu_b[0:half, :], preferred_element_type=F32)
                      + jnp.dot(x_hi, wgu_b[half:, :], preferred_element_type=F32))

    def release(j):
        @pl.when(j >= 2)
        def _():
            store(j - 2, j % 2).wait()

    def down(j):
        buf = j % 2
        a = h_scr[buf, :, 0:f]
        act = (a * _sigmoid(a) * h_scr[buf, :, f:]).astype(BF16)
        y = jnp.dot(act, wdn_b[...], preferred_element_type=F32)
        obuf[buf] = _pack_halves(y.astype(BF16).astype(F32))
        store(j, buf).start()

    @pl.when(n_blocks > 0)
    def _():
        load(0, 0).start()
        wgu_b[...] = wgu_f[e % 2].astype(BF16)
        wdn_b[...] = wdn_f[e % 2].astype(BF16)
        prefetch(1)
        load(0, 0).wait()
        up(0)

    def block(j, carry):
        prefetch(j + 1)
        release(j - 1)
        load(j, j % 2).wait()
        down(j - 1)
        up(j)
        return carry

    lax.fori_loop(1, n_blocks, block, 0)

    @pl.when(n_blocks > 0)
    def _():
        release(n_blocks - 1)
        down(n_blocks - 1)

    @pl.when(n_blocks >= 2)
    def _():
        store(n_blocks - 2, n_blocks % 2).wait()

    @pl.when(n_blocks >= 1)
    def _():
        store(n_blocks - 1, (n_blocks - 1) % 2).wait()

    @pl.when(e == pl.num_programs(0) - 1)
    def _():
        xbuf[0] = jnp.zeros((blk, half), jnp.uint32)

        def fill(b):
            return pltpu.make_async_copy(xbuf.at[0], y_hbm.at[pl.ds(pl.multiple_of(b * blk, blk), blk)], zsem)

        def each(fn):
            def body(b, carry):
                fn(fill(b))
                return carry
            lax.fori_loop(nact_ref[0], nb, body, 0)

        each(lambda cp: cp.start())
        each(lambda cp: cp.wait())


def _expert_ffn(buf, wgu, wdn, layer, tables):
    n_rows, half = buf.shape
    d = 2 * half
    blk = ROW_BLOCK
    nb = n_rows // blk
    f2 = wgu.shape[3]
    f = wdn.shape[2]
    grid_spec = pltpu.PrefetchScalarGridSpec(
        num_scalar_prefetch=3,
        grid=(N_EXPERTS,),
        in_specs=[pl.BlockSpec(memory_space=pl.ANY), pl.BlockSpec(memory_space=pl.ANY),
                  pl.BlockSpec(memory_space=pl.ANY)],
        out_specs=pl.BlockSpec(memory_space=pl.ANY),
        scratch_shapes=[pltpu.VMEM((2, d, f2), F32), pltpu.VMEM((2, f, d), F32),
                        pltpu.VMEM((d, f2), BF16), pltpu.VMEM((f, d), BF16),
                        pltpu.VMEM((2, blk, half), jnp.uint32), pltpu.VMEM((2, blk, half), jnp.uint32),
                        pltpu.VMEM((2, blk, f2), F32),
                        pltpu.SemaphoreType.DMA((2,)), pltpu.SemaphoreType.DMA((2,)),
                        pltpu.SemaphoreType.DMA((2,)), pltpu.SemaphoreType.DMA],
    )
    return pl.pallas_call(
        functools.partial(_ffn_kernel, nb=nb, blk=blk, layer=layer),
        grid_spec=grid_spec,
        out_shape=jax.ShapeDtypeStruct((n_rows, half), jnp.uint32),
        compiler_params=_params("arbitrary"),
        name="expert_ffn",
    )(tables["first_block"], tables["n_blocks"], tables["n_act"], buf, wgu, wdn)


def _store_class_views(xt, scr, view_refs, views):
    tm, d = xt.shape
    if views:
        for j in range(d // 128):
            scr[j] = xt[:, j * 128:(j + 1) * 128]
    for ref, r in zip(view_refs, views):
        for cls in range(r):
            for j in range(d // 128):
                lanes = slice(cls * d + j * 128, cls * d + (j + 1) * 128)
                ref[:, lanes] = scr[j, pl.ds(cls, tm // r, stride=r), :].astype(BF16)


def _views_kernel(x_ref, xb_ref, *rest, views):
    view_refs, scr = rest[:len(views)], rest[len(views)]
    xt = x_ref[...]
    xb_ref[...] = xt.astype(BF16)
    _store_class_views(xt, scr, view_refs, views)


def _input_views(xf, views):
    n, d = xf.shape
    tm = TOKEN_TILE
    tok = pl.BlockSpec((tm, d), lambda t: (t, 0))
    return pl.pallas_call(
        functools.partial(_views_kernel, views=views),
        grid=(n // tm,),
        in_specs=[tok],
        out_specs=[tok] + [pl.BlockSpec((tm // r, r * d), lambda t: (t, 0)) for r in views],
        out_shape=[jax.ShapeDtypeStruct((n, d), BF16)]
        + [jax.ShapeDtypeStruct((n // r, r * d), BF16) for r in views],
        scratch_shapes=[pltpu.VMEM((d // 128, tm, 128), F32)],
        compiler_params=_params("arbitrary"),
        name="input_views",
    )(xf)


def _merge_kernel(dst_ref, off_ref, len_ref, tot_ref, y_hbm, x1_ref, pos_ref, gt_ref, g_ref, b_ref,
                  x2_ref, x2b_ref, *rest, nt, alpha, views):
    view_refs, (ys, x2_scr, sem) = rest[:len(views)], rest[len(views):]
    t = pl.program_id(0)
    cur = t % 2
    tm, d = x1_ref.shape
    srows = ys.shape[1]

    def fetch(tile, buf):
        for e in range(N_EXPERTS):
            k = tile * N_EXPERTS + e
            n = pl.multiple_of(len_ref[k], SEG_ALIGN)
            dst = pl.multiple_of(off_ref[k], SEG_ALIGN)
            src = pl.multiple_of(dst_ref[k], SEG_ALIGN)

            @pl.when(n > 0)
            def _():
                pltpu.make_async_copy(y_hbm.at[pl.ds(src, n)], ys.at[buf, pl.ds(dst, n)], sem.at[buf]).start()

    @pl.when(t == 0)
    def _():
        ys[...] = jnp.zeros_like(ys)
        fetch(t, cur)

    @pl.when(t + 1 < nt)
    def _():
        fetch(t + 1, 1 - cur)

    n_all = pl.multiple_of(tot_ref[t], SEG_ALIGN)
    pltpu.make_async_copy(y_hbm.at[pl.ds(0, n_all)], ys.at[cur, pl.ds(0, n_all)], sem.at[cur]).wait()

    y_lo, y_hi = _unpack_halves(ys[cur])
    pos = pos_ref[...]
    gt = gt_ref[...]
    lane = lax.broadcasted_iota(jnp.int32, (tm, srows), 1)
    sel = (jnp.where(lane == pos[:, 0:1], gt[:, 0:1], 0.0)
           + jnp.where(lane == pos[:, 1:2], gt[:, 1:2], 0.0)).astype(BF16)
    m = jnp.concatenate([jnp.dot(sel, y_lo, preferred_element_type=F32),
                         jnp.dot(sel, y_hi, preferred_element_type=F32)], axis=1)
    x2 = _layer_norm(alpha * x1_ref[...] + m, g_ref[...], b_ref[...])
    x2_ref[...] = x2
    x2b_ref[...] = x2.astype(BF16)
    _store_class_views(x2, x2_scr, view_refs, views)


def _expert_merge(y_buf, pos_t, gate_t, x1, g, b, alpha, tables, views):
    n, d = x1.shape
    tm = TOKEN_TILE
    nt = n // tm
    tok = pl.BlockSpec((tm, d), lambda t, *_: (t, 0))
    pair = pl.BlockSpec((tm, 2), lambda t, *_: (t, 0))
    vec = pl.BlockSpec((1, d), lambda t, *_: (0, 0))
    grid_spec = pltpu.PrefetchScalarGridSpec(
        num_scalar_prefetch=4,
        grid=(nt,),
        in_specs=[pl.BlockSpec(memory_space=pl.ANY), tok, pair, pair, vec, vec],
        out_specs=[tok, tok] + [pl.BlockSpec((tm // r, r * d), lambda t, *_: (t, 0)) for r in views],
        scratch_shapes=[pltpu.VMEM((2, SORTED_ROWS, d // 2), jnp.uint32), pltpu.VMEM((d // 128, tm, 128), F32),
                        pltpu.SemaphoreType.DMA((2,))],
    )
    return pl.pallas_call(
        functools.partial(_merge_kernel, nt=nt, alpha=alpha, views=views),
        grid_spec=grid_spec,
        out_shape=[jax.ShapeDtypeStruct((n, d), F32), jax.ShapeDtypeStruct((n, d), BF16)]
        + [jax.ShapeDtypeStruct((n // r, r * d), BF16) for r in views],
        compiler_params=_params("arbitrary"),
        name="expert_merge",
    )(tables["dst"], tables["off"], tables["len"], tables["tot"], y_buf, x1, pos_t, gate_t, g, b)


def _routing_tables(counts):
    blk = ROW_BLOCK
    nt = counts.shape[0]
    seg = (counts + SEG_ALIGN - 1) // SEG_ALIGN * SEG_ALIGN
    off = jnp.cumsum(seg, axis=1) - seg
    used = jnp.sum(seg, axis=0)
    region = (used + blk - 1) // blk * blk
    region_end = jnp.cumsum(region)
    region_start = region_end - region
    dst = region_start[None, :] + jnp.cumsum(seg, axis=0) - seg
    n_act = region_end[-1] // blk
    i32 = lambda v: v.astype(jnp.int32)
    return dict(dst=i32(dst.reshape(-1)), off=i32(off.reshape(-1)), len=i32(seg.reshape(-1)),
                tot=i32(jnp.sum(seg, axis=1)), fill_row=i32(region_start + used), fill_len=i32(region - used),
                n_act=i32(n_act.reshape(1)), first_block=i32(region_start // blk), n_blocks=i32(region // blk))


def kernel(x, w_in, w_attn_out, w_conv_out, w_o, conv_w, conv_b, conv_ln_g, conv_ln_b, ln_mix_g, ln_mix_b,
           expert_w_gate_up, expert_w_down, ln_ffn_g, ln_ffn_b, router_w, rel_bias):
    batch, seq, d = x.shape
    depth = w_in.shape[0]
    n = batch * seq
    aw, gw = ATTN_WIDTH, GROUP_WIDTH
    assert d % 256 == 0 and n % 1024 == 0 and seq % PROJ_TILE == 0
    for window, r in DILATION_GROUPS:
        assert window // r == BAND and seq % (r * BAND) == 0
    alpha = (2 * depth) ** 0.25
    scale = HEAD_DIM ** -0.5

    biases = [_band_bias(rel_bias, gi, r) for gi, (_, r) in enumerate(DILATION_GROUPS)]
    rwt = router_w.T.astype(BF16)
    nt = n // TOKEN_TILE
    nb = -(-(2 * n + N_EXPERTS * (nt * (SEG_ALIGN - 1) + ROW_BLOCK - 1)) // ROW_BLOCK)
    dilations = tuple(r for _, r in DILATION_GROUPS if r > 1)

    xf = x.reshape(n, d)
    xb, *xv = _input_views(xf, dilations)
    xviews = dict(zip(dilations, xv))
    for l in range(depth):
        wl = w_in[l]

        def qkv_cols(gi):
            return jnp.concatenate([wl[:, gi * gw:(gi + 1) * gw] * scale,
                                    wl[:, aw + gi * gw:aw + (gi + 1) * gw],
                                    wl[:, 2 * aw + gi * gw:2 * aw + (gi + 1) * gw]], axis=1)

        w0 = jnp.concatenate([wl[:, 3 * aw + d:], wl[:, 3 * aw:3 * aw + d], qkv_cols(0)], axis=1).astype(BF16)
        conv = dict(cw=conv_w[l], cb=conv_b[l][None], clg=conv_ln_g[l][None], clb=conv_ln_b[l][None])
        proj0, hc = _project_and_conv(xb, w0, conv, seq, 2 * d)
        outs, lses = [], []
        for gi, (_, r) in enumerate(DILATION_GROUPS):
            if r == 1:
                qkv, col0 = proj0.reshape(batch, seq, proj0.shape[1]), (3 * d) // gw
            else:
                qkv = _project_dilated(xviews[r], qkv_cols(gi).astype(BF16), batch, seq, r)
                qkv, col0 = qkv.reshape(batch * r, seq // r, 3 * gw), 0
            o, lse = _attention(qkv, biases[gi], batch, seq, r, col0)
            outs.append(o)
            lses.append(lse)

        lw = dict(wao=w_attn_out[l].astype(BF16), wco=w_conv_out[l].astype(BF16), wo=w_o[l].astype(BF16),
                  lng=ln_mix_g[l][None], lnb=ln_mix_b[l][None], rwt=rwt, alpha=alpha)
        x1, x1b, gate, pos, cnt = _mixer_tail(outs, lses, proj0, hc, xf, lw)

        tables = _routing_tables(cnt[:, :, 0].astype(jnp.int32))
        buf = _dispatch(x1b, pos, tables, nb)
        y_buf = _expert_ffn(buf, expert_w_gate_up, expert_w_down, l, tables)
        views = dilations if l + 1 < depth else ()
        xf, xb, *xv = _expert_merge(y_buf, pos.T, gate.T, x1, ln_ffn_g[l][None], ln_ffn_b[l][None], alpha,
                                    tables, views)
        xviews = dict(zip(views, xv))
    return xf.reshape(batch, seq, d)
```

```python
import functools
import math

import jax
import jax.numpy as jnp
from jax import lax
from jax.experimental import pallas as pl
from jax.experimental.pallas import tpu as pltpu

F32 = jnp.float32
BF16 = jnp.bfloat16

HEAD_DIM = 64
HEADS_PER_GROUP = 4
GROUP_WIDTH = HEADS_PER_GROUP * HEAD_DIM
DILATION_GROUPS = ((128, 1), (512, 4), (2048, 16))
N_GROUPS = len(DILATION_GROUPS)
ATTN_WIDTH = N_GROUPS * GROUP_WIDTH
BAND = 128
CONV_KERNEL = 31
CONV_HALO = 32
CONV_TAIL = 16
CONV_ROWS = 64
PROJ_TILE = 512
PROJ_PIECES = 5
N_BUCKETS = 32
MAX_DISTANCE = 2048
N_EXPERTS = 16
EXPERTS_PER_GROUP = 4
LN_EPS = 1e-5
MASKED = -1e30

WEIGHT_DMA_PRIORITY = 1
PROJ_ROWS = 2048
ROW_BLOCK = 256
TOKEN_TILE = 256
MIXER_TILE = 1024
SEG_ALIGN = 8
SORTED_ROWS = -(-(2 * TOKEN_TILE + N_EXPERTS * (SEG_ALIGN - 1)) // 128) * 128
VMEM_LIMIT = 56 * 1024 * 1024


def _sigmoid(v):
    return 1.0 / (1.0 + jnp.exp(-v))


def _layer_norm(z, g, b):
    mu = jnp.mean(z, axis=-1, keepdims=True)
    zc = z - mu
    var = jnp.mean(zc * zc, axis=-1, keepdims=True)
    return zc * lax.rsqrt(var + LN_EPS) * g + b


def _params(*sem):
    return pltpu.CompilerParams(dimension_semantics=sem, vmem_limit_bytes=VMEM_LIMIT)


def _proj_conv_kernel(x_ref, w_ref, cw_ref, cb_ref, clg_ref, clb_ref, o_ref, hc_ref,
                      u_scr, hs_scr, ys_scr, conv_scr, *, tm, nt, seq, u_col):
    s = pl.program_id(0)
    c = hc_ref.shape[1]
    prev, cur = (s + 1) % 2, s % 2

    @pl.when(s == 0)
    def _():
        u_scr[1] = jnp.zeros(u_scr.shape[1:], BF16)
        hs_scr[...] = jnp.zeros(hs_scr.shape, F32)

    u = u_scr[prev].astype(F32)
    seq_start = (jnp.maximum(s - 1, 0) * tm) % seq == 0
    hs_scr[0:CONV_HALO, :] = jnp.where(seq_start, 0.0, hs_scr[tm:tm + CONV_HALO, :])
    hs_scr[CONV_HALO:CONV_HALO + tm, :] = u[:, :c] * _sigmoid(u[:, c:])

    off = CONV_HALO - (CONV_KERNEL - 1)
    rows_y = CONV_ROWS + 8
    def conv_chunk(cc, rc, zero):
        ls = slice(cc * 128, (cc + 1) * 128)
        r0 = rc * CONV_ROWS
        acc = None
        for k in range(8):
            y = None
            for a in range((off + CONV_KERNEL + 7) // 8):
                tap = 8 * a + k - off
                if 0 <= tap < CONV_KERNEL:
                    term = hs_scr[r0 + 8 * a:r0 + 8 * a + rows_y, ls] * cw_ref[tap:tap + 1, ls]
                    y = term if y is None else y + term
            if k == 0:
                acc = y[0:CONV_ROWS] + zero
            else:
                ys_scr[k, rc * rows_y:(rc + 1) * rows_y, :] = y
                acc = acc + ys_scr[k, rc * rows_y + k:rc * rows_y + k + CONV_ROWS, :]
        conv_scr[r0:r0 + CONV_ROWS, ls] = acc

    cols = o_ref.shape[1]
    pieces = max(p for p in range(1, PROJ_PIECES + 1) if (cols // 128) % p == 0)
    pw = cols // pieces
    chunks = [(cc, rc) for cc in range(c // 128) for rc in range(tm // CONV_ROWS)]
    per_piece = -(-len(chunks) // pieces)
    for p in range(pieces):
        piece = jnp.dot(x_ref[...], w_ref[:, p * pw:(p + 1) * pw], preferred_element_type=F32)
        o_ref[:, p * pw:(p + 1) * pw] = piece.astype(o_ref.dtype)
        bits = lax.bitcast_convert_type(piece[0:8, 0:128], jnp.uint32)
        zero8 = lax.bitcast_convert_type((bits >> 16) >> 16, F32)
        zero = jnp.concatenate([zero8] * (CONV_ROWS // 8), axis=0)
        for cc, rc in chunks[p * per_piece:(p + 1) * per_piece]:
            conv_chunk(cc, rc, zero)
    hc = _layer_norm(conv_scr[...] + cb_ref[...], clg_ref[...], clb_ref[...])
    hc_ref[...] = (hc * _sigmoid(hc)).astype(BF16)
    u_scr[cur] = o_ref[:, u_col:u_col + 2 * c]


def _project_and_conv(xb, w, conv, seq, u_col):
    n, d = xb.shape
    cols = w.shape[1]
    c = d // 2
    tm = PROJ_TILE
    nt = n // tm

    def early(s):
        return jnp.minimum(s, nt - 1)

    def late(s):
        return jnp.maximum(s - 1, 0)

    def const(shape):
        return pl.BlockSpec(shape, lambda s: (0,) * len(shape))

    return pl.pallas_call(
        functools.partial(_proj_conv_kernel, tm=tm, nt=nt, seq=seq, u_col=u_col),
        grid=(nt + 1,),
        in_specs=[pl.BlockSpec((tm, d), lambda s: (early(s), 0)), const((d, cols))]
        + [const(conv[k].shape) for k in ("cw", "cb", "clg", "clb")],
        out_specs=[pl.BlockSpec((tm, cols), lambda s: (early(s), 0)),
                   pl.BlockSpec((tm, c), lambda s: (late(s), 0))],
        out_shape=[jax.ShapeDtypeStruct((n, cols), BF16), jax.ShapeDtypeStruct((n, c), BF16)],
        scratch_shapes=[pltpu.VMEM((2, tm, 2 * c), BF16),
                        pltpu.VMEM((CONV_HALO + tm + CONV_TAIL, c), F32),
                        pltpu.VMEM((8, (tm // CONV_ROWS) * (CONV_ROWS + 8), 128), F32),
                        pltpu.VMEM((tm, c), F32)],
        compiler_params=_params("arbitrary"),
        name="proj_natural",
    )(xb, w, *[conv[k] for k in ("cw", "cb", "clg", "clb")])


def _project_dilated(xv, w, batch, seq, r):
    d, c = w.shape
    n = xv.shape[0] * r
    sub = seq // r
    tm = min(PROJ_ROWS, sub)
    nl = sub // tm
    cps = max(1, min(r, PROJ_ROWS // sub))
    assert r % cps == 0

    def mm_classes(x_ref, w_ref, o_ref):
        for k in range(cps):
            o_ref[k * tm:(k + 1) * tm, :] = jnp.dot(x_ref[:, k * d:(k + 1) * d], w_ref[...],
                                                    preferred_element_type=F32).astype(o_ref.dtype)

    return pl.pallas_call(
        mm_classes,
        grid=(batch, r // cps, nl),
        in_specs=[pl.BlockSpec((tm, cps * d), lambda b, cg, j: (b * nl + j, cg)),
                  pl.BlockSpec((d, c), lambda b, cg, j: (0, 0))],
        out_specs=pl.BlockSpec((cps * tm, c), lambda b, cg, j: ((b * (r // cps) + cg) * nl + j, 0)),
        out_shape=jax.ShapeDtypeStruct((n, c), BF16),
        compiler_params=_params("arbitrary", "arbitrary", "arbitrary"),
        name=f"proj_dilated_{r}",
    )(xv, w)


def _attn_kernel(q_ref, kp_ref, vp_ref, kc_ref, vc_ref, bias_ref, o_ref, lse_ref, k_scr, v_scr, *, nqb):
    i = pl.program_id(1)
    k_scr[0:BAND, :] = kp_ref[0]
    k_scr[BAND:, :] = kc_ref[0]
    v_scr[0:BAND, :] = vp_ref[0]
    v_scr[BAND:, :] = vc_ref[0]
    col = lax.broadcasted_iota(jnp.int32, (BAND, 2 * BAND), 1)
    first_head = lax.broadcasted_iota(jnp.int32, (BAND, 2 * HEAD_DIM), 1) < HEAD_DIM

    def body(j, carry):
        r0 = pl.multiple_of(j * BAND, BAND)
        q = q_ref[0, pl.ds(r0, BAND), :]
        kk = k_scr[pl.ds(r0, 2 * BAND), :]
        vv = v_scr[pl.ds(r0, 2 * BAND), :]
        no_prev = jnp.logical_and(jnp.logical_and(i == 0, j == 0), col < BAND)
        outs, lses = [], []
        for pair in range(HEADS_PER_GROUP // 2):
            sl = slice(pair * 2 * HEAD_DIM, (pair + 1) * 2 * HEAD_DIM)
            q2, k2, v2 = q[:, sl], kk[:, sl], vv[:, sl]
            o_pair, lse_pair = None, None
            for hh in range(2):
                mine = first_head if hh == 0 else jnp.logical_not(first_head)
                qm = jnp.where(mine, q2, jnp.zeros_like(q2))
                s = lax.dot_general(qm, k2, (((1,), (1,)), ((), ())), preferred_element_type=F32)
                s = s + bias_ref[2 * pair + hh]
                s = jnp.where(no_prev, MASKED, s)
                m = jnp.max(s, axis=-1, keepdims=True)
                p = jnp.exp(s - m)
                den = jnp.sum(p, axis=-1, keepdims=True)
                o = jnp.dot(p.astype(BF16), v2, preferred_element_type=F32) * (1.0 / den)
                lse = jnp.broadcast_to(m + jnp.log(den), (BAND, 2 * HEAD_DIM))
                o_pair = o if hh == 0 else jnp.where(first_head, o_pair, o)
                lse_pair = lse if hh == 0 else jnp.where(first_head, lse_pair, lse)
            outs.append(o_pair)
            lses.append(lse_pair)
        o_ref[0, pl.ds(r0, BAND), :] = jnp.concatenate(outs, axis=1).astype(o_ref.dtype)
        lse_ref[0, pl.ds(r0, BAND), :] = jnp.concatenate(lses, axis=1)
        return carry

    lax.fori_loop(0, nqb, body, 0, unroll=math.gcd(nqb, 8))


def _attention(qkv, bias, batch, seq, r, col0):
    sub = seq // r
    nqb = min(8, sub // BAND)
    rows = nqb * BAND
    nt = sub // rows
    gw = GROUP_WIDTH

    def cur(c):
        return pl.BlockSpec((1, rows, gw), lambda bc, i: (bc, i, c))

    def prev(c):
        return pl.BlockSpec((1, BAND, gw), lambda bc, i: (bc, jnp.maximum(i * nqb - 1, 0), c))

    out_spec = pl.BlockSpec((1, rows, gw), lambda bc, i: (bc // r, i, bc % r))
    o, lse = pl.pallas_call(
        functools.partial(_attn_kernel, nqb=nqb),
        grid=(batch * r, nt),
        in_specs=[cur(col0), prev(col0 + 1), prev(col0 + 2), cur(col0 + 1), cur(col0 + 2),
                  pl.BlockSpec((HEADS_PER_GROUP, BAND, 2 * BAND), lambda bc, i: (0, 0, 0))],
        out_specs=[out_spec, out_spec],
        out_shape=[jax.ShapeDtypeStruct((batch, sub, r * gw), BF16),
                   jax.ShapeDtypeStruct((batch, sub, r * gw), F32)],
        scratch_shapes=[pltpu.VMEM((rows + BAND, gw), BF16), pltpu.VMEM((rows + BAND, gw), BF16)],
        compiler_params=_params("arbitrary", "arbitrary"),
        name=f"attn_dilation_{r}",
    )(qkv, qkv, qkv, qkv, qkv, bias)
    return o.reshape(batch * sub, r * gw), lse.reshape(batch * sub, r * gw)


def _t5_bucket(dist):
    max_exact = N_BUCKETS // 2
    n = jnp.maximum(dist, 0)
    nf = jnp.maximum(n, 1).astype(F32)
    large = max_exact + (jnp.log(nf / max_exact) / math.log(MAX_DISTANCE / max_exact)
                         * (N_BUCKETS - max_exact)).astype(jnp.int32)
    large = jnp.minimum(large, N_BUCKETS - 1)
    return jnp.where(n < max_exact, n, large)


def _band_bias(rel_bias, gi, r):
    qi = jnp.arange(BAND)[:, None]
    kj = jnp.arange(2 * BAND)[None, :]
    dist = qi + BAND - kj
    bucket = _t5_bucket(dist * r)
    valid = (dist >= 0) & (dist <= BAND)
    b = jnp.full((HEADS_PER_GROUP, BAND, 2 * BAND), MASKED, F32)
    for k in range(N_BUCKETS):
        row = rel_bias[k, gi * HEADS_PER_GROUP:(gi + 1) * HEADS_PER_GROUP].astype(F32)
        b = jnp.where(((bucket == k) & valid)[None], row[:, None, None], b)
    return b


def _post_kernel(o0_ref, o1_ref, o2_ref, l0_ref, l1_ref, l2_ref, hc_ref, g_ref, x_ref,
                 wao_ref, wco_ref, wo_ref, lng_ref, lnb_ref, rwt_ref,
                 x1_ref, x1b_ref, gate_ref, pos_ref, cnt_ref, nat_scr, *, tm, alpha):
    d = x_ref.shape[1]
    gw = GROUP_WIDTH

    def natural(ref, slot, r):
        if r == 1:
            return ref[...].astype(F32)
        nl = gw // 128
        for cls in range(r):
            for j in range(nl):
                lanes = slice(cls * gw + j * 128, cls * gw + (j + 1) * 128)
                nat_scr[slot * nl + j, pl.ds(cls, tm // r, stride=r), :] = ref[:, lanes].astype(F32)
        return jnp.concatenate([nat_scr[slot * nl + j] for j in range(nl)], axis=1)

    rs = [r for _, r in DILATION_GROUPS]
    o0, o1, o2 = (natural(ref, s, r) for s, (ref, r) in enumerate(zip((o0_ref, o1_ref, o2_ref), rs)))
    l0, l1, l2 = (natural(ref, 3 + s, r) for s, (ref, r) in enumerate(zip((l0_ref, l1_ref, l2_ref), rs)))
    mx = jnp.maximum(jnp.maximum(l0, l1), l2)
    e0, e1, e2 = jnp.exp(l0 - mx), jnp.exp(l1 - mx), jnp.exp(l2 - mx)
    attn = (e0 * o0 + e1 * o1 + e2 * o2) * (1.0 / (e0 + e1 + e2))
    attn_branch = jnp.dot(attn.astype(BF16), wao_ref[...], preferred_element_type=F32)

    conv_branch = jnp.dot(hc_ref[...], wco_ref[...], preferred_element_type=F32)

    gates = _sigmoid(g_ref[...].astype(F32))
    merged = gates[:, :d] * attn_branch + gates[:, d:] * conv_branch
    hmix = jnp.dot(merged.astype(BF16), wo_ref[...], preferred_element_type=F32)
    x1 = _layer_norm(alpha * x_ref[...] + hmix, lng_ref[...], lnb_ref[...])
    x1_ref[...] = x1
    x1b = x1.astype(BF16)
    x1b_ref[...] = x1b

    logits = lax.dot_general(rwt_ref[...], x1b, (((1,), (1,)), ((), ())), preferred_element_type=F32)
    ex = jnp.exp(logits - jnp.max(logits, axis=0, keepdims=True))
    rows = [ex[e:e + 1, :] for e in range(N_EXPERTS)]
    best, gsel = None, None
    for g in range(N_EXPERTS // EXPERTS_PER_GROUP):
        v = rows[g * EXPERTS_PER_GROUP:(g + 1) * EXPERTS_PER_GROUP]
        score = None
        for a in range(EXPERTS_PER_GROUP):
            for b in range(a + 1, EXPERTS_PER_GROUP):
                ps = v[a] + v[b]
                score = ps if score is None else jnp.maximum(score, ps)
        if best is None:
            best, gsel = score, jnp.zeros_like(score, dtype=jnp.int32)
        else:
            upd = score > best
            gsel = jnp.where(upd, g, gsel)
            best = jnp.where(upd, score, best)
    vals = []
    for j in range(EXPERTS_PER_GROUP):
        vj = rows[j]
        for g in range(1, N_EXPERTS // EXPERTS_PER_GROUP):
            vj = jnp.where(gsel == g, rows[g * EXPERTS_PER_GROUP + j], vj)
        vals.append(vj)
    v1, i1 = vals[0], jnp.zeros_like(gsel)
    for j in range(1, EXPERTS_PER_GROUP):
        upd = vals[j] > v1
        i1 = jnp.where(upd, j, i1)
        v1 = jnp.where(upd, vals[j], v1)
    v2, i2 = jnp.full_like(v1, -1.0), jnp.zeros_like(gsel)
    for j in range(EXPERTS_PER_GROUP):
        upd = jnp.logical_and(i1 != j, vals[j] > v2)
        i2 = jnp.where(upd, j, i2)
        v2 = jnp.where(upd, vals[j], v2)
    ea = gsel * EXPERTS_PER_GROUP + i1
    eb = gsel * EXPERTS_PER_GROUP + i2
    inv = 1.0 / (v1 + v2)
    gate_ref[...] = jnp.concatenate([v1 * inv, v2 * inv], axis=0)

    st = TOKEN_TILE
    eid = lax.broadcasted_iota(jnp.int32, (N_EXPERTS, st), 0)
    before = (lax.broadcasted_iota(jnp.int32, (st, st), 0)
              < lax.broadcasted_iota(jnp.int32, (st, st), 1))
    upper = jnp.where(before, 1.0, 0.0).astype(BF16)
    lower = (lax.broadcasted_iota(jnp.int32, (N_EXPERTS, N_EXPERTS), 1)
             < lax.broadcasted_iota(jnp.int32, (N_EXPERTS, N_EXPERTS), 0))
    lower = jnp.where(lower, 1.0, 0.0).astype(BF16)
    for sub in range(tm // st):
        lanes = slice(sub * st, (sub + 1) * st)
        hit_a = eid == ea[:, lanes]
        hit_b = eid == eb[:, lanes]
        onehot = jnp.where(jnp.logical_or(hit_a, hit_b), 1.0, 0.0)
        rank = jnp.dot(onehot.astype(BF16), upper, preferred_element_type=F32)
        count = jnp.sum(onehot, axis=1, keepdims=True)
        seg_len = jnp.floor((count + (SEG_ALIGN - 1)) * (1.0 / SEG_ALIGN)) * SEG_ALIGN
        seg_off = jnp.dot(lower, jnp.broadcast_to(seg_len, (N_EXPERTS, 128)).astype(BF16),
                          preferred_element_type=F32)[:, 0:1]
        row = rank + seg_off
        pa = jnp.sum(jnp.where(hit_a, row, 0.0), axis=0, keepdims=True)
        pb = jnp.sum(jnp.where(hit_b, row, 0.0), axis=0, keepdims=True)
        pos_ref[:, lanes] = jnp.concatenate([pa, pb], axis=0).astype(jnp.int32)
        cnt_ref[sub] = jnp.broadcast_to(count, (N_EXPERTS, 128))


def _mixer_tail(o, lse, proj0, hc, x, lw):
    n, d = x.shape
    tm = MIXER_TILE
    nt = n // tm
    sub = tm // TOKEN_TILE
    gw = GROUP_WIDTH

    def tok(width):
        return pl.BlockSpec((tm, width), lambda i: (i, 0))

    def const(shape):
        return pl.BlockSpec(shape, lambda i: (0,) * len(shape))

    def grouped(r):
        return pl.BlockSpec((tm // r, r * gw), lambda i: (i, 0))

    rs = [r for _, r in DILATION_GROUPS]
    in_specs = (
        [grouped(r) for r in rs] * 2
        + [tok(d // 2),
           pl.BlockSpec((tm, 2 * d), lambda i: (i, 0)),
           tok(d)]
        + [const(lw[k].shape) for k in ("wao", "wco", "wo", "lng", "lnb", "rwt")]
    )
    lane_row = pl.BlockSpec((2, tm), lambda i: (0, i))
    out_specs = [tok(d), tok(d), lane_row, lane_row,
                 pl.BlockSpec((sub, N_EXPERTS, 128), lambda i: (i, 0, 0))]
    out_shape = [jax.ShapeDtypeStruct((n, d), F32), jax.ShapeDtypeStruct((n, d), BF16),
                 jax.ShapeDtypeStruct((2, n), F32), jax.ShapeDtypeStruct((2, n), jnp.int32),
                 jax.ShapeDtypeStruct((n // TOKEN_TILE, N_EXPERTS, 128), F32)]
    alpha = lw["alpha"]
    return pl.pallas_call(
        functools.partial(_post_kernel, tm=tm, alpha=alpha),
        grid=(nt,),
        in_specs=in_specs,
        out_specs=out_specs,
        out_shape=out_shape,
        scratch_shapes=[pltpu.VMEM((2 * N_GROUPS * (gw // 128), tm, 128), F32)],
        compiler_params=_params("arbitrary"),
        name="mixer_tail",
    )(o[0], o[1], o[2], lse[0], lse[1], lse[2], hc, proj0, x,
      *[lw[k] for k in ("wao", "wco", "wo", "lng", "lnb", "rwt")])


def _pack_halves(v):
    h = v.shape[1] // 2
    bits = lax.bitcast_convert_type(v, jnp.uint32)
    return (bits[:, :h] >> 16) | (bits[:, h:] & jnp.uint32(0xFFFF0000))


def _unpack_halves(w):
    lo = lax.bitcast_convert_type(w << 16, F32).astype(BF16)
    hi = lax.bitcast_convert_type(w & jnp.uint32(0xFFFF0000), F32).astype(BF16)
    return lo, hi


def _dispatch_kernel(dst_ref, off_ref, len_ref, tot_ref, fill_row_ref, fill_len_ref, nact_ref,
                     x_ref, pos_ref, buf_hbm, xs, zeros, sem, zsem, *, nt, nb, blk):
    t = pl.program_id(0)
    cur = t % 2
    srows = xs.shape[1]

    def tile_wait(tile, buf):
        n = pl.multiple_of(tot_ref[tile], SEG_ALIGN)
        pltpu.make_async_copy(xs.at[buf, pl.ds(0, n)], buf_hbm.at[pl.ds(0, n)], sem.at[buf]).wait()

    @pl.when(t >= 2)
    def _():
        tile_wait(t - 2, cur)

    pos = pos_ref[...]
    row = lax.broadcasted_iota(jnp.int32, (srows, pos.shape[1]), 0)
    hit = jnp.logical_or(row == pos[0:1, :], row == pos[1:2, :])
    sel = jnp.where(hit, 1.0, 0.0).astype(BF16)
    xs[cur] = _pack_halves(jnp.dot(sel, x_ref[...], preferred_element_type=F32))

    for e in range(N_EXPERTS):
        k = t * N_EXPERTS + e
        n = pl.multiple_of(len_ref[k], SEG_ALIGN)
        src = pl.multiple_of(off_ref[k], SEG_ALIGN)
        dst = pl.multiple_of(dst_ref[k], SEG_ALIGN)

        @pl.when(n > 0)
        def _():
            pltpu.make_async_copy(xs.at[cur, pl.ds(src, n)], buf_hbm.at[pl.ds(dst, n)], sem.at[cur]).start()

    @pl.when(t == nt - 1)
    def _():
        if nt >= 2:
            tile_wait(t - 1, 1 - cur)
        tile_wait(t, cur)
        zeros[...] = jnp.zeros_like(zeros)

        def region_fill(e):
            n = pl.multiple_of(fill_len_ref[e], SEG_ALIGN)
            dst = pl.multiple_of(fill_row_ref[e], SEG_ALIGN)
            return n, pltpu.make_async_copy(zeros.at[pl.ds(0, n)], buf_hbm.at[pl.ds(dst, n)], zsem)

        def block_fill(b):
            return pltpu.make_async_copy(zeros, buf_hbm.at[pl.ds(pl.multiple_of(b * blk, blk), blk)], zsem)

        def each_block(fn):
            def body(b, carry):
                fn(block_fill(b))
                return carry
            lax.fori_loop(nact_ref[0], nb, body, 0)

        for e in range(N_EXPERTS):
            n, cp = region_fill(e)
            pl.when(n > 0)(cp.start)
        each_block(lambda cp: cp.start())
        for e in range(N_EXPERTS):
            n, cp = region_fill(e)
            pl.when(n > 0)(cp.wait)
        each_block(lambda cp: cp.wait())


def _dispatch(x1b, pos, tables, nb):
    n, d = x1b.shape
    tm = TOKEN_TILE
    nt = n // tm
    blk = ROW_BLOCK
    grid_spec = pltpu.PrefetchScalarGridSpec(
        num_scalar_prefetch=7,
        grid=(nt,),
        in_specs=[pl.BlockSpec((tm, d), lambda t, *_: (t, 0)),
                  pl.BlockSpec((2, tm), lambda t, *_: (0, t))],
        out_specs=pl.BlockSpec(memory_space=pl.ANY),
        scratch_shapes=[pltpu.VMEM((2, SORTED_ROWS, d // 2), jnp.uint32), pltpu.VMEM((blk, d // 2), jnp.uint32),
                        pltpu.SemaphoreType.DMA((2,)), pltpu.SemaphoreType.DMA],
    )
    return pl.pallas_call(
        functools.partial(_dispatch_kernel, nt=nt, nb=nb, blk=blk),
        grid_spec=grid_spec,
        out_shape=jax.ShapeDtypeStruct((nb * blk, d // 2), jnp.uint32),
        compiler_params=_params("arbitrary"),
        name="expert_dispatch",
    )(tables["dst"], tables["off"], tables["len"], tables["tot"], tables["fill_row"], tables["fill_len"],
      tables["n_act"], x1b, pos)


def _ffn_kernel(first_ref, count_ref, nact_ref, xs_hbm, wgu_hbm, wdn_hbm, y_hbm,
                wgu_f, wdn_f, wgu_b, wdn_b, xbuf, obuf, h_scr, wsem, isem, osem, zsem, *, nb, blk, layer):
    e = pl.program_id(0)
    n_exp = pl.num_programs(0)
    f = wdn_f.shape[1]
    half = xbuf.shape[2]
    b0 = first_ref[e]
    n_blocks = count_ref[e]

    def weights(ex, slot):
        return (pltpu.make_async_copy(wgu_hbm.at[layer, ex], wgu_f.at[slot], wsem.at[slot]),
                pltpu.make_async_copy(wdn_hbm.at[layer, ex], wdn_f.at[slot], wsem.at[slot]))

    def fetch_weights(ex):
        for cp in weights(ex, ex % 2):
            cp.start(priority=WEIGHT_DMA_PRIORITY)

    @pl.when(e == 0)
    def _():
        fetch_weights(e)

    for cp in weights(e, e % 2):
        cp.wait()

    @pl.when(e + 1 < n_exp)
    def _():
        fetch_weights(e + 1)

    def rows(j):
        return pl.ds(pl.multiple_of((b0 + j) * blk, blk), blk)

    def load(j, buf):
        return pltpu.make_async_copy(xs_hbm.at[rows(j)], xbuf.at[buf], isem.at[buf])

    def store(j, buf):
        return pltpu.make_async_copy(obuf.at[buf], y_hbm.at[rows(j)], osem.at[buf])

    def prefetch(j):
        @pl.when(j < n_blocks)
        def _():
            load(j, j % 2).start()

    def up(j):
        buf = j % 2
        x_lo, x_hi = _unpack_halves(xbuf[buf])
        h_scr[buf] = (jnp.dot(x_lo, wgu_b[0:half, :], preferred_element_type=F32)
                      + jnp.dot(x_hi, wgu_b[half:, :], preferred_element_type=F32))

    def release(j):
        @pl.when(j >= 2)
        def _():
            store(j - 2, j % 2).wait()

    def down(j):
        buf = j % 2
        a = h_scr[buf, :, 0:f]
        act = (a * _sigmoid(a) * h_scr[buf, :, f:]).astype(BF16)
        y = jnp.dot(act, wdn_b[...], preferred_element_type=F32)
        obuf[buf] = _pack_halves(y.astype(BF16).astype(F32))
        store(j, buf).start()

    @pl.when(n_blocks > 0)
    def _():
        load(0, 0).start()
        wgu_b[...] = wgu_f[e % 2].astype(BF16)
        wdn_b[...] = wdn_f[e % 2].astype(BF16)
        prefetch(1)
        load(0, 0).wait()
        up(0)

    def block(j, carry):
        prefetch(j + 1)
        release(j - 1)
        load(j, j % 2).wait()
        down(j - 1)
        up(j)
        return carry

    lax.fori_loop(1, n_blocks, block, 0)

    @pl.when(n_blocks > 0)
    def _():
        release(n_blocks - 1)
        down(n_blocks - 1)

    @pl.when(n_blocks >= 2)
    def _():
        store(n_blocks - 2, n_blocks % 2).wait()

    @pl.when(n_blocks >= 1)
    def _():
        store(n_blocks - 1, (n_blocks - 1) % 2).wait()

    @pl.when(e == pl.num_programs(0) - 1)
    def _():
        xbuf[0] = jnp.zeros((blk, half), jnp.uint32)

        def fill(b):
            return pltpu.make_async_copy(xbuf.at[0], y_hbm.at[pl.ds(pl.multiple_of(b * blk, blk), blk)], zsem)

        def each(fn):
            def body(b, carry):
                fn(fill(b))
                return carry
            lax.fori_loop(nact_ref[0], nb, body, 0)

        each(lambda cp: cp.start())
        each(lambda cp: cp.wait())


def _expert_ffn(buf, wgu, wdn, layer, tables):
    n_rows, half = buf.shape
    d = 2 * half
    blk = ROW_BLOCK
    nb = n_rows // blk
    f2 = wgu.shape[3]
    f = wdn.shape[2]
    grid_spec = pltpu.PrefetchScalarGridSpec(
        num_scalar_prefetch=3,
        grid=(N_EXPERTS,),
        in_specs=[pl.BlockSpec(memory_space=pl.ANY), pl.BlockSpec(memory_space=pl.ANY),
                  pl.BlockSpec(memory_space=pl.ANY)],
        out_specs=pl.BlockSpec(memory_space=pl.ANY),
        scratch_shapes=[pltpu.VMEM((2, d, f2), F32), pltpu.VMEM((2, f, d), F32),
                        pltpu.VMEM((d, f2), BF16), pltpu.VMEM((f, d), BF16),
                        pltpu.VMEM((2, blk, half), jnp.uint32), pltpu.VMEM((2, blk, half), jnp.uint32),
                        pltpu.VMEM((2, blk, f2), F32),
                        pltpu.SemaphoreType.DMA((2,)), pltpu.SemaphoreType.DMA((2,)),
                        pltpu.SemaphoreType.DMA((2,)), pltpu.SemaphoreType.DMA],
    )
    return pl.pallas_call(
        functools.partial(_ffn_kernel, nb=nb, blk=blk, layer=layer),
        grid_spec=grid_spec,
        out_shape=jax.ShapeDtypeStruct((n_rows, half), jnp.uint32),
        compiler_params=_params("arbitrary"),
        name="expert_ffn",
    )(tables["first_block"], tables["n_blocks"], tables["n_act"], buf, wgu, wdn)


def _store_class_views(xt, scr, view_refs, views):
    tm, d = xt.shape
    if views:
        for j in range(d // 128):
            scr[j] = xt[:, j * 128:(j + 1) * 128]
    for ref, r in zip(view_refs, views):
        for cls in range(r):
            for j in range(d // 128):
                lanes = slice(cls * d + j * 128, cls * d + (j + 1) * 128)
                ref[:, lanes] = scr[j, pl.ds(cls, tm // r, stride=r), :].astype(BF16)


def _views_kernel(x_ref, xb_ref, *rest, views):
    view_refs, scr = rest[:len(views)], rest[len(views)]
    xt = x_ref[...]
    xb_ref[...] = xt.astype(BF16)
    _store_class_views(xt, scr, view_refs, views)


def _input_views(xf, views):
    n, d = xf.shape
    tm = TOKEN_TILE
    tok = pl.BlockSpec((tm, d), lambda t: (t, 0))
    return pl.pallas_call(
        functools.partial(_views_kernel, views=views),
        grid=(n // tm,),
        in_specs=[tok],
        out_specs=[tok] + [pl.BlockSpec((tm // r, r * d), lambda t: (t, 0)) for r in views],
        out_shape=[jax.ShapeDtypeStruct((n, d), BF16)]
        + [jax.ShapeDtypeStruct((n // r, r * d), BF16) for r in views],
        scratch_shapes=[pltpu.VMEM((d // 128, tm, 128), F32)],
        compiler_params=_params("arbitrary"),
        name="input_views",
    )(xf)


def _merge_kernel(dst_ref, off_ref, len_ref, tot_ref, y_hbm, x1_ref, pos_ref, gt_ref, g_ref, b_ref,
                  x2_ref, x2b_ref, *rest, nt, alpha, views):
    view_refs, (ys, x2_scr, sem) = rest[:len(views)], rest[len(views):]
    t = pl.program_id(0)
    cur = t % 2
    tm, d = x1_ref.shape
    srows = ys.shape[1]

    def fetch(tile, buf):
        for e in range(N_EXPERTS):
            k = tile * N_EXPERTS + e
            n = pl.multiple_of(len_ref[k], SEG_ALIGN)
            dst = pl.multiple_of(off_ref[k], SEG_ALIGN)
            src = pl.multiple_of(dst_ref[k], SEG_ALIGN)

            @pl.when(n > 0)
            def _():
                pltpu.make_async_copy(y_hbm.at[pl.ds(src, n)], ys.at[buf, pl.ds(dst, n)], sem.at[buf]).start()

    @pl.when(t == 0)
    def _():
        ys[...] = jnp.zeros_like(ys)
        fetch(t, cur)

    @pl.when(t + 1 < nt)
    def _():
        fetch(t + 1, 1 - cur)

    n_all = pl.multiple_of(tot_ref[t], SEG_ALIGN)
    pltpu.make_async_copy(y_hbm.at[pl.ds(0, n_all)], ys.at[cur, pl.ds(0, n_all)], sem.at[cur]).wait()

    y_lo, y_hi = _unpack_halves(ys[cur])
    pos = pos_ref[...]
    gt = gt_ref[...]
    lane = lax.broadcasted_iota(jnp.int32, (tm, srows), 1)
    sel = (jnp.where(lane == pos[:, 0:1], gt[:, 0:1], 0.0)
           + jnp.where(lane == pos[:, 1:2], gt[:, 1:2], 0.0)).astype(BF16)
    m = jnp.concatenate([jnp.dot(sel, y_lo, preferred_element_type=F32),
                         jnp.dot(sel, y_hi, preferred_element_type=F32)], axis=1)
    x2 = _layer_norm(alpha * x1_ref[...] + m, g_ref[...], b_ref[...])
    x2_ref[...] = x2
    x2b_ref[...] = x2.astype(BF16)
    _store_class_views(x2, x2_scr, view_refs, views)


def _expert_merge(y_buf, pos_t, gate_t, x1, g, b, alpha, tables, views):
    n, d = x1.shape
    tm = TOKEN_TILE
    nt = n // tm
    tok = pl.BlockSpec((tm, d), lambda t, *_: (t, 0))
    pair = pl.BlockSpec((tm, 2), lambda t, *_: (t, 0))
    vec = pl.BlockSpec((1, d), lambda t, *_: (0, 0))
    grid_spec = pltpu.PrefetchScalarGridSpec(
        num_scalar_prefetch=4,
        grid=(nt,),
        in_specs=[pl.BlockSpec(memory_space=pl.ANY), tok, pair, pair, vec, vec],
        out_specs=[tok, tok] + [pl.BlockSpec((tm // r, r * d), lambda t, *_: (t, 0)) for r in views],
        scratch_shapes=[pltpu.VMEM((2, SORTED_ROWS, d // 2), jnp.uint32), pltpu.VMEM((d // 128, tm, 128), F32),
                        pltpu.SemaphoreType.DMA((2,))],
    )
    return pl.pallas_call(
        functools.partial(_merge_kernel, nt=nt, alpha=alpha, views=views),
        grid_spec=grid_spec,
        out_shape=[jax.ShapeDtypeStruct((n, d), F32), jax.ShapeDtypeStruct((n, d), BF16)]
        + [jax.ShapeDtypeStruct((n // r, r * d), BF16) for r in views],
        compiler_params=_params("arbitrary"),
        name="expert_merge",
    )(tables["dst"], tables["off"], tables["len"], tables["tot"], y_buf, x1, pos_t, gate_t, g, b)


def _routing_tables(counts):
    blk = ROW_BLOCK
    nt = counts.shape[0]
    seg = (counts + SEG_ALIGN - 1) // SEG_ALIGN * SEG_ALIGN
    off = jnp.cumsum(seg, axis=1) - seg
    used = jnp.sum(seg, axis=0)
    region = (used + blk - 1) // blk * blk
    region_end = jnp.cumsum(region)
    region_start = region_end - region
    dst = region_start[None, :] + jnp.cumsum(seg, axis=0) - seg
    n_act = region_end[-1] // blk
    i32 = lambda v: v.astype(jnp.int32)
    return dict(dst=i32(dst.reshape(-1)), off=i32(off.reshape(-1)), len=i32(seg.reshape(-1)),
                tot=i32(jnp.sum(seg, axis=1)), fill_row=i32(region_start + used), fill_len=i32(region - used),
                n_act=i32(n_act.reshape(1)), first_block=i32(region_start // blk), n_blocks=i32(region // blk))


def kernel(x, w_in, w_attn_out, w_conv_out, w_o, conv_w, conv_b, conv_ln_g, conv_ln_b, ln_mix_g, ln_mix_b,
           expert_w_gate_up, expert_w_down, ln_ffn_g, ln_ffn_b, router_w, rel_bias):
    batch, seq, d = x.shape
    depth = w_in.shape[0]
    n = batch * seq
    aw, gw = ATTN_WIDTH, GROUP_WIDTH
    assert d % 256 == 0 and n % 1024 == 0 and seq % PROJ_TILE == 0
    for window, r in DILATION_GROUPS:
        assert window // r == BAND and seq % (r * BAND) == 0
    alpha = (2 * depth) ** 0.25
    scale = HEAD_DIM ** -0.5

    biases = [_band_bias(rel_bias, gi, r) for gi, (_, r) in enumerate(DILATION_GROUPS)]
    rwt = router_w.T.astype(BF16)
    nt = n // TOKEN_TILE
    nb = -(-(2 * n + N_EXPERTS * (nt * (SEG_ALIGN - 1) + ROW_BLOCK - 1)) // ROW_BLOCK)
    dilations = tuple(r for _, r in DILATION_GROUPS if r > 1)

    xf = x.reshape(n, d)
    xb, *xv = _input_views(xf, dilations)
    xviews = dict(zip(dilations, xv))
    for l in range(depth):
        wl = w_in[l]

        def qkv_cols(gi):
            return jnp.concatenate([wl[:, gi * gw:(gi + 1) * gw] * scale,
                                    wl[:, aw + gi * gw:aw + (gi + 1) * gw],
                                    wl[:, 2 * aw + gi * gw:2 * aw + (gi + 1) * gw]], axis=1)

        w0 = jnp.concatenate([wl[:, 3 * aw + d:], wl[:, 3 * aw:3 * aw + d], qkv_cols(0)], axis=1).astype(BF16)
        conv = dict(cw=conv_w[l], cb=conv_b[l][None], clg=conv_ln_g[l][None], clb=conv_ln_b[l][None])
        proj0, hc = _project_and_conv(xb, w0, conv, seq, 2 * d)
        outs, lses = [], []
        for gi, (_, r) in enumerate(DILATION_GROUPS):
            if r == 1:
                qkv, col0 = proj0.reshape(batch, seq, proj0.shape[1]), (3 * d) // gw
            else:
                qkv = _project_dilated(xviews[r], qkv_cols(gi).astype(BF16), batch, seq, r)
                qkv, col0 = qkv.reshape(batch * r, seq // r, 3 * gw), 0
            o, lse = _attention(qkv, biases[gi], batch, seq, r, col0)
            outs.append(o)
            lses.append(lse)

        lw = dict(wao=w_attn_out[l].astype(BF16), wco=w_conv_out[l].astype(BF16), wo=w_o[l].astype(BF16),
                  lng=ln_mix_g[l][None], lnb=ln_mix_b[l][None], rwt=rwt, alpha=alpha)
        x1, x1b, gate, pos, cnt = _mixer_tail(outs, lses, proj0, hc, xf, lw)

        tables = _routing_tables(cnt[:, :, 0].astype(jnp.int32))
        buf = _dispatch(x1b, pos, tables, nb)
        y_buf = _expert_ffn(buf, expert_w_gate_up, expert_w_down, l, tables)
        views = dilations if l + 1 < depth else ()
        xf, xb, *xv = _expert_merge(y_buf, pos.T, gate.T, x1, ln_ffn_g[l][None], ln_ffn_b[l][None], alpha,
                                    tables, views)
        xviews = dict(zip(views, xv))
    return xf.reshape(batch, seq, d)
```

```python
import functools
import math

import jax
import jax.numpy as jnp
from jax import lax
from jax.experimental import pallas as pl
from jax.experimental.pallas import tpu as pltpu

F32 = jnp.float32
BF16 = jnp.bfloat16

HEAD_DIM = 64
HEADS_PER_GROUP = 4
GROUP_WIDTH = HEADS_PER_GROUP * HEAD_DIM
DILATION_GROUPS = ((128, 1), (512, 4), (2048, 16))
N_GROUPS = len(DILATION_GROUPS)
ATTN_WIDTH = N_GROUPS * GROUP_WIDTH
BAND = 128
CONV_KERNEL = 31
CONV_HALO = 32
CONV_TAIL = 16
CONV_ROWS = 64
PROJ_TILE = 512
PROJ_PIECES = 5
N_BUCKETS = 32
MAX_DISTANCE = 2048
N_EXPERTS = 16
EXPERTS_PER_GROUP = 4
LN_EPS = 1e-5
MASKED = -1e30

WEIGHT_DMA_PRIORITY = 1
PROJ_ROWS = 2048
ROW_BLOCK = 256
TOKEN_TILE = 256
MIXER_TILE = 1024
SEG_ALIGN = 8
SORTED_ROWS = -(-(2 * TOKEN_TILE + N_EXPERTS * (SEG_ALIGN - 1)) // 128) * 128
VMEM_LIMIT = 56 * 1024 * 1024


def _sigmoid(v):
    return 1.0 / (1.0 + jnp.exp(-v))


def _layer_norm(z, g, b):
    mu = jnp.mean(z, axis=-1, keepdims=True)
    zc = z - mu
    var = jnp.mean(zc * zc, axis=-1, keepdims=True)
    return zc * lax.rsqrt(var + LN_EPS) * g + b


def _params(*sem):
    return pltpu.CompilerParams(dimension_semantics=sem, vmem_limit_bytes=VMEM_LIMIT)


def _proj_conv_kernel(x_ref, w_ref, cw_ref, cb_ref, clg_ref, clb_ref, o_ref, hc_ref,
                      u_scr, hs_scr, ys_scr, conv_scr, *, tm, nt, seq, u_col):
    s = pl.program_id(0)
    c = hc_ref.shape[1]
    prev, cur = (s + 1) % 2, s % 2

    @pl.when(s == 0)
    def _():
        u_scr[1] = jnp.zeros(u_scr.shape[1:], BF16)
        hs_scr[...] = jnp.zeros(hs_scr.shape, F32)

    u = u_scr[prev].astype(F32)
    seq_start = (jnp.maximum(s - 1, 0) * tm) % seq == 0
    hs_scr[0:CONV_HALO, :] = jnp.where(seq_start, 0.0, hs_scr[tm:tm + CONV_HALO, :])
    hs_scr[CONV_HALO:CONV_HALO + tm, :] = u[:, :c] * _sigmoid(u[:, c:])

    off = CONV_HALO - (CONV_KERNEL - 1)
    rows_y = CONV_ROWS + 8
    def conv_chunk(cc, rc, zero):
        ls = slice(cc * 128, (cc + 1) * 128)
        r0 = rc * CONV_ROWS
        acc = None
        for k in range(8):
            y = None
            for a in range((off + CONV_KERNEL + 7) // 8):
                tap = 8 * a + k - off
                if 0 <= tap < CONV_KERNEL:
                    term = hs_scr[r0 + 8 * a:r0 + 8 * a + rows_y, ls] * cw_ref[tap:tap + 1, ls]
                    y = term if y is None else y + term
            if k == 0:
                acc = y[0:CONV_ROWS] + zero
            else:
                ys_scr[k, rc * rows_y:(rc + 1) * rows_y, :] = y
                acc = acc + ys_scr[k, rc * rows_y + k:rc * rows_y + k + CONV_ROWS, :]
        conv_scr[r0:r0 + CONV_ROWS, ls] = acc

    cols = o_ref.shape[1]
    pieces = max(p for p in range(1, PROJ_PIECES + 1) if (cols // 128) % p == 0)
    pw = cols // pieces
    chunks = [(cc, rc) for cc in range(c // 128) for rc in range(tm // CONV_ROWS)]
    per_piece = -(-len(chunks) // pieces)
    for p in range(pieces):
        piece = jnp.dot(x_ref[...], w_ref[:, p * pw:(p + 1) * pw], preferred_element_type=F32)
        o_ref[:, p * pw:(p + 1) * pw] = piece.astype(o_ref.dtype)
        bits = lax.bitcast_convert_type(piece[0:8, 0:128], jnp.uint32)
        zero8 = lax.bitcast_convert_type((bits >> 16) >> 16, F32)
        zero = jnp.concatenate([zero8] * (CONV_ROWS // 8), axis=0)
        for cc, rc in chunks[p * per_piece:(p + 1) * per_piece]:
            conv_chunk(cc, rc, zero)
    hc = _layer_norm(conv_scr[...] + cb_ref[...], clg_ref[...], clb_ref[...])
    hc_ref[...] = (hc * _sigmoid(hc)).astype(BF16)
    u_scr[cur] = o_ref[:, u_col:u_col + 2 * c]


def _project_and_conv(xb, w, conv, seq, u_col):
    n, d = xb.shape
    cols = w.shape[1]
    c = d // 2
    tm = PROJ_TILE
    nt = n // tm

    def early(s):
        return jnp.minimum(s, nt - 1)

    def late(s):
        return jnp.maximum(s - 1, 0)

    def const(shape):
        return pl.BlockSpec(shape, lambda s: (0,) * len(shape))

    return pl.pallas_call(
        functools.partial(_proj_conv_kernel, tm=tm, nt=nt, seq=seq, u_col=u_col),
        grid=(nt + 1,),
        in_specs=[pl.BlockSpec((tm, d), lambda s: (early(s), 0)), const((d, cols))]
        + [const(conv[k].shape) for k in ("cw", "cb", "clg", "clb")],
        out_specs=[pl.BlockSpec((tm, cols), lambda s: (early(s), 0)),
                   pl.BlockSpec((tm, c), lambda s: (late(s), 0))],
        out_shape=[jax.ShapeDtypeStruct((n, cols), BF16), jax.ShapeDtypeStruct((n, c), BF16)],
        scratch_shapes=[pltpu.VMEM((2, tm, 2 * c), BF16),
                        pltpu.VMEM((CONV_HALO + tm + CONV_TAIL, c), F32),
                        pltpu.VMEM((8, (tm // CONV_ROWS) * (CONV_ROWS + 8), 128), F32),
                        pltpu.VMEM((tm, c), F32)],
        compiler_params=_params("arbitrary"),
        name="proj_natural",
    )(xb, w, *[conv[k] for k in ("cw", "cb", "clg", "clb")])


def _project_dilated(xv, w, batch, seq, r):
    d, c = w.shape
    n = xv.shape[0] * r
    sub = seq // r
    tm = min(PROJ_ROWS, sub)
    nl = sub // tm
    cps = max(1, min(r, PROJ_ROWS // sub))
    assert r % cps == 0

    def mm_classes(x_ref, w_ref, o_ref):
        for k in range(cps):
            o_ref[k * tm:(k + 1) * tm, :] = jnp.dot(x_ref[:, k * d:(k + 1) * d], w_ref[...],
                                                    preferred_element_type=F32).astype(o_ref.dtype)

    return pl.pallas_call(
        mm_classes,
        grid=(batch, r // cps, nl),
        in_specs=[pl.BlockSpec((tm, cps * d), lambda b, cg, j: (b * nl + j, cg)),
                  pl.BlockSpec((d, c), lambda b, cg, j: (0, 0))],
        out_specs=pl.BlockSpec((cps * tm, c), lambda b, cg, j: ((b * (r // cps) + cg) * nl + j, 0)),
        out_shape=jax.ShapeDtypeStruct((n, c), BF16),
        compiler_params=_params("arbitrary", "arbitrary", "arbitrary"),
        name=f"proj_dilated_{r}",
    )(xv, w)


def _attn_kernel(q_ref, kp_ref, vp_ref, kc_ref, vc_ref, bias_ref, o_ref, lse_ref, k_scr, v_scr, *, nqb):
    i = pl.program_id(1)
    k_scr[0:BAND, :] = kp_ref[0]
    k_scr[BAND:, :] = kc_ref[0]
    v_scr[0:BAND, :] = vp_ref[0]
    v_scr[BAND:, :] = vc_ref[0]
    col = lax.broadcasted_iota(jnp.int32, (2 * BAND, 2 * BAND), 1)
    first_head = lax.broadcasted_iota(jnp.int32, (BAND, 2 * HEAD_DIM), 1) < HEAD_DIM

    def body(j, carry):
        r0 = pl.multiple_of(j * BAND, BAND)
        q = q_ref[0, pl.ds(r0, BAND), :]
        kk = k_scr[pl.ds(r0, 2 * BAND), :]
        vv = v_scr[pl.ds(r0, 2 * BAND), :]
        no_prev = jnp.logical_and(jnp.logical_and(i == 0, j == 0), col < BAND)
        outs, lses = [], []
        for pair in range(HEADS_PER_GROUP // 2):
            sl = slice(pair * 2 * HEAD_DIM, (pair + 1) * 2 * HEAD_DIM)
            q2, k2, v2 = q[:, sl], kk[:, sl], vv[:, sl]
            zq = jnp.zeros_like(q2)
            qs = jnp.concatenate([jnp.where(first_head, q2, zq), jnp.where(first_head, zq, q2)], axis=0)
            s = lax.dot_general(qs, k2, (((1,), (1,)), ((), ())), preferred_element_type=F32)
            s = s + bias_ref[pair]
            s = jnp.where(no_prev, MASKED, s)
            m = jnp.max(s, axis=-1, keepdims=True)
            p = jnp.exp(s - m)
            den = jnp.sum(p, axis=-1, keepdims=True)
            o = jnp.dot(p.astype(BF16), v2, preferred_element_type=F32) * (1.0 / den)
            lse = jnp.broadcast_to(m + jnp.log(den), (2 * BAND, 2 * HEAD_DIM))
            outs.append(jnp.where(first_head, o[:BAND], o[BAND:]))
            lses.append(jnp.where(first_head, lse[:BAND], lse[BAND:]))
        o_ref[0, pl.ds(r0, BAND), :] = jnp.concatenate(outs, axis=1).astype(o_ref.dtype)
        lse_ref[0, pl.ds(r0, BAND), :] = jnp.concatenate(lses, axis=1)
        return carry

    lax.fori_loop(0, nqb, body, 0, unroll=math.gcd(nqb, 8))


def _attention(qkv, bias, batch, seq, r, col0):
    sub = seq // r
    nqb = min(8, sub // BAND)
    rows = nqb * BAND
    nt = sub // rows
    gw = GROUP_WIDTH

    def cur(c):
        return pl.BlockSpec((1, rows, gw), lambda bc, i: (bc, i, c))

    def prev(c):
        return pl.BlockSpec((1, BAND, gw), lambda bc, i: (bc, jnp.maximum(i * nqb - 1, 0), c))

    out_spec = pl.BlockSpec((1, rows, gw), lambda bc, i: (bc // r, i, bc % r))
    o, lse = pl.pallas_call(
        functools.partial(_attn_kernel, nqb=nqb),
        grid=(batch * r, nt),
        in_specs=[cur(col0), prev(col0 + 1), prev(col0 + 2), cur(col0 + 1), cur(col0 + 2),
                  pl.BlockSpec((HEADS_PER_GROUP // 2, 2 * BAND, 2 * BAND), lambda bc, i: (0, 0, 0))],
        out_specs=[out_spec, out_spec],
        out_shape=[jax.ShapeDtypeStruct((batch, sub, r * gw), BF16),
                   jax.ShapeDtypeStruct((batch, sub, r * gw), F32)],
        scratch_shapes=[pltpu.VMEM((rows + BAND, gw), BF16), pltpu.VMEM((rows + BAND, gw), BF16)],
        compiler_params=_params("arbitrary", "arbitrary"),
        name=f"attn_dilation_{r}",
    )(qkv, qkv, qkv, qkv, qkv, bias.reshape(HEADS_PER_GROUP // 2, 2 * BAND, 2 * BAND))
    return o.reshape(batch * sub, r * gw), lse.reshape(batch * sub, r * gw)


def _t5_bucket(dist):
    max_exact = N_BUCKETS // 2
    n = jnp.maximum(dist, 0)
    nf = jnp.maximum(n, 1).astype(F32)
    large = max_exact + (jnp.log(nf / max_exact) / math.log(MAX_DISTANCE / max_exact)
                         * (N_BUCKETS - max_exact)).astype(jnp.int32)
    large = jnp.minimum(large, N_BUCKETS - 1)
    return jnp.where(n < max_exact, n, large)


def _band_bias(rel_bias, gi, r):
    qi = jnp.arange(BAND)[:, None]
    kj = jnp.arange(2 * BAND)[None, :]
    dist = qi + BAND - kj
    bucket = _t5_bucket(dist * r)
    valid = (dist >= 0) & (dist <= BAND)
    b = jnp.full((HEADS_PER_GROUP, BAND, 2 * BAND), MASKED, F32)
    for k in range(N_BUCKETS):
        row = rel_bias[k, gi * HEADS_PER_GROUP:(gi + 1) * HEADS_PER_GROUP].astype(F32)
        b = jnp.where(((bucket == k) & valid)[None], row[:, None, None], b)
    return b


def _post_kernel(o0_ref, o1_ref, o2_ref, l0_ref, l1_ref, l2_ref, hc_ref, g_ref, x_ref,
                 wao_ref, wco_ref, wo_ref, lng_ref, lnb_ref, rwt_ref,
                 x1_ref, x1b_ref, gate_ref, pos_ref, cnt_ref, nat_scr, *, tm, alpha):
    d = x_ref.shape[1]
    gw = GROUP_WIDTH

    def natural(ref, slot, r):
        if r == 1:
            return ref[...].astype(F32)
        nl = gw // 128
        for cls in range(r):
            for j in range(nl):
                lanes = slice(cls * gw + j * 128, cls * gw + (j + 1) * 128)
                nat_scr[slot * nl + j, pl.ds(cls, tm // r, stride=r), :] = ref[:, lanes].astype(F32)
        return jnp.concatenate([nat_scr[slot * nl + j] for j in range(nl)], axis=1)

    rs = [r for _, r in DILATION_GROUPS]
    o0, o1, o2 = (natural(ref, s, r) for s, (ref, r) in enumerate(zip((o0_ref, o1_ref, o2_ref), rs)))
    l0, l1, l2 = (natural(ref, 3 + s, r) for s, (ref, r) in enumerate(zip((l0_ref, l1_ref, l2_ref), rs)))
    mx = jnp.maximum(jnp.maximum(l0, l1), l2)
    e0, e1, e2 = jnp.exp(l0 - mx), jnp.exp(l1 - mx), jnp.exp(l2 - mx)
    attn = (e0 * o0 + e1 * o1 + e2 * o2) * (1.0 / (e0 + e1 + e2))
    attn_branch = jnp.dot(attn.astype(BF16), wao_ref[...], preferred_element_type=F32)

    conv_branch = jnp.dot(hc_ref[...], wco_ref[...], preferred_element_type=F32)

    gates = _sigmoid(g_ref[...].astype(F32))
    merged = gates[:, :d] * attn_branch + gates[:, d:] * conv_branch
    hmix = jnp.dot(merged.astype(BF16), wo_ref[...], preferred_element_type=F32)
    x1 = _layer_norm(alpha * x_ref[...] + hmix, lng_ref[...], lnb_ref[...])
    x1_ref[...] = x1
    x1b = x1.astype(BF16)
    x1b_ref[...] = x1b

    logits = lax.dot_general(rwt_ref[...], x1b, (((1,), (1,)), ((), ())), preferred_element_type=F32)
    ex = jnp.exp(logits - jnp.max(logits, axis=0, keepdims=True))
    rows = [ex[e:e + 1, :] for e in range(N_EXPERTS)]
    best, gsel = None, None
    for g in range(N_EXPERTS // EXPERTS_PER_GROUP):
        v = rows[g * EXPERTS_PER_GROUP:(g + 1) * EXPERTS_PER_GROUP]
        score = None
        for a in range(EXPERTS_PER_GROUP):
            for b in range(a + 1, EXPERTS_PER_GROUP):
                ps = v[a] + v[b]
                score = ps if score is None else jnp.maximum(score, ps)
        if best is None:
            best, gsel = score, jnp.zeros_like(score, dtype=jnp.int32)
        else:
            upd = score > best
            gsel = jnp.where(upd, g, gsel)
            best = jnp.where(upd, score, best)
    vals = []
    for j in range(EXPERTS_PER_GROUP):
        vj = rows[j]
        for g in range(1, N_EXPERTS // EXPERTS_PER_GROUP):
            vj = jnp.where(gsel == g, rows[g * EXPERTS_PER_GROUP + j], vj)
        vals.append(vj)
    v1, i1 = vals[0], jnp.zeros_like(gsel)
    for j in range(1, EXPERTS_PER_GROUP):
        upd = vals[j] > v1
        i1 = jnp.where(upd, j, i1)
        v1 = jnp.where(upd, vals[j], v1)
    v2, i2 = jnp.full_like(v1, -1.0), jnp.zeros_like(gsel)
    for j in range(EXPERTS_PER_GROUP):
        upd = jnp.logical_and(i1 != j, vals[j] > v2)
        i2 = jnp.where(upd, j, i2)
        v2 = jnp.where(upd, vals[j], v2)
    ea = gsel * EXPERTS_PER_GROUP + i1
    eb = gsel * EXPERTS_PER_GROUP + i2
    inv = 1.0 / (v1 + v2)
    gate_ref[...] = jnp.concatenate([v1 * inv, v2 * inv], axis=0)

    st = TOKEN_TILE
    eid = lax.broadcasted_iota(jnp.int32, (N_EXPERTS, st), 0)
    before = (lax.broadcasted_iota(jnp.int32, (st, st), 0)
              < lax.broadcasted_iota(jnp.int32, (st, st), 1))
    upper = jnp.where(before, 1.0, 0.0).astype(BF16)
    lower = (lax.broadcasted_iota(jnp.int32, (N_EXPERTS, N_EXPERTS), 1)
             < lax.broadcasted_iota(jnp.int32, (N_EXPERTS, N_EXPERTS), 0))
    lower = jnp.where(lower, 1.0, 0.0).astype(BF16)
    for sub in range(tm // st):
        lanes = slice(sub * st, (sub + 1) * st)
        hit_a = eid == ea[:, lanes]
        hit_b = eid == eb[:, lanes]
        onehot = jnp.where(jnp.logical_or(hit_a, hit_b), 1.0, 0.0)
        rank = jnp.dot(onehot.astype(BF16), upper, preferred_element_type=F32)
        count = jnp.sum(onehot, axis=1, keepdims=True)
        seg_len = jnp.floor((count + (SEG_ALIGN - 1)) * (1.0 / SEG_ALIGN)) * SEG_ALIGN
        seg_off = jnp.dot(lower, jnp.broadcast_to(seg_len, (N_EXPERTS, 128)).astype(BF16),
                          preferred_element_type=F32)[:, 0:1]
        row = rank + seg_off
        pa = jnp.sum(jnp.where(hit_a, row, 0.0), axis=0, keepdims=True)
        pb = jnp.sum(jnp.where(hit_b, row, 0.0), axis=0, keepdims=True)
        pos_ref[:, lanes] = jnp.concatenate([pa, pb], axis=0).astype(jnp.int32)
        cnt_ref[sub] = jnp.broadcast_to(count, (N_EXPERTS, 128))


def _mixer_tail(o, lse, proj0, hc, x, lw):
    n, d = x.shape
    tm = MIXER_TILE
    nt = n // tm
    sub = tm // TOKEN_TILE
    gw = GROUP_WIDTH

    def tok(width):
        return pl.BlockSpec((tm, width), lambda i: (i, 0))

    def const(shape):
        return pl.BlockSpec(shape, lambda i: (0,) * len(shape))

    def grouped(r):
        return pl.BlockSpec((tm // r, r * gw), lambda i: (i, 0))

    rs = [r for _, r in DILATION_GROUPS]
    in_specs = (
        [grouped(r) for r in rs] * 2
        + [tok(d // 2),
           pl.BlockSpec((tm, 2 * d), lambda i: (i, 0)),
           tok(d)]
        + [const(lw[k].shape) for k in ("wao", "wco", "wo", "lng", "lnb", "rwt")]
    )
    lane_row = pl.BlockSpec((2, tm), lambda i: (0, i))
    out_specs = [tok(d), tok(d), lane_row, lane_row,
                 pl.BlockSpec((sub, N_EXPERTS, 128), lambda i: (i, 0, 0))]
    out_shape = [jax.ShapeDtypeStruct((n, d), F32), jax.ShapeDtypeStruct((n, d), BF16),
                 jax.ShapeDtypeStruct((2, n), F32), jax.ShapeDtypeStruct((2, n), jnp.int32),
                 jax.ShapeDtypeStruct((n // TOKEN_TILE, N_EXPERTS, 128), F32)]
    alpha = lw["alpha"]
    return pl.pallas_call(
        functools.partial(_post_kernel, tm=tm, alpha=alpha),
        grid=(nt,),
        in_specs=in_specs,
        out_specs=out_specs,
        out_shape=out_shape,
        scratch_shapes=[pltpu.VMEM((2 * N_GROUPS * (gw // 128), tm, 128), F32)],
        compiler_params=_params("arbitrary"),
        name="mixer_tail",
    )(o[0], o[1], o[2], lse[0], lse[1], lse[2], hc, proj0, x,
      *[lw[k] for k in ("wao", "wco", "wo", "lng", "lnb", "rwt")])


def _pack_halves(v):
    h = v.shape[1] // 2
    bits = lax.bitcast_convert_type(v, jnp.uint32)
    return (bits[:, :h] >> 16) | (bits[:, h:] & jnp.uint32(0xFFFF0000))


def _unpack_halves(w):
    lo = lax.bitcast_convert_type(w << 16, F32).astype(BF16)
    hi = lax.bitcast_convert_type(w & jnp.uint32(0xFFFF0000), F32).astype(BF16)
    return lo, hi


def _dispatch_kernel(dst_ref, off_ref, len_ref, tot_ref, fill_row_ref, fill_len_ref, nact_ref,
                     x_ref, pos_ref, buf_hbm, xs, zeros, sem, zsem, *, nt, nb, blk):
    t = pl.program_id(0)
    cur = t % 2
    srows = xs.shape[1]

    def tile_wait(tile, buf):
        n = pl.multiple_of(tot_ref[tile], SEG_ALIGN)
        pltpu.make_async_copy(xs.at[buf, pl.ds(0, n)], buf_hbm.at[pl.ds(0, n)], sem.at[buf]).wait()

    @pl.when(t >= 2)
    def _():
        tile_wait(t - 2, cur)

    pos = pos_ref[...]
    row = lax.broadcasted_iota(jnp.int32, (srows, pos.shape[1]), 0)
    hit = jnp.logical_or(row == pos[0:1, :], row == pos[1:2, :])
    sel = jnp.where(hit, 1.0, 0.0).astype(BF16)
    xs[cur] = _pack_halves(jnp.dot(sel, x_ref[...], preferred_element_type=F32))

    for e in range(N_EXPERTS):
        k = t * N_EXPERTS + e
        n = pl.multiple_of(len_ref[k], SEG_ALIGN)
        src = pl.multiple_of(off_ref[k], SEG_ALIGN)
        dst = pl.multiple_of(dst_ref[k], SEG_ALIGN)

        @pl.when(n > 0)
        def _():
            pltpu.make_async_copy(xs.at[cur, pl.ds(src, n)], buf_hbm.at[pl.ds(dst, n)], sem.at[cur]).start()

    @pl.when(t == nt - 1)
    def _():
        if nt >= 2:
            tile_wait(t - 1, 1 - cur)
        tile_wait(t, cur)
        zeros[...] = jnp.zeros_like(zeros)

        def region_fill(e):
            n = pl.multiple_of(fill_len_ref[e], SEG_ALIGN)
            dst = pl.multiple_of(fill_row_ref[e], SEG_ALIGN)
            return n, pltpu.make_async_copy(zeros.at[pl.ds(0, n)], buf_hbm.at[pl.ds(dst, n)], zsem)

        def block_fill(b):
            return pltpu.make_async_copy(zeros, buf_hbm.at[pl.ds(pl.multiple_of(b * blk, blk), blk)], zsem)

        def each_block(fn):
            def body(b, carry):
                fn(block_fill(b))
                return carry
            lax.fori_loop(nact_ref[0], nb, body, 0)

        for e in range(N_EXPERTS):
            n, cp = region_fill(e)
            pl.when(n > 0)(cp.start)
        each_block(lambda cp: cp.start())
        for e in range(N_EXPERTS):
            n, cp = region_fill(e)
            pl.when(n > 0)(cp.wait)
        each_block(lambda cp: cp.wait())


def _dispatch(x1b, pos, tables, nb):
    n, d = x1b.shape
    tm = TOKEN_TILE
    nt = n // tm
    blk = ROW_BLOCK
    grid_spec = pltpu.PrefetchScalarGridSpec(
        num_scalar_prefetch=7,
        grid=(nt,),
        in_specs=[pl.BlockSpec((tm, d), lambda t, *_: (t, 0)),
                  pl.BlockSpec((2, tm), lambda t, *_: (0, t))],
        out_specs=pl.BlockSpec(memory_space=pl.ANY),
        scratch_shapes=[pltpu.VMEM((2, SORTED_ROWS, d // 2), jnp.uint32), pltpu.VMEM((blk, d // 2), jnp.uint32),
                        pltpu.SemaphoreType.DMA((2,)), pltpu.SemaphoreType.DMA],
    )
    return pl.pallas_call(
        functools.partial(_dispatch_kernel, nt=nt, nb=nb, blk=blk),
        grid_spec=grid_spec,
        out_shape=jax.ShapeDtypeStruct((nb * blk, d // 2), jnp.uint32),
        compiler_params=_params("arbitrary"),
        name="expert_dispatch",
    )(tables["dst"], tables["off"], tables["len"], tables["tot"], tables["fill_row"], tables["fill_len"],
      tables["n_act"], x1b, pos)


def _ffn_kernel(first_ref, count_ref, nact_ref, xs_hbm, wgu_hbm, wdn_hbm, y_hbm,
                wgu_f, wdn_f, wgu_b, wdn_b, xbuf, obuf, h_scr, wsem, isem, osem, zsem, *, nb, blk, layer):
    e = pl.program_id(0)
    n_exp = pl.num_programs(0)
    f = wdn_f.shape[1]
    half = xbuf.shape[2]
    b0 = first_ref[e]
    n_blocks = count_ref[e]

    def weights(ex, slot):
        return (pltpu.make_async_copy(wgu_hbm.at[layer, ex], wgu_f.at[slot], wsem.at[slot]),
                pltpu.make_async_copy(wdn_hbm.at[layer, ex], wdn_f.at[slot], wsem.at[slot]))

    def fetch_weights(ex):
        for cp in weights(ex, ex % 2):
            cp.start(priority=WEIGHT_DMA_PRIORITY)

    @pl.when(e == 0)
    def _():
        fetch_weights(e)

    for cp in weights(e, e % 2):
        cp.wait()

    @pl.when(e + 1 < n_exp)
    def _():
        fetch_weights(e + 1)

    def rows(j):
        return pl.ds(pl.multiple_of((b0 + j) * blk, blk), blk)

    def load(j, buf):
        return pltpu.make_async_copy(xs_hbm.at[rows(j)], xbuf.at[buf], isem.at[buf])

    def store(j, buf):
        return pltpu.make_async_copy(obuf.at[buf], y_hbm.at[rows(j)], osem.at[buf])

    def prefetch(j):
        @pl.when(j < n_blocks)
        def _():
            load(j, j % 2).start()

    def up(j):
        buf = j % 2
        x_lo, x_hi = _unpack_halves(xbuf[buf])
        h_scr[buf] = (jnp.dot(x_lo, wgu_b[0:half, :], preferred_element_type=F32)
                      + jnp.dot(x_hi, wgu_b[half:, :], preferred_element_type=F32))

    def release(j):
        @pl.when(j >= 2)
        def _():
            store(j - 2, j % 2).wait()

    def down(j):
        buf = j % 2
        a = h_scr[buf, :, 0:f]
        act = (a * _sigmoid(a) * h_scr[buf, :, f:]).astype(BF16)
        y = jnp.dot(act, wdn_b[...], preferred_element_type=F32)
        obuf[buf] = _pack_halves(y.astype(BF16).astype(F32))
        store(j, buf).start()

    @pl.when(n_blocks > 0)
    def _():
        load(0, 0).start()
        wgu_b[...] = wgu_f[e % 2].astype(BF16)
        wdn_b[...] = wdn_f[e % 2].astype(BF16)
        prefetch(1)
        load(0, 0).wait()
        up(0)

    def block(j, carry):
        prefetch(j + 1)
        release(j - 1)
        load(j, j % 2).wait()
        down(j - 1)
        up(j)
        return carry

    lax.fori_loop(1, n_blocks, block, 0)

    @pl.when(n_blocks > 0)
    def _():
        release(n_blocks - 1)
        down(n_blocks - 1)

    @pl.when(n_blocks >= 2)
    def _():
        store(n_blocks - 2, n_blocks % 2).wait()

    @pl.when(n_blocks >= 1)
    def _():
        store(n_blocks - 1, (n_blocks - 1) % 2).wait()

    @pl.when(e == pl.num_programs(0) - 1)
    def _():
        xbuf[0] = jnp.zeros((blk, half), jnp.uint32)

        def fill(b):
            return pltpu.make_async_copy(xbuf.at[0], y_hbm.at[pl.ds(pl.multiple_of(b * blk, blk), blk)], zsem)

        def each(fn):
            def body(b, carry):
                fn(fill(b))
                return carry
            lax.fori_loop(nact_ref[0], nb, body, 0)

        each(lambda cp: cp.start())
        each(lambda cp: cp.wait())


def _expert_ffn(buf, wgu, wdn, layer, tables):
    n_rows, half = buf.shape
    d = 2 * half
    blk = ROW_BLOCK
    nb = n_rows // blk
    f2 = wgu.shape[3]
    f = wdn.shape[2]
    grid_spec = pltpu.PrefetchScalarGridSpec(
        num_scalar_prefetch=3,
        grid=(N_EXPERTS,),
        in_specs=[pl.BlockSpec(memory_space=pl.ANY), pl.BlockSpec(memory_space=pl.ANY),
                  pl.BlockSpec(memory_space=pl.ANY)],
        out_specs=pl.BlockSpec(memory_space=pl.ANY),
        scratch_shapes=[pltpu.VMEM((2, d, f2), F32), pltpu.VMEM((2, f, d), F32),
                        pltpu.VMEM((d, f2), BF16), pltpu.VMEM((f, d), BF16),
                        pltpu.VMEM((2, blk, half), jnp.uint32), pltpu.VMEM((2, blk, half), jnp.uint32),
                        pltpu.VMEM((2, blk, f2), F32),
                        pltpu.SemaphoreType.DMA((2,)), pltpu.SemaphoreType.DMA((2,)),
                        pltpu.SemaphoreType.DMA((2,)), pltpu.SemaphoreType.DMA],
    )
    return pl.pallas_call(
        functools.partial(_ffn_kernel, nb=nb, blk=blk, layer=layer),
        grid_spec=grid_spec,
        out_shape=jax.ShapeDtypeStruct((n_rows, half), jnp.uint32),
        compiler_params=_params("arbitrary"),
        name="expert_ffn",
    )(tables["first_block"], tables["n_blocks"], tables["n_act"], buf, wgu, wdn)


def _store_class_views(xt, scr, view_refs, views):
    tm, d = xt.shape
    if views:
        for j in range(d // 128):
            scr[j] = xt[:, j * 128:(j + 1) * 128]
    for ref, r in zip(view_refs, views):
        for cls in range(r):
            for j in range(d // 128):
                lanes = slice(cls * d + j * 128, cls * d + (j + 1) * 128)
                ref[:, lanes] = scr[j, pl.ds(cls, tm // r, stride=r), :].astype(BF16)


def _views_kernel(x_ref, xb_ref, *rest, views):
    view_refs, scr = rest[:len(views)], rest[len(views)]
    xt = x_ref[...]
    xb_ref[...] = xt.astype(BF16)
    _store_class_views(xt, scr, view_refs, views)


def _input_views(xf, views):
    n, d = xf.shape
    tm = TOKEN_TILE
    tok = pl.BlockSpec((tm, d), lambda t: (t, 0))
    return pl.pallas_call(
        functools.partial(_views_kernel, views=views),
        grid=(n // tm,),
        in_specs=[tok],
        out_specs=[tok] + [pl.BlockSpec((tm // r, r * d), lambda t: (t, 0)) for r in views],
        out_shape=[jax.ShapeDtypeStruct((n, d), BF16)]
        + [jax.ShapeDtypeStruct((n // r, r * d), BF16) for r in views],
        scratch_shapes=[pltpu.VMEM((d // 128, tm, 128), F32)],
        compiler_params=_params("arbitrary"),
        name="input_views",
    )(xf)


def _merge_kernel(dst_ref, off_ref, len_ref, tot_ref, y_hbm, x1_ref, pos_ref, gt_ref, g_ref, b_ref,
                  x2_ref, x2b_ref, *rest, nt, alpha, views):
    view_refs, (ys, x2_scr, sem) = rest[:len(views)], rest[len(views):]
    t = pl.program_id(0)
    cur = t % 2
    tm, d = x1_ref.shape
    srows = ys.shape[1]

    def fetch(tile, buf):
        for e in range(N_EXPERTS):
            k = tile * N_EXPERTS + e
            n = pl.multiple_of(len_ref[k], SEG_ALIGN)
            dst = pl.multiple_of(off_ref[k], SEG_ALIGN)
            src = pl.multiple_of(dst_ref[k], SEG_ALIGN)

            @pl.when(n > 0)
            def _():
                pltpu.make_async_copy(y_hbm.at[pl.ds(src, n)], ys.at[buf, pl.ds(dst, n)], sem.at[buf]).start()

    @pl.when(t == 0)
    def _():
        ys[...] = jnp.zeros_like(ys)
        fetch(t, cur)

    @pl.when(t + 1 < nt)
    def _():
        fetch(t + 1, 1 - cur)

    n_all = pl.multiple_of(tot_ref[t], SEG_ALIGN)
    pltpu.make_async_copy(y_hbm.at[pl.ds(0, n_all)], ys.at[cur, pl.ds(0, n_all)], sem.at[cur]).wait()

    y_lo, y_hi = _unpack_halves(ys[cur])
    pos = pos_ref[...]
    gt = gt_ref[...]
    lane = lax.broadcasted_iota(jnp.int32, (tm, srows), 1)
    sel = (jnp.where(lane == pos[:, 0:1], gt[:, 0:1], 0.0)
           + jnp.where(lane == pos[:, 1:2], gt[:, 1:2], 0.0)).astype(BF16)
    m = jnp.concatenate([jnp.dot(sel, y_lo, preferred_element_type=F32),
                         jnp.dot(sel, y_hi, preferred_element_type=F32)], axis=1)
    x2 = _layer_norm(alpha * x1_ref[...] + m, g_ref[...], b_ref[...])
    x2_ref[...] = x2
    x2b_ref[...] = x2.astype(BF16)
    _store_class_views(x2, x2_scr, view_refs, views)


def _expert_merge(y_buf, pos_t, gate_t, x1, g, b, alpha, tables, views):
    n, d = x1.shape
    tm = TOKEN_TILE
    nt = n // tm
    tok = pl.BlockSpec((tm, d), lambda t, *_: (t, 0))
    pair = pl.BlockSpec((tm, 2), lambda t, *_: (t, 0))
    vec = pl.BlockSpec((1, d), lambda t, *_: (0, 0))
    grid_spec = pltpu.PrefetchScalarGridSpec(
        num_scalar_prefetch=4,
        grid=(nt,),
        in_specs=[pl.BlockSpec(memory_space=pl.ANY), tok, pair, pair, vec, vec],
        out_specs=[tok, tok] + [pl.BlockSpec((tm // r, r * d), lambda t, *_: (t, 0)) for r in views],
        scratch_shapes=[pltpu.VMEM((2, SORTED_ROWS, d // 2), jnp.uint32), pltpu.VMEM((d // 128, tm, 128), F32),
                        pltpu.SemaphoreType.DMA((2,))],
    )
    return pl.pallas_call(
        functools.partial(_merge_kernel, nt=nt, alpha=alpha, views=views),
        grid_spec=grid_spec,
        out_shape=[jax.ShapeDtypeStruct((n, d), F32), jax.ShapeDtypeStruct((n, d), BF16)]
        + [jax.ShapeDtypeStruct((n // r, r * d), BF16) for r in views],
        compiler_params=_params("arbitrary"),
        name="expert_merge",
    )(tables["dst"], tables["off"], tables["len"], tables["tot"], y_buf, x1, pos_t, gate_t, g, b)


def _routing_tables(counts):
    blk = ROW_BLOCK
    nt = counts.shape[0]
    seg = (counts + SEG_ALIGN - 1) // SEG_ALIGN * SEG_ALIGN
    off = jnp.cumsum(seg, axis=1) - seg
    used = jnp.sum(seg, axis=0)
    region = (used + blk - 1) // blk * blk
    region_end = jnp.cumsum(region)
    region_start = region_end - region
    dst = region_start[None, :] + jnp.cumsum(seg, axis=0) - seg
    n_act = region_end[-1] // blk
    i32 = lambda v: v.astype(jnp.int32)
    return dict(dst=i32(dst.reshape(-1)), off=i32(off.reshape(-1)), len=i32(seg.reshape(-1)),
                tot=i32(jnp.sum(seg, axis=1)), fill_row=i32(region_start + used), fill_len=i32(region - used),
                n_act=i32(n_act.reshape(1)), first_block=i32(region_start // blk), n_blocks=i32(region // blk))


def kernel(x, w_in, w_attn_out, w_conv_out, w_o, conv_w, conv_b, conv_ln_g, conv_ln_b, ln_mix_g, ln_mix_b,
           expert_w_gate_up, expert_w_down, ln_ffn_g, ln_ffn_b, router_w, rel_bias):
    batch, seq, d = x.shape
    depth = w_in.shape[0]
    n = batch * seq
    aw, gw = ATTN_WIDTH, GROUP_WIDTH
    assert d % 256 == 0 and n % 1024 == 0 and seq % PROJ_TILE == 0
    for window, r in DILATION_GROUPS:
        assert window // r == BAND and seq % (r * BAND) == 0
    alpha = (2 * depth) ** 0.25
    scale = HEAD_DIM ** -0.5

    biases = [_band_bias(rel_bias, gi, r) for gi, (_, r) in enumerate(DILATION_GROUPS)]
    rwt = router_w.T.astype(BF16)
    nt = n // TOKEN_TILE
    nb = -(-(2 * n + N_EXPERTS * (nt * (SEG_ALIGN - 1) + ROW_BLOCK - 1)) // ROW_BLOCK)
    dilations = tuple(r for _, r in DILATION_GROUPS if r > 1)

    xf = x.reshape(n, d)
    xb, *xv = _input_views(xf, dilations)
    xviews = dict(zip(dilations, xv))
    for l in range(depth):
        wl = w_in[l]

        def qkv_cols(gi):
            return jnp.concatenate([wl[:, gi * gw:(gi + 1) * gw] * scale,
                                    wl[:, aw + gi * gw:aw + (gi + 1) * gw],
                                    wl[:, 2 * aw + gi * gw:2 * aw + (gi + 1) * gw]], axis=1)

        w0 = jnp.concatenate([wl[:, 3 * aw + d:], wl[:, 3 * aw:3 * aw + d], qkv_cols(0)], axis=1).astype(BF16)
        conv = dict(cw=conv_w[l], cb=conv_b[l][None], clg=conv_ln_g[l][None], clb=conv_ln_b[l][None])
        proj0, hc = _project_and_conv(xb, w0, conv, seq, 2 * d)
        outs, lses = [], []
        for gi, (_, r) in enumerate(DILATION_GROUPS):
            if r == 1:
                qkv, col0 = proj0.reshape(batch, seq, proj0.shape[1]), (3 * d) // gw
            else:
                qkv = _project_dilated(xviews[r], qkv_cols(gi).astype(BF16), batch, seq, r)
                qkv, col0 = qkv.reshape(batch * r, seq // r, 3 * gw), 0
            o, lse = _attention(qkv, biases[gi], batch, seq, r, col0)
            outs.append(o)
            lses.append(lse)

        lw = dict(wao=w_attn_out[l].astype(BF16), wco=w_conv_out[l].astype(BF16), wo=w_o[l].astype(BF16),
                  lng=ln_mix_g[l][None], lnb=ln_mix_b[l][None], rwt=rwt, alpha=alpha)
        x1, x1b, gate, pos, cnt = _mixer_tail(outs, lses, proj0, hc, xf, lw)

        tables = _routing_tables(cnt[:, :, 0].astype(jnp.int32))
        buf = _dispatch(x1b, pos, tables, nb)
        y_buf = _expert_ffn(buf, expert_w_gate_up, expert_w_down, l, tables)
        views = dilations if l + 1 < depth else ()
        xf, xb, *xv = _expert_merge(y_buf, pos.T, gate.T, x1, ln_ffn_g[l][None], ln_ffn_b[l][None], alpha,
                                    tables, views)
        xviews = dict(zip(views, xv))
    return xf.reshape(batch, seq, d)
```

```python
import functools
import math

import jax
import jax.numpy as jnp
from jax import lax
from jax.experimental import pallas as pl
from jax.experimental.pallas import tpu as pltpu

F32 = jnp.float32
BF16 = jnp.bfloat16

HEAD_DIM = 64
HEADS_PER_GROUP = 4
GROUP_WIDTH = HEADS_PER_GROUP * HEAD_DIM
DILATION_GROUPS = ((128, 1), (512, 4), (2048, 16))
N_GROUPS = len(DILATION_GROUPS)
ATTN_WIDTH = N_GROUPS * GROUP_WIDTH
BAND = 128
CONV_KERNEL = 31
CONV_HALO = 32
CONV_TAIL = 16
CONV_ROWS = 64
PROJ_TILE = 512
PROJ_PIECES = 5
N_BUCKETS = 32
MAX_DISTANCE = 2048
N_EXPERTS = 16
EXPERTS_PER_GROUP = 4
LN_EPS = 1e-5
MASKED = -1e30

WEIGHT_DMA_PRIORITY = 1
PROJ_ROWS = 2048
ROW_BLOCK = 256
TOKEN_TILE = 256
MIXER_TILE = 1024
SEG_ALIGN = 8
SORTED_ROWS = -(-(2 * TOKEN_TILE + N_EXPERTS * (SEG_ALIGN - 1)) // 128) * 128
VMEM_LIMIT = 56 * 1024 * 1024


def _sigmoid(v):
    return 1.0 / (1.0 + jnp.exp(-v))


def _layer_norm(z, g, b):
    mu = jnp.mean(z, axis=-1, keepdims=True)
    zc = z - mu
    var = jnp.mean(zc * zc, axis=-1, keepdims=True)
    return zc * lax.rsqrt(var + LN_EPS) * g + b


def _params(*sem):
    return pltpu.CompilerParams(dimension_semantics=sem, vmem_limit_bytes=VMEM_LIMIT)


def _proj_conv_kernel(x_ref, w_ref, cw_ref, cb_ref, clg_ref, clb_ref, o_ref, hc_ref,
                      u_scr, hs_scr, ys_scr, conv_scr, *, tm, nt, seq, u_col):
    s = pl.program_id(0)
    c = hc_ref.shape[1]
    prev, cur = (s + 1) % 2, s % 2

    @pl.when(s == 0)
    def _():
        u_scr[1] = jnp.zeros(u_scr.shape[1:], BF16)
        hs_scr[...] = jnp.zeros(hs_scr.shape, F32)

    u = u_scr[prev].astype(F32)
    seq_start = (jnp.maximum(s - 1, 0) * tm) % seq == 0
    hs_scr[0:CONV_HALO, :] = jnp.where(seq_start, 0.0, hs_scr[tm:tm + CONV_HALO, :])
    hs_scr[CONV_HALO:CONV_HALO + tm, :] = u[:, :c] * _sigmoid(u[:, c:])

    off = CONV_HALO - (CONV_KERNEL - 1)
    rows_y = CONV_ROWS + 8
    def conv_chunk(cc, rc, zero):
        ls = slice(cc * 128, (cc + 1) * 128)
        r0 = rc * CONV_ROWS
        acc = None
        for k in range(8):
            y = None
            for a in range((off + CONV_KERNEL + 7) // 8):
                tap = 8 * a + k - off
                if 0 <= tap < CONV_KERNEL:
                    term = hs_scr[r0 + 8 * a:r0 + 8 * a + rows_y, ls] * cw_ref[0, tap:tap + 1, ls]
                    y = term if y is None else y + term
            if k == 0:
                acc = y[0:CONV_ROWS] + zero
            else:
                ys_scr[k, rc * rows_y:(rc + 1) * rows_y, :] = y
                acc = acc + ys_scr[k, rc * rows_y + k:rc * rows_y + k + CONV_ROWS, :]
        conv_scr[r0:r0 + CONV_ROWS, ls] = acc

    cols = o_ref.shape[1]
    pieces = max(p for p in range(1, PROJ_PIECES + 1) if (cols // 128) % p == 0)
    pw = cols // pieces
    chunks = [(cc, rc) for cc in range(c // 128) for rc in range(tm // CONV_ROWS)]
    per_piece = -(-len(chunks) // pieces)
    for p in range(pieces):
        piece = jnp.dot(x_ref[...], w_ref[:, p * pw:(p + 1) * pw], preferred_element_type=F32)
        o_ref[:, p * pw:(p + 1) * pw] = piece.astype(o_ref.dtype)
        bits = lax.bitcast_convert_type(piece[0:8, 0:128], jnp.uint32)
        zero8 = lax.bitcast_convert_type((bits >> 16) >> 16, F32)
        zero = jnp.concatenate([zero8] * (CONV_ROWS // 8), axis=0)
        for cc, rc in chunks[p * per_piece:(p + 1) * per_piece]:
            conv_chunk(cc, rc, zero)
    hc = _layer_norm(conv_scr[...] + cb_ref[0], clg_ref[0], clb_ref[0])
    hc_ref[...] = (hc * _sigmoid(hc)).astype(BF16)
    u_scr[cur] = o_ref[:, u_col:u_col + 2 * c]


def _layer_rows(a, layer):
    assert a.ndim == 3
    return pl.BlockSpec((1,) + a.shape[1:], lambda *_: (layer, 0, 0))


def _project_and_conv(xb, w, conv, layer, seq, u_col):
    n, d = xb.shape
    cols = w.shape[1]
    c = d // 2
    tm = PROJ_TILE
    nt = n // tm

    def early(s):
        return jnp.minimum(s, nt - 1)

    def late(s):
        return jnp.maximum(s - 1, 0)

    def const(shape):
        return pl.BlockSpec(shape, lambda s: (0,) * len(shape))

    return pl.pallas_call(
        functools.partial(_proj_conv_kernel, tm=tm, nt=nt, seq=seq, u_col=u_col),
        grid=(nt + 1,),
        in_specs=[pl.BlockSpec((tm, d), lambda s: (early(s), 0)), const((d, cols))]
        + [_layer_rows(conv[k], layer) for k in ("cw", "cb", "clg", "clb")],
        out_specs=[pl.BlockSpec((tm, cols), lambda s: (early(s), 0)),
                   pl.BlockSpec((tm, c), lambda s: (late(s), 0))],
        out_shape=[jax.ShapeDtypeStruct((n, cols), BF16), jax.ShapeDtypeStruct((n, c), BF16)],
        scratch_shapes=[pltpu.VMEM((2, tm, 2 * c), BF16),
                        pltpu.VMEM((CONV_HALO + tm + CONV_TAIL, c), F32),
                        pltpu.VMEM((8, (tm // CONV_ROWS) * (CONV_ROWS + 8), 128), F32),
                        pltpu.VMEM((tm, c), F32)],
        compiler_params=_params("arbitrary"),
        name="proj_natural",
    )(xb, w, *[conv[k] for k in ("cw", "cb", "clg", "clb")])


def _project_dilated(xv, w, batch, seq, r):
    d, c = w.shape
    n = xv.shape[0] * r
    sub = seq // r
    tm = min(PROJ_ROWS, sub)
    nl = sub // tm
    cps = max(1, min(r, PROJ_ROWS // sub))
    assert r % cps == 0

    def mm_classes(x_ref, w_ref, o_ref):
        for k in range(cps):
            o_ref[k * tm:(k + 1) * tm, :] = jnp.dot(x_ref[:, k * d:(k + 1) * d], w_ref[...],
                                                    preferred_element_type=F32).astype(o_ref.dtype)

    return pl.pallas_call(
        mm_classes,
        grid=(batch, r // cps, nl),
        in_specs=[pl.BlockSpec((tm, cps * d), lambda b, cg, j: (b * nl + j, cg)),
                  pl.BlockSpec((d, c), lambda b, cg, j: (0, 0))],
        out_specs=pl.BlockSpec((cps * tm, c), lambda b, cg, j: ((b * (r // cps) + cg) * nl + j, 0)),
        out_shape=jax.ShapeDtypeStruct((n, c), BF16),
        compiler_params=_params("arbitrary", "arbitrary", "arbitrary"),
        name=f"proj_dilated_{r}",
    )(xv, w)


def _attn_kernel(q_ref, kp_ref, vp_ref, kc_ref, vc_ref, bias_ref, o_ref, lse_ref, k_scr, v_scr, *, nqb):
    i = pl.program_id(1)
    k_scr[0:BAND, :] = kp_ref[0]
    k_scr[BAND:, :] = kc_ref[0]
    v_scr[0:BAND, :] = vp_ref[0]
    v_scr[BAND:, :] = vc_ref[0]
    col = lax.broadcasted_iota(jnp.int32, (2 * BAND, 2 * BAND), 1)
    first_head = lax.broadcasted_iota(jnp.int32, (BAND, 2 * HEAD_DIM), 1) < HEAD_DIM

    def body(j, carry):
        r0 = pl.multiple_of(j * BAND, BAND)
        q = q_ref[0, pl.ds(r0, BAND), :]
        kk = k_scr[pl.ds(r0, 2 * BAND), :]
        vv = v_scr[pl.ds(r0, 2 * BAND), :]
        no_prev = jnp.logical_and(jnp.logical_and(i == 0, j == 0), col < BAND)
        outs, lses = [], []
        for pair in range(HEADS_PER_GROUP // 2):
            sl = slice(pair * 2 * HEAD_DIM, (pair + 1) * 2 * HEAD_DIM)
            q2, k2, v2 = q[:, sl], kk[:, sl], vv[:, sl]
            zq = jnp.zeros_like(q2)
            qs = jnp.concatenate([jnp.where(first_head, q2, zq), jnp.where(first_head, zq, q2)], axis=0)
            s = lax.dot_general(qs, k2, (((1,), (1,)), ((), ())), preferred_element_type=F32)
            s = s + bias_ref[pair]
            s = jnp.where(no_prev, MASKED, s)
            m = jnp.max(s, axis=-1, keepdims=True)
            p = jnp.exp(s - m)
            den = jnp.sum(p, axis=-1, keepdims=True)
            o = jnp.dot(p.astype(BF16), v2, preferred_element_type=F32) * (1.0 / den)
            lse = jnp.broadcast_to(m + jnp.log(den), (2 * BAND, 2 * HEAD_DIM))
            outs.append(jnp.where(first_head, o[:BAND], o[BAND:]))
            lses.append(jnp.where(first_head, lse[:BAND], lse[BAND:]))
        o_ref[0, pl.ds(r0, BAND), :] = jnp.concatenate(outs, axis=1).astype(o_ref.dtype)
        lse_ref[0, pl.ds(r0, BAND), :] = jnp.concatenate(lses, axis=1)
        return carry

    lax.fori_loop(0, nqb, body, 0, unroll=math.gcd(nqb, 8))


def _attention(qkv, bias, batch, seq, r, col0):
    sub = seq // r
    nqb = min(8, sub // BAND)
    rows = nqb * BAND
    nt = sub // rows
    gw = GROUP_WIDTH

    def cur(c):
        return pl.BlockSpec((1, rows, gw), lambda bc, i: (bc, i, c))

    def prev(c):
        return pl.BlockSpec((1, BAND, gw), lambda bc, i: (bc, jnp.maximum(i * nqb - 1, 0), c))

    out_spec = pl.BlockSpec((1, rows, gw), lambda bc, i: (bc // r, i, bc % r))
    o, lse = pl.pallas_call(
        functools.partial(_attn_kernel, nqb=nqb),
        grid=(batch * r, nt),
        in_specs=[cur(col0), prev(col0 + 1), prev(col0 + 2), cur(col0 + 1), cur(col0 + 2),
                  pl.BlockSpec((HEADS_PER_GROUP // 2, 2 * BAND, 2 * BAND), lambda bc, i: (0, 0, 0))],
        out_specs=[out_spec, out_spec],
        out_shape=[jax.ShapeDtypeStruct((batch, sub, r * gw), BF16),
                   jax.ShapeDtypeStruct((batch, sub, r * gw), F32)],
        scratch_shapes=[pltpu.VMEM((rows + BAND, gw), BF16), pltpu.VMEM((rows + BAND, gw), BF16)],
        compiler_params=_params("arbitrary", "arbitrary"),
        name=f"attn_dilation_{r}",
    )(qkv, qkv, qkv, qkv, qkv, bias.reshape(HEADS_PER_GROUP // 2, 2 * BAND, 2 * BAND))
    return o.reshape(batch * sub, r * gw), lse.reshape(batch * sub, r * gw)


def _t5_bucket(dist):
    max_exact = N_BUCKETS // 2
    n = jnp.maximum(dist, 0)
    nf = jnp.maximum(n, 1).astype(F32)
    large = max_exact + (jnp.log(nf / max_exact) / math.log(MAX_DISTANCE / max_exact)
                         * (N_BUCKETS - max_exact)).astype(jnp.int32)
    large = jnp.minimum(large, N_BUCKETS - 1)
    return jnp.where(n < max_exact, n, large)


def _band_bias(rel_bias, gi, r):
    qi = jnp.arange(BAND)[:, None]
    kj = jnp.arange(2 * BAND)[None, :]
    dist = qi + BAND - kj
    bucket = _t5_bucket(dist * r)
    valid = (dist >= 0) & (dist <= BAND)
    b = jnp.full((HEADS_PER_GROUP, BAND, 2 * BAND), MASKED, F32)
    for k in range(N_BUCKETS):
        row = rel_bias[k, gi * HEADS_PER_GROUP:(gi + 1) * HEADS_PER_GROUP].astype(F32)
        b = jnp.where(((bucket == k) & valid)[None], row[:, None, None], b)
    return b


def _post_kernel(o0_ref, o1_ref, o2_ref, l0_ref, l1_ref, l2_ref, hc_ref, g_ref, x_ref,
                 wao_ref, wco_ref, wo_ref, lng_ref, lnb_ref, rwt_ref,
                 x1_ref, x1b_ref, gate_ref, pos_ref, cnt_ref, nat_scr, *, tm, alpha):
    d = x_ref.shape[1]
    gw = GROUP_WIDTH

    def natural(ref, slot, r):
        if r == 1:
            return ref[...].astype(F32)
        nl = gw // 128
        for cls in range(r):
            for j in range(nl):
                lanes = slice(cls * gw + j * 128, cls * gw + (j + 1) * 128)
                nat_scr[slot * nl + j, pl.ds(cls, tm // r, stride=r), :] = ref[:, lanes].astype(F32)
        return jnp.concatenate([nat_scr[slot * nl + j] for j in range(nl)], axis=1)

    rs = [r for _, r in DILATION_GROUPS]
    o0, o1, o2 = (natural(ref, s, r) for s, (ref, r) in enumerate(zip((o0_ref, o1_ref, o2_ref), rs)))
    l0, l1, l2 = (natural(ref, 3 + s, r) for s, (ref, r) in enumerate(zip((l0_ref, l1_ref, l2_ref), rs)))
    mx = jnp.maximum(jnp.maximum(l0, l1), l2)
    e0, e1, e2 = jnp.exp(l0 - mx), jnp.exp(l1 - mx), jnp.exp(l2 - mx)
    attn = (e0 * o0 + e1 * o1 + e2 * o2) * (1.0 / (e0 + e1 + e2))
    attn_branch = jnp.dot(attn.astype(BF16), wao_ref[...], preferred_element_type=F32)

    conv_branch = jnp.dot(hc_ref[...], wco_ref[...], preferred_element_type=F32)

    gates = _sigmoid(g_ref[...].astype(F32))
    merged = gates[:, :d] * attn_branch + gates[:, d:] * conv_branch
    hmix = jnp.dot(merged.astype(BF16), wo_ref[...], preferred_element_type=F32)
    x1 = _layer_norm(alpha * x_ref[...] + hmix, lng_ref[0], lnb_ref[0])
    x1_ref[...] = x1
    x1b = x1.astype(BF16)
    x1b_ref[...] = x1b

    logits = lax.dot_general(rwt_ref[...], x1b, (((1,), (1,)), ((), ())), preferred_element_type=F32)
    ex = jnp.exp(logits - jnp.max(logits, axis=0, keepdims=True))
    rows = [ex[e:e + 1, :] for e in range(N_EXPERTS)]
    best, gsel = None, None
    for g in range(N_EXPERTS // EXPERTS_PER_GROUP):
        v = rows[g * EXPERTS_PER_GROUP:(g + 1) * EXPERTS_PER_GROUP]
        score = None
        for a in range(EXPERTS_PER_GROUP):
            for b in range(a + 1, EXPERTS_PER_GROUP):
                ps = v[a] + v[b]
                score = ps if score is None else jnp.maximum(score, ps)
        if best is None:
            best, gsel = score, jnp.zeros_like(score, dtype=jnp.int32)
        else:
            upd = score > best
            gsel = jnp.where(upd, g, gsel)
            best = jnp.where(upd, score, best)
    vals = []
    for j in range(EXPERTS_PER_GROUP):
        vj = rows[j]
        for g in range(1, N_EXPERTS // EXPERTS_PER_GROUP):
            vj = jnp.where(gsel == g, rows[g * EXPERTS_PER_GROUP + j], vj)
        vals.append(vj)
    v1, i1 = vals[0], jnp.zeros_like(gsel)
    for j in range(1, EXPERTS_PER_GROUP):
        upd = vals[j] > v1
        i1 = jnp.where(upd, j, i1)
        v1 = jnp.where(upd, vals[j], v1)
    v2, i2 = jnp.full_like(v1, -1.0), jnp.zeros_like(gsel)
    for j in range(EXPERTS_PER_GROUP):
        upd = jnp.logical_and(i1 != j, vals[j] > v2)
        i2 = jnp.where(upd, j, i2)
        v2 = jnp.where(upd, vals[j], v2)
    ea = gsel * EXPERTS_PER_GROUP + i1
    eb = gsel * EXPERTS_PER_GROUP + i2
    inv = 1.0 / (v1 + v2)
    gate_ref[...] = jnp.concatenate([v1 * inv, v2 * inv], axis=0)

    st = TOKEN_TILE
    eid = lax.broadcasted_iota(jnp.int32, (N_EXPERTS, st), 0)
    before = (lax.broadcasted_iota(jnp.int32, (st, st), 0)
              < lax.broadcasted_iota(jnp.int32, (st, st), 1))
    upper = jnp.where(before, 1.0, 0.0).astype(BF16)
    lower = (lax.broadcasted_iota(jnp.int32, (N_EXPERTS, N_EXPERTS), 1)
             < lax.broadcasted_iota(jnp.int32, (N_EXPERTS, N_EXPERTS), 0))
    lower = jnp.where(lower, 1.0, 0.0).astype(BF16)
    for sub in range(tm // st):
        lanes = slice(sub * st, (sub + 1) * st)
        hit_a = eid == ea[:, lanes]
        hit_b = eid == eb[:, lanes]
        onehot = jnp.where(jnp.logical_or(hit_a, hit_b), 1.0, 0.0)
        rank = jnp.dot(onehot.astype(BF16), upper, preferred_element_type=F32)
        count = jnp.sum(onehot, axis=1, keepdims=True)
        seg_len = jnp.floor((count + (SEG_ALIGN - 1)) * (1.0 / SEG_ALIGN)) * SEG_ALIGN
        seg_off = jnp.dot(lower, jnp.broadcast_to(seg_len, (N_EXPERTS, 128)).astype(BF16),
                          preferred_element_type=F32)[:, 0:1]
        row = rank + seg_off
        pa = jnp.sum(jnp.where(hit_a, row, 0.0), axis=0, keepdims=True)
        pb = jnp.sum(jnp.where(hit_b, row, 0.0), axis=0, keepdims=True)
        pos_ref[:, lanes] = jnp.concatenate([pa, pb], axis=0).astype(jnp.int32)
        cnt_ref[sub] = jnp.broadcast_to(count, (N_EXPERTS, 128))


def _mixer_tail(o, lse, proj0, hc, x, lw):
    n, d = x.shape
    tm = MIXER_TILE
    nt = n // tm
    sub = tm // TOKEN_TILE
    gw = GROUP_WIDTH

    def tok(width):
        return pl.BlockSpec((tm, width), lambda i: (i, 0))

    def const(shape):
        return pl.BlockSpec(shape, lambda i: (0,) * len(shape))

    def grouped(r):
        return pl.BlockSpec((tm // r, r * gw), lambda i: (i, 0))

    rs = [r for _, r in DILATION_GROUPS]
    in_specs = (
        [grouped(r) for r in rs] * 2
        + [tok(d // 2),
           pl.BlockSpec((tm, 2 * d), lambda i: (i, 0)),
           tok(d)]
        + [const(lw[k].shape) for k in ("wao", "wco", "wo")]
        + [_layer_rows(lw[k], lw["layer"]) for k in ("lng", "lnb")] + [const(lw["rwt"].shape)]
    )
    lane_row = pl.BlockSpec((2, tm), lambda i: (0, i))
    out_specs = [tok(d), tok(d), lane_row, lane_row,
                 pl.BlockSpec((sub, N_EXPERTS, 128), lambda i: (i, 0, 0))]
    out_shape = [jax.ShapeDtypeStruct((n, d), F32), jax.ShapeDtypeStruct((n, d), BF16),
                 jax.ShapeDtypeStruct((2, n), F32), jax.ShapeDtypeStruct((2, n), jnp.int32),
                 jax.ShapeDtypeStruct((n // TOKEN_TILE, N_EXPERTS, 128), F32)]
    alpha = lw["alpha"]
    return pl.pallas_call(
        functools.partial(_post_kernel, tm=tm, alpha=alpha),
        grid=(nt,),
        in_specs=in_specs,
        out_specs=out_specs,
        out_shape=out_shape,
        scratch_shapes=[pltpu.VMEM((2 * N_GROUPS * (gw // 128), tm, 128), F32)],
        compiler_params=_params("arbitrary"),
        name="mixer_tail",
    )(o[0], o[1], o[2], lse[0], lse[1], lse[2], hc, proj0, x,
      *[lw[k] for k in ("wao", "wco", "wo", "lng", "lnb", "rwt")])


def _pack_halves(v):
    h = v.shape[1] // 2
    bits = lax.bitcast_convert_type(v, jnp.uint32)
    return (bits[:, :h] >> 16) | (bits[:, h:] & jnp.uint32(0xFFFF0000))


def _unpack_halves(w):
    lo = lax.bitcast_convert_type(w << 16, F32).astype(BF16)
    hi = lax.bitcast_convert_type(w & jnp.uint32(0xFFFF0000), F32).astype(BF16)
    return lo, hi


def _dispatch_kernel(dst_ref, off_ref, len_ref, tot_ref, fill_row_ref, fill_len_ref, nact_ref,
                     x_ref, pos_ref, buf_hbm, xs, zeros, sem, zsem, *, nt, nb, blk):
    t = pl.program_id(0)
    cur = t % 2
    srows = xs.shape[1]

    def tile_wait(tile, buf):
        n = pl.multiple_of(tot_ref[tile], SEG_ALIGN)
        pltpu.make_async_copy(xs.at[buf, pl.ds(0, n)], buf_hbm.at[pl.ds(0, n)], sem.at[buf]).wait()

    @pl.when(t >= 2)
    def _():
        tile_wait(t - 2, cur)

    pos = pos_ref[...]
    row = lax.broadcasted_iota(jnp.int32, (srows, pos.shape[1]), 0)
    hit = jnp.logical_or(row == pos[0:1, :], row == pos[1:2, :])
    sel = jnp.where(hit, 1.0, 0.0).astype(BF16)
    xs[cur] = _pack_halves(jnp.dot(sel, x_ref[...], preferred_element_type=F32))

    for e in range(N_EXPERTS):
        k = t * N_EXPERTS + e
        n = pl.multiple_of(len_ref[k], SEG_ALIGN)
        src = pl.multiple_of(off_ref[k], SEG_ALIGN)
        dst = pl.multiple_of(dst_ref[k], SEG_ALIGN)

        @pl.when(n > 0)
        def _():
            pltpu.make_async_copy(xs.at[cur, pl.ds(src, n)], buf_hbm.at[pl.ds(dst, n)], sem.at[cur]).start()

    @pl.when(t == nt - 1)
    def _():
        if nt >= 2:
            tile_wait(t - 1, 1 - cur)
        tile_wait(t, cur)
        zeros[...] = jnp.zeros_like(zeros)

        def region_fill(e):
            n = pl.multiple_of(fill_len_ref[e], SEG_ALIGN)
            dst = pl.multiple_of(fill_row_ref[e], SEG_ALIGN)
            return n, pltpu.make_async_copy(zeros.at[pl.ds(0, n)], buf_hbm.at[pl.ds(dst, n)], zsem)

        def block_fill(b):
            return pltpu.make_async_copy(zeros, buf_hbm.at[pl.ds(pl.multiple_of(b * blk, blk), blk)], zsem)

        def each_block(fn):
            def body(b, carry):
                fn(block_fill(b))
                return carry
            lax.fori_loop(nact_ref[0], nb, body, 0)

        for e in range(N_EXPERTS):
            n, cp = region_fill(e)
            pl.when(n > 0)(cp.start)
        each_block(lambda cp: cp.start())
        for e in range(N_EXPERTS):
            n, cp = region_fill(e)
            pl.when(n > 0)(cp.wait)
        each_block(lambda cp: cp.wait())


def _dispatch(x1b, pos, tables, nb):
    n, d = x1b.shape
    tm = TOKEN_TILE
    nt = n // tm
    blk = ROW_BLOCK
    grid_spec = pltpu.PrefetchScalarGridSpec(
        num_scalar_prefetch=7,
        grid=(nt,),
        in_specs=[pl.BlockSpec((tm, d), lambda t, *_: (t, 0)),
                  pl.BlockSpec((2, tm), lambda t, *_: (0, t))],
        out_specs=pl.BlockSpec(memory_space=pl.ANY),
        scratch_shapes=[pltpu.VMEM((2, SORTED_ROWS, d // 2), jnp.uint32), pltpu.VMEM((blk, d // 2), jnp.uint32),
                        pltpu.SemaphoreType.DMA((2,)), pltpu.SemaphoreType.DMA],
    )
    return pl.pallas_call(
        functools.partial(_dispatch_kernel, nt=nt, nb=nb, blk=blk),
        grid_spec=grid_spec,
        out_shape=jax.ShapeDtypeStruct((nb * blk, d // 2), jnp.uint32),
        compiler_params=_params("arbitrary"),
        name="expert_dispatch",
    )(tables["dst"], tables["off"], tables["len"], tables["tot"], tables["fill_row"], tables["fill_len"],
      tables["n_act"], x1b, pos)


def _ffn_kernel(first_ref, count_ref, nact_ref, xs_hbm, wgu_hbm, wdn_hbm, y_hbm,
                wgu_f, wdn_f, wgu_b, wdn_b, xbuf, obuf, h_scr, wsem, isem, osem, zsem, *, nb, blk, layer):
    e = pl.program_id(0)
    n_exp = pl.num_programs(0)
    f = wdn_f.shape[1]
    half = xbuf.shape[2]
    b0 = first_ref[e]
    n_blocks = count_ref[e]

    def weights(ex, slot):
        return (pltpu.make_async_copy(wgu_hbm.at[layer, ex], wgu_f.at[slot], wsem.at[slot]),
                pltpu.make_async_copy(wdn_hbm.at[layer, ex], wdn_f.at[slot], wsem.at[slot]))

    def fetch_weights(ex):
        for cp in weights(ex, ex % 2):
            cp.start(priority=WEIGHT_DMA_PRIORITY)

    @pl.when(e == 0)
    def _():
        fetch_weights(e)

    for cp in weights(e, e % 2):
        cp.wait()

    @pl.when(e + 1 < n_exp)
    def _():
        fetch_weights(e + 1)

    def rows(j):
        return pl.ds(pl.multiple_of((b0 + j) * blk, blk), blk)

    def load(j, buf):
        return pltpu.make_async_copy(xs_hbm.at[rows(j)], xbuf.at[buf], isem.at[buf])

    def store(j, buf):
        return pltpu.make_async_copy(obuf.at[buf], y_hbm.at[rows(j)], osem.at[buf])

    def prefetch(j):
        @pl.when(j < n_blocks)
        def _():
            load(j, j % 2).start()

    def up(j):
        buf = j % 2
        x_lo, x_hi = _unpack_halves(xbuf[buf])
        h_scr[buf] = (jnp.dot(x_lo, wgu_b[0:half, :], preferred_element_type=F32)
                      + jnp.dot(x_hi, wgu_b[half:, :], preferred_element_type=F32))

    def release(j):
        @pl.when(j >= 2)
        def _():
            store(j - 2, j % 2).wait()

    def down(j):
        buf = j % 2
        a = h_scr[buf, :, 0:f]
        act = (a * _sigmoid(a) * h_scr[buf, :, f:]).astype(BF16)
        y = jnp.dot(act, wdn_b[...], preferred_element_type=F32)
        obuf[buf] = _pack_halves(y.astype(BF16).astype(F32))
        store(j, buf).start()

    @pl.when(n_blocks > 0)
    def _():
        load(0, 0).start()
        wgu_b[...] = wgu_f[e % 2].astype(BF16)
        wdn_b[...] = wdn_f[e % 2].astype(BF16)
        prefetch(1)
        load(0, 0).wait()
        up(0)

    def block(j, carry):
        prefetch(j + 1)
        release(j - 1)
        load(j, j % 2).wait()
        down(j - 1)
        up(j)
        return carry

    lax.fori_loop(1, n_blocks, block, 0)

    @pl.when(n_blocks > 0)
    def _():
        release(n_blocks - 1)
        down(n_blocks - 1)

    @pl.when(n_blocks >= 2)
    def _():
        store(n_blocks - 2, n_blocks % 2).wait()

    @pl.when(n_blocks >= 1)
    def _():
        store(n_blocks - 1, (n_blocks - 1) % 2).wait()

    @pl.when(e == pl.num_programs(0) - 1)
    def _():
        xbuf[0] = jnp.zeros((blk, half), jnp.uint32)

        def fill(b):
            return pltpu.make_async_copy(xbuf.at[0], y_hbm.at[pl.ds(pl.multiple_of(b * blk, blk), blk)], zsem)

        def each(fn):
            def body(b, carry):
                fn(fill(b))
                return carry
            lax.fori_loop(nact_ref[0], nb, body, 0)

        each(lambda cp: cp.start())
        each(lambda cp: cp.wait())


def _expert_ffn(buf, wgu, wdn, layer, tables):
    n_rows, half = buf.shape
    d = 2 * half
    blk = ROW_BLOCK
    nb = n_rows // blk
    f2 = wgu.shape[3]
    f = wdn.shape[2]
    grid_spec = pltpu.PrefetchScalarGridSpec(
        num_scalar_prefetch=3,
        grid=(N_EXPERTS,),
        in_specs=[pl.BlockSpec(memory_space=pl.ANY), pl.BlockSpec(memory_space=pl.ANY),
                  pl.BlockSpec(memory_space=pl.ANY)],
        out_specs=pl.BlockSpec(memory_space=pl.ANY),
        scratch_shapes=[pltpu.VMEM((2, d, f2), F32), pltpu.VMEM((2, f, d), F32),
                        pltpu.VMEM((d, f2), BF16), pltpu.VMEM((f, d), BF16),
                        pltpu.VMEM((2, blk, half), jnp.uint32), pltpu.VMEM((2, blk, half), jnp.uint32),
                        pltpu.VMEM((2, blk, f2), F32),
                        pltpu.SemaphoreType.DMA((2,)), pltpu.SemaphoreType.DMA((2,)),
                        pltpu.SemaphoreType.DMA((2,)), pltpu.SemaphoreType.DMA],
    )
    return pl.pallas_call(
        functools.partial(_ffn_kernel, nb=nb, blk=blk, layer=layer),
        grid_spec=grid_spec,
        out_shape=jax.ShapeDtypeStruct((n_rows, half), jnp.uint32),
        compiler_params=_params("arbitrary"),
        name="expert_ffn",
    )(tables["first_block"], tables["n_blocks"], tables["n_act"], buf, wgu, wdn)


def _store_class_views(xt, scr, view_refs, views):
    tm, d = xt.shape
    if views:
        for j in range(d // 128):
            scr[j] = xt[:, j * 128:(j + 1) * 128]
    for ref, r in zip(view_refs, views):
        for cls in range(r):
            for j in range(d // 128):
                lanes = slice(cls * d + j * 128, cls * d + (j + 1) * 128)
                ref[:, lanes] = scr[j, pl.ds(cls, tm // r, stride=r), :].astype(BF16)


def _views_kernel(x_ref, xb_ref, *rest, views):
    view_refs, scr = rest[:len(views)], rest[len(views)]
    xt = x_ref[...]
    xb_ref[...] = xt.astype(BF16)
    _store_class_views(xt, scr, view_refs, views)


def _input_views(xf, views):
    n, d = xf.shape
    tm = TOKEN_TILE
    tok = pl.BlockSpec((tm, d), lambda t: (t, 0))
    return pl.pallas_call(
        functools.partial(_views_kernel, views=views),
        grid=(n // tm,),
        in_specs=[tok],
        out_specs=[tok] + [pl.BlockSpec((tm // r, r * d), lambda t: (t, 0)) for r in views],
        out_shape=[jax.ShapeDtypeStruct((n, d), BF16)]
        + [jax.ShapeDtypeStruct((n // r, r * d), BF16) for r in views],
        scratch_shapes=[pltpu.VMEM((d // 128, tm, 128), F32)],
        compiler_params=_params("arbitrary"),
        name="input_views",
    )(xf)


def _merge_kernel(dst_ref, off_ref, len_ref, tot_ref, y_hbm, x1_ref, pos_ref, gt_ref, g_ref, b_ref,
                  x2_ref, x2b_ref, *rest, nt, alpha, views):
    view_refs, (ys, x2_scr, sem) = rest[:len(views)], rest[len(views):]
    t = pl.program_id(0)
    cur = t % 2
    tm, d = x1_ref.shape
    srows = ys.shape[1]

    def fetch(tile, buf):
        for e in range(N_EXPERTS):
            k = tile * N_EXPERTS + e
            n = pl.multiple_of(len_ref[k], SEG_ALIGN)
            dst = pl.multiple_of(off_ref[k], SEG_ALIGN)
            src = pl.multiple_of(dst_ref[k], SEG_ALIGN)

            @pl.when(n > 0)
            def _():
                pltpu.make_async_copy(y_hbm.at[pl.ds(src, n)], ys.at[buf, pl.ds(dst, n)], sem.at[buf]).start()

    @pl.when(t == 0)
    def _():
        ys[...] = jnp.zeros_like(ys)
        fetch(t, cur)

    @pl.when(t + 1 < nt)
    def _():
        fetch(t + 1, 1 - cur)

    n_all = pl.multiple_of(tot_ref[t], SEG_ALIGN)
    pltpu.make_async_copy(y_hbm.at[pl.ds(0, n_all)], ys.at[cur, pl.ds(0, n_all)], sem.at[cur]).wait()

    y_lo, y_hi = _unpack_halves(ys[cur])
    pos = pos_ref[...]
    gt = gt_ref[...]
    lane = lax.broadcasted_iota(jnp.int32, (tm, srows), 1)
    sel = (jnp.where(lane == pos[:, 0:1], gt[:, 0:1], 0.0)
           + jnp.where(lane == pos[:, 1:2], gt[:, 1:2], 0.0)).astype(BF16)
    m = jnp.concatenate([jnp.dot(sel, y_lo, preferred_element_type=F32),
                         jnp.dot(sel, y_hi, preferred_element_type=F32)], axis=1)
    x2 = _layer_norm(alpha * x1_ref[...] + m, g_ref[0], b_ref[0])
    x2_ref[...] = x2
    x2b_ref[...] = x2.astype(BF16)
    _store_class_views(x2, x2_scr, view_refs, views)


def _expert_merge(y_buf, pos_t, gate_t, x1, g, b, layer, alpha, tables, views):
    n, d = x1.shape
    tm = TOKEN_TILE
    nt = n // tm
    tok = pl.BlockSpec((tm, d), lambda t, *_: (t, 0))
    pair = pl.BlockSpec((tm, 2), lambda t, *_: (t, 0))
    vec = _layer_rows(g, layer)
    grid_spec = pltpu.PrefetchScalarGridSpec(
        num_scalar_prefetch=4,
        grid=(nt,),
        in_specs=[pl.BlockSpec(memory_space=pl.ANY), tok, pair, pair, vec, vec],
        out_specs=[tok, tok] + [pl.BlockSpec((tm // r, r * d), lambda t, *_: (t, 0)) for r in views],
        scratch_shapes=[pltpu.VMEM((2, SORTED_ROWS, d // 2), jnp.uint32), pltpu.VMEM((d // 128, tm, 128), F32),
                        pltpu.SemaphoreType.DMA((2,))],
    )
    return pl.pallas_call(
        functools.partial(_merge_kernel, nt=nt, alpha=alpha, views=views),
        grid_spec=grid_spec,
        out_shape=[jax.ShapeDtypeStruct((n, d), F32), jax.ShapeDtypeStruct((n, d), BF16)]
        + [jax.ShapeDtypeStruct((n // r, r * d), BF16) for r in views],
        compiler_params=_params("arbitrary"),
        name="expert_merge",
    )(tables["dst"], tables["off"], tables["len"], tables["tot"], y_buf, x1, pos_t, gate_t, g, b)


def _routing_tables(counts):
    blk = ROW_BLOCK
    nt = counts.shape[0]
    seg = (counts + SEG_ALIGN - 1) // SEG_ALIGN * SEG_ALIGN
    off = jnp.cumsum(seg, axis=1) - seg
    used = jnp.sum(seg, axis=0)
    region = (used + blk - 1) // blk * blk
    region_end = jnp.cumsum(region)
    region_start = region_end - region
    dst = region_start[None, :] + jnp.cumsum(seg, axis=0) - seg
    n_act = region_end[-1] // blk
    i32 = lambda v: v.astype(jnp.int32)
    return dict(dst=i32(dst.reshape(-1)), off=i32(off.reshape(-1)), len=i32(seg.reshape(-1)),
                tot=i32(jnp.sum(seg, axis=1)), fill_row=i32(region_start + used), fill_len=i32(region - used),
                n_act=i32(n_act.reshape(1)), first_block=i32(region_start // blk), n_blocks=i32(region // blk))


def kernel(x, w_in, w_attn_out, w_conv_out, w_o, conv_w, conv_b, conv_ln_g, conv_ln_b, ln_mix_g, ln_mix_b,
           expert_w_gate_up, expert_w_down, ln_ffn_g, ln_ffn_b, router_w, rel_bias):
    batch, seq, d = x.shape
    depth = w_in.shape[0]
    n = batch * seq
    aw, gw = ATTN_WIDTH, GROUP_WIDTH
    assert d % 256 == 0 and n % 1024 == 0 and seq % PROJ_TILE == 0
    for window, r in DILATION_GROUPS:
        assert window // r == BAND and seq % (r * BAND) == 0
    alpha = (2 * depth) ** 0.25
    scale = HEAD_DIM ** -0.5

    biases = [_band_bias(rel_bias, gi, r) for gi, (_, r) in enumerate(DILATION_GROUPS)]
    rwt = router_w.T.astype(BF16)
    nt = n // TOKEN_TILE
    nb = -(-(2 * n + N_EXPERTS * (nt * (SEG_ALIGN - 1) + ROW_BLOCK - 1)) // ROW_BLOCK)
    dilations = tuple(r for _, r in DILATION_GROUPS if r > 1)

    xf = x.reshape(n, d)
    xb, *xv = _input_views(xf, dilations)
    xviews = dict(zip(dilations, xv))
    for l in range(depth):
        wl = w_in[l]

        def qkv_cols(gi):
            return jnp.concatenate([wl[:, gi * gw:(gi + 1) * gw] * scale,
                                    wl[:, aw + gi * gw:aw + (gi + 1) * gw],
                                    wl[:, 2 * aw + gi * gw:2 * aw + (gi + 1) * gw]], axis=1)

        w0 = jnp.concatenate([wl[:, 3 * aw + d:], wl[:, 3 * aw:3 * aw + d], qkv_cols(0)], axis=1).astype(BF16)
        conv = dict(cw=conv_w, cb=conv_b[:, None], clg=conv_ln_g[:, None], clb=conv_ln_b[:, None])
        proj0, hc = _project_and_conv(xb, w0, conv, l, seq, 2 * d)
        outs, lses = [], []
        for gi, (_, r) in enumerate(DILATION_GROUPS):
            if r == 1:
                qkv, col0 = proj0.reshape(batch, seq, proj0.shape[1]), (3 * d) // gw
            else:
                qkv = _project_dilated(xviews[r], qkv_cols(gi).astype(BF16), batch, seq, r)
                qkv, col0 = qkv.reshape(batch * r, seq // r, 3 * gw), 0
            o, lse = _attention(qkv, biases[gi], batch, seq, r, col0)
            outs.append(o)
            lses.append(lse)

        lw = dict(wao=w_attn_out[l].astype(BF16), wco=w_conv_out[l].astype(BF16), wo=w_o[l].astype(BF16),
                  lng=ln_mix_g[:, None], lnb=ln_mix_b[:, None], layer=l, rwt=rwt, alpha=alpha)
        x1, x1b, gate, pos, cnt = _mixer_tail(outs, lses, proj0, hc, xf, lw)

        tables = _routing_tables(cnt[:, :, 0].astype(jnp.int32))
        buf = _dispatch(x1b, pos, tables, nb)
        y_buf = _expert_ffn(buf, expert_w_gate_up, expert_w_down, l, tables)
        views = dilations if l + 1 < depth else ()
        xf, xb, *xv = _expert_merge(y_buf, pos.T, gate.T, x1, ln_ffn_g[:, None], ln_ffn_b[:, None], l, alpha,
                                    tables, views)
        xviews = dict(zip(views, xv))
    return xf.reshape(batch, seq, d)
```
